```python
import jax, jax.numpy as jnp
from jax import lax
import numpy as np

D_MODEL = 2048
BATCH = 4
SEQ = 2048
DEPTH = 4

N_MIXERS = 4
HEAD_DIM = 128
N_HEADS = D_MODEL // HEAD_DIM
N_KV_HEADS = 4
GROUP = N_HEADS // N_KV_HEADS
INNER = N_HEADS * HEAD_DIM
KV_WIDTH = N_KV_HEADS * HEAD_DIM
IDX_HEADS = 16
IDX_DIM = 64
DSA_TOPK = 256
MOBA_BLOCK = 256
MOBA_TOPK = 3
RET_HEADS = 8
RET_QK_DIM = D_MODEL // RET_HEADS
RET_V_DIM = 2 * RET_QK_DIM
RET_INNER = RET_HEADS * RET_V_DIM
RET_CHUNK = 128
Q_BLOCK = 128
ROPE_THETA = 10000.0
EPS = 1e-6
NEG = -1e30

A_WIDTHS = (INNER, KV_WIDTH, KV_WIDTH, INNER, IDX_HEADS * IDX_DIM, IDX_DIM, IDX_HEADS)
B_WIDTHS = (INNER, KV_WIDTH, KV_WIDTH, INNER)
C_WIDTHS = (RET_HEADS * RET_QK_DIM, RET_HEADS * RET_QK_DIM, RET_INNER, RET_INNER)
D_WIDTHS = (INNER, INNER, INNER, INNER, N_HEADS)

kernel_name = "hybrid_dsa_moba_retnet_fox_trunk"


def n_layers_of(m):
    return len(range(m, DEPTH, N_MIXERS))


def split_cols(z, widths):
    cuts = [int(c) for c in np.cumsum(widths)[:-1]]
    return jnp.split(z, cuts, axis=-1)


def rms_norm(x, g):
    xf = x.astype(jnp.float32)
    y = xf * lax.rsqrt(jnp.mean(xf * xf, axis=-1, keepdims=True) + EPS)
    return (y * g.astype(jnp.float32)).astype(x.dtype)


def rope(x, pos):
    d = x.shape[-1]
    inv = ROPE_THETA ** (-jnp.arange(0, d, 2, dtype=jnp.float32) / d)
    ang = pos.astype(jnp.float32)[:, None] * inv[None, :]
    cos = jnp.cos(ang)[None, :, None, :]
    sin = jnp.sin(ang)[None, :, None, :]
    x1, x2 = jnp.split(x.astype(jnp.float32), 2, axis=-1)
    return jnp.concatenate([x1 * cos - x2 * sin, x1 * sin + x2 * cos], axis=-1).astype(x.dtype)


def sweep_query_blocks(fn, T):
    starts = jnp.arange(T // Q_BLOCK, dtype=jnp.int32) * Q_BLOCK
    out = lax.map(fn, starts)
    _, B, _, w = out.shape
    return jnp.swapaxes(out, 0, 1).reshape(B, T, w)


def batched_gather(src, idx):
    return jax.vmap(lambda s, i: s[i])(src, idx)


def dsa_mixer(h, w_in, q_g, k_g, w_out):
    B, T, _ = h.shape
    pos = jnp.arange(T, dtype=jnp.int32)
    q, k, v, gate, qi, ki, wi = split_cols(h @ w_in, A_WIDTHS)
    q = rope(rms_norm(q.reshape(B, T, N_HEADS, HEAD_DIM), q_g), pos)
    k = rope(rms_norm(k.reshape(B, T, N_KV_HEADS, HEAD_DIM), k_g), pos)
    v = v.reshape(B, T, N_KV_HEADS, HEAD_DIM)
    qi = rope(qi.reshape(B, T, IDX_HEADS, IDX_DIM), pos).astype(jnp.float32)
    ki = rope(ki.reshape(B, T, 1, IDX_DIM), pos)[:, :, 0].astype(jnp.float32)
    wi = wi.astype(jnp.float32) * IDX_HEADS ** -0.5
    n_keep = min(DSA_TOPK, T // 4)
    scale = HEAD_DIM ** -0.5

    def block(start):
        tq = start + jnp.arange(Q_BLOCK, dtype=jnp.int32)
        qib = lax.dynamic_slice_in_dim(qi, start, Q_BLOCK, axis=1)
        wib = lax.dynamic_slice_in_dim(wi, start, Q_BLOCK, axis=1)
        rel = jax.nn.relu(jnp.einsum('bqhd,bsd->bqhs', qib, ki) * IDX_DIM ** -0.5)
        score = jnp.einsum('bqh,bqhs->bqs', wib, rel)
        causal = pos[None, :] <= tq[:, None]
        score = jnp.where(causal[None], score, NEG)
        _, sel = lax.top_k(score, n_keep)
        valid = sel <= tq[None, :, None]
        k_sel = batched_gather(k, sel)
        v_sel = batched_gather(v, sel)
        qb = lax.dynamic_slice_in_dim(q, start, Q_BLOCK, axis=1).reshape(B, Q_BLOCK, N_KV_HEADS, GROUP, HEAD_DIM)
        logits = jnp.einsum('bqgrd,bqkgd->bqgrk', qb, k_sel).astype(jnp.float32) * scale
        logits = jnp.where(valid[:, :, None, None, :], logits, NEG)
        p = jax.nn.softmax(logits, axis=-1).astype(v.dtype)
        o = jnp.einsum('bqgrk,bqkgd->bqgrd', p, v_sel)
        return o.reshape(B, Q_BLOCK, INNER)

    o = sweep_query_blocks(block, T)
    return (jax.nn.silu(gate) * o) @ w_out


def moba_mixer(h, w_in, q_g, k_g, w_out):
    B, T, _ = h.shape
    pos = jnp.arange(T, dtype=jnp.int32)
    q, k, v, gate = split_cols(h @ w_in, B_WIDTHS)
    q = rope(rms_norm(q.reshape(B, T, N_HEADS, HEAD_DIM), q_g), pos)
    k = rope(rms_norm(k.reshape(B, T, N_KV_HEADS, HEAD_DIM), k_g), pos)
    v = v.reshape(B, T, N_KV_HEADS, HEAD_DIM)
    n_blocks = -(-T // MOBA_BLOCK)
    pad = n_blocks * MOBA_BLOCK - T
    kb = jnp.pad(k, ((0, 0), (0, pad), (0, 0), (0, 0))).reshape(B, n_blocks, MOBA_BLOCK, N_KV_HEADS, HEAD_DIM)
    vb = jnp.pad(v, ((0, 0), (0, pad), (0, 0), (0, 0))).reshape(B, n_blocks, MOBA_BLOCK, N_KV_HEADS, HEAD_DIM)
    k_mean = jnp.mean(kb.astype(jnp.float32), axis=2)
    kbt = kb.transpose(0, 3, 1, 2, 4)
    vbt = vb.transpose(0, 3, 1, 2, 4)
    n_sel = min(MOBA_TOPK, n_blocks - 1)
    gather2 = jax.vmap(jax.vmap(lambda s, i: s[i]))
    scale = HEAD_DIM ** -0.5

    def block(start):
        tq = start + jnp.arange(Q_BLOCK, dtype=jnp.int32)
        own = start // MOBA_BLOCK
        qb = lax.dynamic_slice_in_dim(q, start, Q_BLOCK, axis=1).reshape(B, Q_BLOCK, N_KV_HEADS, GROUP, HEAD_DIM)
        k_own = lax.dynamic_index_in_dim(kb, own, axis=1, keepdims=False)
        v_own = lax.dynamic_index_in_dim(vb, own, axis=1, keepdims=False)
        own_pos = own * MOBA_BLOCK + jnp.arange(MOBA_BLOCK, dtype=jnp.int32)
        l_own = jnp.einsum('bqgrd,bsgd->bqgrs', qb, k_own).astype(jnp.float32) * scale
        l_own = jnp.where((own_pos[None, :] <= tq[:, None])[None, :, None, None, :], l_own, NEG)
        if n_sel == 0:
            p = jax.nn.softmax(l_own, axis=-1).astype(v.dtype)
            o = jnp.einsum('bqgrs,bsgd->bqgrd', p, v_own)
        else:
            gscore = jnp.einsum('bqgrd,bngd->bqgn', qb.astype(jnp.float32), k_mean)
            past = jnp.arange(n_blocks, dtype=jnp.int32) < own
            gscore = jnp.where(past, gscore, NEG)
            _, sel = lax.top_k(gscore, n_sel)
            valid = sel < own
            sel_t = jnp.swapaxes(sel, 1, 2)
            k_sel = gather2(kbt, sel_t)
            v_sel = gather2(vbt, sel_t)
            l_sel = jnp.einsum('bqgrd,bgqnsd->bqgrns', qb, k_sel).astype(jnp.float32) * scale
            l_sel = jnp.where(valid[:, :, :, None, :, None], l_sel, NEG)
            n_k = n_sel * MOBA_BLOCK
            logits = jnp.concatenate([l_sel.reshape(B, Q_BLOCK, N_KV_HEADS, GROUP, n_k), l_own], axis=-1)
            p = jax.nn.softmax(logits, axis=-1).astype(v.dtype)
            p_sel = p[..., :n_k].reshape(B, Q_BLOCK, N_KV_HEADS, GROUP, n_sel, MOBA_BLOCK)
            p_own = p[..., n_k:]
            o = (jnp.einsum('bqgrns,bgqnsd->bqgrd', p_sel, v_sel)
                 + jnp.einsum('bqgrs,bsgd->bqgrd', p_own, v_own))
        return o.reshape(B, Q_BLOCK, INNER)

    o = sweep_query_blocks(block, T)
    return (jax.nn.silu(gate) * o) @ w_out


def retention_mixer(h, w_in, gn_g, w_out):
    B, T, _ = h.shape
    pos = jnp.arange(T, dtype=jnp.int32)
    q, k, v, gate = split_cols(h @ w_in, C_WIDTHS)
    q = rope(q.reshape(B, T, RET_HEADS, RET_QK_DIM), pos).astype(jnp.float32)
    k = rope(k.reshape(B, T, RET_HEADS, RET_QK_DIM), pos).astype(jnp.float32) * RET_QK_DIM ** -0.5
    v = v.reshape(B, T, RET_HEADS, RET_V_DIM).astype(jnp.float32)
    log_gamma = jnp.log(1.0 - 2.0 ** (-5.0 - jnp.arange(RET_HEADS, dtype=jnp.float32)))
    i = jnp.arange(RET_CHUNK, dtype=jnp.float32)
    diff = i[:, None] - i[None, :]
    decay_mask = jnp.where(diff >= 0, jnp.exp(jnp.maximum(diff, 0.0)[None] * log_gamma[:, None, None]), 0.0)
    q_decay = jnp.exp((i[:, None] + 1.0) * log_gamma[None, :])
    k_decay = jnp.exp((RET_CHUNK - 1.0 - i[:, None]) * log_gamma[None, :])
    chunk_decay = jnp.exp(RET_CHUNK * log_gamma)
    n_chunks = T // RET_CHUNK

    def to_chunks(a):
        return jnp.swapaxes(a.reshape(B, n_chunks, RET_CHUNK, *a.shape[2:]), 0, 1)

    def step(state, inp):
        qc, kc, vc = inp
        inner = jnp.einsum('bihd,bjhd->bhij', qc, kc) * decay_mask[None]
        o = (jnp.einsum('bhij,bjhv->bihv', inner, vc)
             + jnp.einsum('bihd,bhdv->bihv', qc * q_decay[None, :, :, None], state))
        state = (state * chunk_decay[None, :, None, None]
                 + jnp.einsum('bjhd,bjhv->bhdv', kc * k_decay[None, :, :, None], vc))
        return state, o

    s0 = jnp.zeros((B, RET_HEADS, RET_QK_DIM, RET_V_DIM), jnp.float32)
    _, o = lax.scan(step, s0, (to_chunks(q), to_chunks(k), to_chunks(v)))
    o = jnp.swapaxes(o, 0, 1).reshape(B, T, RET_HEADS, RET_V_DIM)
    mu = jnp.mean(o, axis=-1, keepdims=True)
    var = jnp.mean(jnp.square(o - mu), axis=-1, keepdims=True)
    o = ((o - mu) * lax.rsqrt(var + EPS)).reshape(B, T, RET_INNER) * gn_g.astype(jnp.float32)
    return (jax.nn.silu(gate) * o.astype(h.dtype)) @ w_out


def fox_mixer(h, w_in, f_bias, q_g, k_g, w_out):
    B, T, _ = h.shape
    pos = jnp.arange(T, dtype=jnp.int32)
    q, k, v, gate, f_logit = split_cols(h @ w_in, D_WIDTHS)
    q = rms_norm(q.reshape(B, T, N_HEADS, HEAD_DIM), q_g)
    k = rms_norm(k.reshape(B, T, N_HEADS, HEAD_DIM), k_g)
    v = v.reshape(B, T, N_HEADS, HEAD_DIM)
    log_f = jax.nn.log_sigmoid((f_logit + f_bias).astype(jnp.float32))
    cum_t = jnp.swapaxes(lax.cumsum(log_f, axis=1), 1, 2)
    scale = HEAD_DIM ** -0.5

    def block(start):
        tq = start + jnp.arange(Q_BLOCK, dtype=jnp.int32)
        qb = lax.dynamic_slice_in_dim(q, start, Q_BLOCK, axis=1)
        cq = lax.dynamic_slice_in_dim(cum_t, start, Q_BLOCK, axis=2)
        logits = (jnp.einsum('bqhd,bshd->bhqs', qb, k).astype(jnp.float32) * scale
                  + cq[..., None] - cum_t[:, :, None, :])
        logits = jnp.where((pos[None, :] <= tq[:, None])[None, None], logits, NEG)
        p = jax.nn.softmax(logits, axis=-1).astype(v.dtype)
        o = jnp.einsum('bhqs,bshd->bqhd', p, v)
        return o.reshape(B, Q_BLOCK, INNER)

    o = sweep_query_blocks(block, T)
    return (jax.nn.silu(gate) * o) @ w_out


def setup_inputs(seed: int = 0) -> dict:
    key = jax.random.key(seed)
    ks = jax.random.split(key, 21)
    nA, nB, nC, nD = (n_layers_of(m) for m in range(N_MIXERS))

    def dense(k, shape, fan_in):
        return jax.random.normal(k, shape, jnp.float32) * fan_in ** -0.5

    def gain(k, shape):
        return 1.0 + 0.02 * jax.random.normal(k, shape, jnp.float32)

    return {
        'x': jax.random.normal(ks[0], (BATCH, SEQ, D_MODEL), jnp.float32),
        'a_norm': gain(ks[1], (nA, D_MODEL)),
        'a_w_in': dense(ks[2], (nA, D_MODEL, sum(A_WIDTHS)), D_MODEL),
        'a_q_norm': gain(ks[3], (nA, HEAD_DIM)),
        'a_k_norm': gain(ks[4], (nA, HEAD_DIM)),
        'a_w_out': dense(ks[5], (nA, INNER, D_MODEL), INNER),
        'b_norm': gain(ks[6], (nB, D_MODEL)),
        'b_w_in': dense(ks[7], (nB, D_MODEL, sum(B_WIDTHS)), D_MODEL),
        'b_q_norm': gain(ks[8], (nB, HEAD_DIM)),
        'b_k_norm': gain(ks[9], (nB, HEAD_DIM)),
        'b_w_out': dense(ks[10], (nB, INNER, D_MODEL), INNER),
        'c_norm': gain(ks[11], (nC, D_MODEL)),
        'c_w_in': dense(ks[12], (nC, D_MODEL, sum(C_WIDTHS)), D_MODEL),
        'c_gn': gain(ks[13], (nC, RET_INNER)),
        'c_w_out': dense(ks[14], (nC, RET_INNER, D_MODEL), RET_INNER),
        'd_norm': gain(ks[15], (nD, D_MODEL)),
        'd_w_in': dense(ks[16], (nD, D_MODEL, sum(D_WIDTHS)), D_MODEL),
        'd_f_bias': jax.random.uniform(ks[17], (nD, N_HEADS), jnp.float32, minval=1.0, maxval=4.0),
        'd_q_norm': gain(ks[18], (nD, HEAD_DIM)),
        'd_k_norm': gain(ks[19], (nD, HEAD_DIM)),
        'd_w_out': dense(ks[20], (nD, INNER, D_MODEL), INNER),
    }


def reference(x, a_norm, a_w_in, a_q_norm, a_k_norm, a_w_out,
              b_norm, b_w_in, b_q_norm, b_k_norm, b_w_out,
              c_norm, c_w_in, c_gn, c_w_out,
              d_norm, d_w_in, d_f_bias, d_q_norm, d_k_norm, d_w_out):
    h = x
    for i in range(DEPTH):
        m = i % N_MIXERS
        j = i // N_MIXERS
        if m == 0:
            h = h + dsa_mixer(rms_norm(h, a_norm[j]), a_w_in[j], a_q_norm[j], a_k_norm[j], a_w_out[j])
        elif m == 1:
            h = h + moba_mixer(rms_norm(h, b_norm[j]), b_w_in[j], b_q_norm[j], b_k_norm[j], b_w_out[j])
        elif m == 2:
            h = h + retention_mixer(rms_norm(h, c_norm[j]), c_w_in[j], c_gn[j], c_w_out[j])
        else:
            h = h + fox_mixer(rms_norm(h, d_norm[j]), d_w_in[j], d_f_bias[j], d_q_norm[j], d_k_norm[j], d_w_out[j])
    return h
```

```python
import functools

import jax
import jax.numpy as jnp
import numpy as np
from jax import lax
from jax.experimental import pallas as pl
from jax.experimental.pallas import tpu as pltpu

F32 = jnp.float32
BF16 = jnp.bfloat16
I32 = jnp.int32

D_MODEL = 2048
HEAD_DIM = 128
N_HEADS = 16
N_KV_HEADS = 4
GROUP = N_HEADS // N_KV_HEADS
INNER = N_HEADS * HEAD_DIM
KV_WIDTH = N_KV_HEADS * HEAD_DIM
IDX_HEADS = 16
IDX_DIM = 64
DSA_TOPK = 256
MOBA_BLOCK = 256
MOBA_TOPK = 3
RET_HEADS = 8
RET_QK_DIM = 256
RET_V_DIM = 512
RET_INNER = RET_HEADS * RET_V_DIM
ROPE_THETA = 10000.0
EPS = 1e-6
NEG = -1e30
INT_MIN = -(2 ** 31)

LANE = 128
TQ = 256
TK = 256
RET_CHUNK = 256
VMEM_LIMIT = 56 * 1024 * 1024


def _cparams(*sem):
    return pltpu.CompilerParams(dimension_semantics=sem, vmem_limit_bytes=VMEM_LIMIT)


def _rmsnorm_kernel(x_ref, g_ref, o_ref):
    x = x_ref[...]
    ms = jnp.mean(x * x, axis=-1, keepdims=True)
    o_ref[...] = (x * lax.rsqrt(ms + EPS) * g_ref[...]).astype(o_ref.dtype)


def _rmsnorm(x, g, tm=512):
    m, d = x.shape
    return pl.pallas_call(
        _rmsnorm_kernel,
        grid=(m // tm,),
        in_specs=[pl.BlockSpec((tm, d), lambda i: (i, 0)),
                  pl.BlockSpec((1, d), lambda i: (0, 0))],
        out_specs=pl.BlockSpec((tm, d), lambda i: (i, 0)),
        out_shape=jax.ShapeDtypeStruct((m, d), BF16),
        compiler_params=_cparams("parallel"),
        name="rmsnorm",
    )(x, g.reshape(1, d))


def _mm_kernel(a_ref, w_ref, o_ref):
    o_ref[...] = jnp.dot(a_ref[...], w_ref[...], preferred_element_type=F32).astype(o_ref.dtype)


def _mm_res_kernel(a_ref, w_ref, r_ref, o_ref):
    o_ref[...] = r_ref[...] + jnp.dot(a_ref[...], w_ref[...], preferred_element_type=F32)


def _matmul(a, w, tm, tn, name):
    m, k = a.shape
    n = w.shape[1]
    return pl.pallas_call(
        _mm_kernel,
        grid=(m // tm, n // tn),
        in_specs=[pl.BlockSpec((tm, k), lambda i, j: (i, 0)),
                  pl.BlockSpec((k, tn), lambda i, j: (0, j))],
        out_specs=pl.BlockSpec((tm, tn), lambda i, j: (i, j)),
        out_shape=jax.ShapeDtypeStruct((m, n), F32),
        compiler_params=_cparams("parallel", "parallel"),
        name=name,
    )(a, w)


def _matmul_residual(a, w, res, tm, tn, name):
    m, k = a.shape
    n = w.shape[1]
    return pl.pallas_call(
        _mm_res_kernel,
        grid=(m // tm, n // tn),
        in_specs=[pl.BlockSpec((tm, k), lambda i, j: (i, 0)),
                  pl.BlockSpec((k, tn), lambda i, j: (0, j)),
                  pl.BlockSpec((tm, tn), lambda i, j: (i, j))],
        out_specs=pl.BlockSpec((tm, tn), lambda i, j: (i, j)),
        out_shape=jax.ShapeDtypeStruct((m, n), F32),
        compiler_params=_cparams("parallel", "parallel"),
        name=name,
    )(a, w, res)


def _weight_bf16(w, n_pad=None):
    w = w.astype(BF16)
    if n_pad is not None and n_pad > w.shape[1]:
        w = jnp.pad(w, ((0, 0), (0, n_pad - w.shape[1])))
    return w


def _rope_tables(t, d, reps):
    pos = jnp.arange(t, dtype=jnp.int32)
    inv = ROPE_THETA ** (-jnp.arange(0, d, 2, dtype=F32) / d)
    ang = pos.astype(F32)[:, None] * inv[None, :]
    cos, sin = jnp.cos(ang), jnp.sin(ang)
    cos_t = jnp.tile(jnp.concatenate([cos, cos], axis=-1), (1, reps))
    sin_t = jnp.tile(jnp.concatenate([-sin, sin], axis=-1), (1, reps))
    return cos_t, sin_t


def _head_norm(x, g):
    ms = jnp.mean(x * x, axis=-1, keepdims=True)
    return x * lax.rsqrt(ms + EPS) * g


def _rope128(y, cos, sin):
    return y * cos + pltpu.roll(y, 64, 1) * sin


def _qkv_prep_body(q_ref, k_ref, v_ref, cos_ref, sin_ref, qg_ref, kg_ref,
                   qT_ref, ko_ref, vT_ref, *, n_kv, rope):
    qg = qg_ref[...]
    kg = kg_ref[...]
    scale = HEAD_DIM ** -0.5
    if rope:
        cos = cos_ref[...]
        sin = sin_ref[...]
    for h in range(N_HEADS):
        sl = slice(h * HEAD_DIM, (h + 1) * HEAD_DIM)
        y = _head_norm(q_ref[:, sl], qg)
        if rope:
            y = _rope128(y, cos, sin)
        qT_ref[0, sl, :] = (y * scale).T.astype(BF16)
    k_out = []
    for g in range(n_kv):
        sl = slice(g * HEAD_DIM, (g + 1) * HEAD_DIM)
        y = _head_norm(k_ref[:, sl], kg)
        if rope:
            y = _rope128(y, cos, sin)
        ko_ref[0, :, sl] = y.astype(BF16)
        vT_ref[0, sl, :] = v_ref[:, sl].T.astype(BF16)
        k_out.append(y)
    return k_out


def _dsa_prep_kernel(q_ref, k_ref, v_ref, qi_ref, tail_ref, cos_ref, sin_ref, cos64_ref, sin64_ref,
                     qg_ref, kg_ref, qT_ref, ko_ref, vT_ref, qiT_ref, kia_ref, kib_ref, wT_ref):
    _qkv_prep_body(q_ref, k_ref, v_ref, cos_ref, sin_ref, qg_ref, kg_ref, qT_ref, ko_ref, vT_ref,
                   n_kv=N_KV_HEADS, rope=True)
    tm = tail_ref.shape[0]
    lane = lax.broadcasted_iota(I32, (tm, LANE), 1)
    first_half = (lane % IDX_DIM) < (IDX_DIM // 2)
    c64 = cos64_ref[...]
    s64 = sin64_ref[...]

    def rope64(x):
        rot = jnp.where(first_half, pltpu.roll(x, LANE - IDX_DIM // 2, 1), pltpu.roll(x, IDX_DIM // 2, 1))
        return x * c64 + rot * s64

    for j in range(IDX_HEADS * IDX_DIM // LANE):
        sl = slice(j * LANE, (j + 1) * LANE)
        y = rope64(qi_ref[:, sl]) * (IDX_DIM ** -0.5)
        qiT_ref[0, sl, :] = y.T.astype(BF16)
    tail = tail_ref[...]
    ka = jnp.where(lane < IDX_DIM, rope64(tail), 0.0)
    kia_ref[0] = ka.astype(BF16)
    kib_ref[0] = pltpu.roll(ka, IDX_DIM, 1).astype(BF16)
    wT_ref[0] = (tail * (IDX_HEADS ** -0.5)).T


def _moba_prep_kernel(q_ref, k_ref, v_ref, cos_ref, sin_ref, qg_ref, kg_ref,
                      qT_ref, ko_ref, vT_ref, kmean_ref):
    k_out = _qkv_prep_body(q_ref, k_ref, v_ref, cos_ref, sin_ref, qg_ref, kg_ref, qT_ref, ko_ref, vT_ref,
                           n_kv=N_KV_HEADS, rope=True)
    for g in range(N_KV_HEADS):
        km = jnp.mean(k_out[g], axis=0, keepdims=True)
        kmean_ref[0, g, pl.ds(pl.program_id(1), 1), :] = jnp.concatenate([km] * GROUP, axis=1)


def _fox_prep_kernel(q_ref, k_ref, v_ref, f_ref, fb_ref, qg_ref, kg_ref,
                     qT_ref, ko_ref, vT_ref, cum_ref, cumT_ref, carry_ref):
    _qkv_prep_body(q_ref, k_ref, v_ref, None, None, qg_ref, kg_ref, qT_ref, ko_ref, vT_ref,
                   n_kv=N_HEADS, rope=False)

    @pl.when(pl.program_id(1) == 0)
    def _():
        carry_ref[...] = jnp.zeros_like(carry_ref)

    tm = f_ref.shape[0]
    x = f_ref[...] + fb_ref[...]
    lf = jnp.minimum(x, 0.0) - jnp.log(1.0 + jnp.exp(-jnp.abs(x)))
    hi = lf.astype(BF16)
    r1 = lf - hi.astype(F32)
    lo = r1.astype(BF16)
    lo2 = (r1 - lo.astype(F32)).astype(BF16)
    row = lax.broadcasted_iota(I32, (tm, tm), 0)
    col = lax.broadcasted_iota(I32, (tm, tm), 1)
    tri = (col <= row).astype(BF16)
    parts = jnp.dot(tri, jnp.concatenate([hi, lo, lo2], axis=1), preferred_element_type=F32)
    cum = parts[:, :LANE] + parts[:, LANE:2 * LANE] + parts[:, 2 * LANE:] + carry_ref[...]
    carry_ref[...] = cum[tm - 1:tm, :]
    cum_ref[0] = cum
    cumT_ref[0] = cum.T


def _prep_specs(b_count, t, tm, z_q_blk, z_k_blk, z_v_blk, kv_width):
    nt = t // tm
    row = lambda b, i: b * nt + i
    in_specs = [
        pl.BlockSpec((tm, INNER), lambda b, i: (row(b, i), z_q_blk)),
        pl.BlockSpec((tm, kv_width), lambda b, i: (row(b, i), z_k_blk)),
        pl.BlockSpec((tm, kv_width), lambda b, i: (row(b, i), z_v_blk)),
    ]
    out_specs = [
        pl.BlockSpec((1, INNER, tm), lambda b, i: (b, 0, i)),
        pl.BlockSpec((1, tm, kv_width), lambda b, i: (b, i, 0)),
        pl.BlockSpec((1, kv_width, tm), lambda b, i: (b, 0, i)),
    ]
    out_shape = [
        jax.ShapeDtypeStruct((b_count, INNER, t), BF16),
        jax.ShapeDtypeStruct((b_count, t, kv_width), BF16),
        jax.ShapeDtypeStruct((b_count, kv_width, t), BF16),
    ]
    return row, in_specs, out_specs, out_shape


def _dsa_prep(z, b_count, t, q_g, k_g, tm=256):
    row, in_specs, out_specs, out_shape = _prep_specs(b_count, t, tm, 0, INNER // KV_WIDTH,
                                                      INNER // KV_WIDTH + 1, KV_WIDTH)
    qi_w = IDX_HEADS * IDX_DIM
    qi_off = 2 * INNER + 2 * KV_WIDTH
    cos, sin = _rope_tables(t, HEAD_DIM, 1)
    cos64, sin64 = _rope_tables(t, IDX_DIM, LANE // IDX_DIM)
    tab = pl.BlockSpec((tm, LANE), lambda b, i: (i, 0))
    gain = pl.BlockSpec((1, HEAD_DIM), lambda b, i: (0, 0))
    in_specs += [
        pl.BlockSpec((tm, qi_w), lambda b, i: (row(b, i), qi_off // qi_w)),
        pl.BlockSpec((tm, LANE), lambda b, i: (row(b, i), (qi_off + qi_w) // LANE)),
        tab, tab, tab, tab, gain, gain,
    ]
    out_specs += [
        pl.BlockSpec((1, qi_w, tm), lambda b, i: (b, 0, i)),
        pl.BlockSpec((1, tm, LANE), lambda b, i: (b, i, 0)),
        pl.BlockSpec((1, tm, LANE), lambda b, i: (b, i, 0)),
        pl.BlockSpec((1, LANE, tm), lambda b, i: (b, 0, i)),
    ]
    out_shape += [
        jax.ShapeDtypeStruct((b_count, qi_w, t), BF16),
        jax.ShapeDtypeStruct((b_count, t, LANE), BF16),
        jax.ShapeDtypeStruct((b_count, t, LANE), BF16),
        jax.ShapeDtypeStruct((b_count, LANE, t), F32),
    ]
    return pl.pallas_call(
        _dsa_prep_kernel,
        grid=(b_count, t // tm),
        in_specs=in_specs, out_specs=out_specs, out_shape=out_shape,
        compiler_params=_cparams("parallel", "parallel"),
        name="dsa_prep",
    )(z, z, z, z, z, cos, sin, cos64, sin64, q_g.reshape(1, HEAD_DIM), k_g.reshape(1, HEAD_DIM))


def _moba_prep(z, b_count, t, q_g, k_g):
    tm = MOBA_BLOCK
    row, in_specs, out_specs, out_shape = _prep_specs(b_count, t, tm, 0, INNER // KV_WIDTH,
                                                      INNER // KV_WIDTH + 1, KV_WIDTH)
    cos, sin = _rope_tables(t, HEAD_DIM, 1)
    tab = pl.BlockSpec((tm, LANE), lambda b, i: (i, 0))
    gain = pl.BlockSpec((1, HEAD_DIM), lambda b, i: (0, 0))
    in_specs += [tab, tab, gain, gain]
    out_specs += [pl.BlockSpec((1, N_KV_HEADS, t // tm, GROUP * HEAD_DIM), lambda b, i: (b, 0, 0, 0))]
    out_shape += [jax.ShapeDtypeStruct((b_count, N_KV_HEADS, t // tm, GROUP * HEAD_DIM), F32)]
    return pl.pallas_call(
        _moba_prep_kernel,
        grid=(b_count, t // tm),
        in_specs=in_specs, out_specs=out_specs, out_shape=out_shape,
        compiler_params=_cparams("parallel", "arbitrary"),
        name="moba_prep",
    )(z, z, z, cos, sin, q_g.reshape(1, HEAD_DIM), k_g.reshape(1, HEAD_DIM))


def _fox_prep(z, b_count, t, f_bias, q_g, k_g, tm=256):
    row, in_specs, out_specs, out_shape = _prep_specs(b_count, t, tm, 0, 1, 2, INNER)
    gain = pl.BlockSpec((1, HEAD_DIM), lambda b, i: (0, 0))
    fb = jnp.pad(f_bias.reshape(1, N_HEADS), ((0, 0), (0, LANE - N_HEADS)))
    in_specs += [
        pl.BlockSpec((tm, LANE), lambda b, i: (row(b, i), 4 * INNER // LANE)),
        pl.BlockSpec((1, LANE), lambda b, i: (0, 0)),
        gain, gain,
    ]
    out_specs += [
        pl.BlockSpec((1, tm, LANE), lambda b, i: (b, i, 0)),
        pl.BlockSpec((1, LANE, tm), lambda b, i: (b, 0, i)),
    ]
    out_shape += [
        jax.ShapeDtypeStruct((b_count, t, LANE), F32),
        jax.ShapeDtypeStruct((b_count, LANE, t), F32),
    ]
    return pl.pallas_call(
        _fox_prep_kernel,
        grid=(b_count, t // tm),
        in_specs=in_specs, out_specs=out_specs, out_shape=out_shape,
        scratch_shapes=[pltpu.VMEM((1, LANE), F32)],
        compiler_params=_cparams("parallel", "arbitrary"),
        name="fox_prep",
    )(z, z, z, z, fb, q_g.reshape(1, HEAD_DIM), k_g.reshape(1, HEAD_DIM))


def _dsa_select_kernel(kia_ref, kib_ref, qiT_ref, wT_ref, bias_ref, key_ref):
    qb = pl.program_id(1)
    n_kt = key_ref.shape[0] // TK
    w = wT_ref[0]
    s_iota = lax.broadcasted_iota(I32, (TK, TQ), 0)
    t_iota = lax.broadcasted_iota(I32, (TK, TQ), 1)

    for kt in range(n_kt):
        rows = slice(kt * TK, (kt + 1) * TK)

        @pl.when(kt <= qb)
        def _():
            ka = kia_ref[0, rows, :]
            kb = kib_ref[0, rows, :]
            score = jnp.zeros((TK, TQ), F32)
            for j in range(IDX_HEADS // 2):
                qp = qiT_ref[0, j * LANE:(j + 1) * LANE, :]
                ra = jnp.maximum(jnp.dot(ka, qp, preferred_element_type=F32), 0.0)
                rb = jnp.maximum(jnp.dot(kb, qp, preferred_element_type=F32), 0.0)
                wa = w[IDX_DIM + 2 * j:IDX_DIM + 2 * j + 1, :]
                wb = w[IDX_DIM + 2 * j + 1:IDX_DIM + 2 * j + 2, :]
                score = score + (ra * wa + rb * wb)
            bits = lax.bitcast_convert_type(score + 0.0, I32)
            key = bits ^ ((bits >> 31) & 0x7FFFFFFF)
            causal = (kt * TK + s_iota) <= (qb * TQ + t_iota)
            key_ref[rows, :] = jnp.where(causal, key, INT_MIN)

        @pl.when(kt > qb)
        def _():
            key_ref[rows, :] = jnp.full((TK, TQ), INT_MIN, I32)

    def count_ge(cand):
        def body(kt, c):
            k = key_ref[pl.ds(pl.multiple_of(kt * TK, TK), TK), :]
            return c + jnp.sum((k >= cand).astype(F32), axis=0, keepdims=True)
        return lax.fori_loop(0, qb + 1, body, jnp.zeros((1, TQ), F32))

    kf = float(DSA_TOPK)
    ans = jnp.where(count_ge(jnp.zeros((1, TQ), I32)) >= kf, 0, INT_MIN).astype(I32)

    def bit_step(i, ans):
        cand = ans | (jnp.int32(1) << (30 - i))
        return jnp.where(count_ge(cand) >= kf, cand, ans)

    ans = lax.fori_loop(0, 31, bit_step, ans)

    for kt in range(n_kt):
        rows = slice(kt * TK, (kt + 1) * TK)
        k = key_ref[rows, :]
        sel = (k >= ans) & (k > INT_MIN)
        bias_ref[0, rows, :] = jnp.where(sel, 0.0, NEG).astype(BF16)


def _dsa_select(kia, kib, qiT, wT):
    b_count, t, _ = kia.shape
    return pl.pallas_call(
        _dsa_select_kernel,
        grid=(b_count, t // TQ),
        in_specs=[pl.BlockSpec((1, t, LANE), lambda b, i: (b, 0, 0)),
                  pl.BlockSpec((1, t, LANE), lambda b, i: (b, 0, 0)),
                  pl.BlockSpec((1, IDX_HEADS * IDX_DIM, TQ), lambda b, i: (b, 0, i)),
                  pl.BlockSpec((1, LANE, TQ), lambda b, i: (b, 0, i))],
        out_specs=pl.BlockSpec((1, t, TQ), lambda b, i: (b, 0, i)),
        out_shape=jax.ShapeDtypeStruct((b_count, t, t), BF16),
        scratch_shapes=[pltpu.VMEM((t, TQ), I32)],
        compiler_params=_cparams("parallel", "parallel"),
        name="dsa_select",
    )(kia, kib, qiT, wT)


def _flash_init(m_ref, l_ref, acc_ref):
    m_ref[...] = jnp.full(m_ref.shape, -jnp.inf, F32)
    l_ref[...] = jnp.zeros(l_ref.shape, F32)
    acc_ref[...] = jnp.zeros(acc_ref.shape, F32)


def _flash_step(r, s, vt, m_ref, l_ref, acc_ref):
    m_old = m_ref[r]
    m_new = jnp.maximum(m_old, jnp.max(s, axis=0, keepdims=True))
    p = jnp.exp(s - m_new)
    alpha = jnp.exp(m_old - m_new)
    l_ref[r] = alpha * l_ref[r] + jnp.sum(p, axis=0, keepdims=True)
    acc_ref[r] = alpha * acc_ref[r] + jnp.dot(vt, p.astype(BF16), preferred_element_type=F32)
    m_ref[r] = m_new


def _flash_finish(r, gate, l_ref, acc_ref):
    o = (acc_ref[r] / l_ref[r]).T
    return (gate * jax.nn.sigmoid(gate) * o).astype(BF16)


def _causal_tile():
    s_iota = lax.broadcasted_iota(I32, (TK, TQ), 0)
    t_iota = lax.broadcasted_iota(I32, (TK, TQ), 1)
    return s_iota <= t_iota


def _dsa_attn_kernel(qT_ref, k_ref, vT_ref, bias_ref, gate_ref, o_ref, m_ref, l_ref, acc_ref):
    qb = pl.program_id(2)
    _flash_init(m_ref, l_ref, acc_ref)

    def tile(j, carry):
        off = pl.multiple_of(j * TK, TK)
        kt = k_ref[0, pl.ds(off, TK), :]
        vt = vT_ref[0, :, pl.ds(off, TK)]
        bias = bias_ref[0, pl.ds(off, TK), :].astype(F32)
        for r in range(GROUP):
            q = qT_ref[0, r * HEAD_DIM:(r + 1) * HEAD_DIM, :]
            s = jnp.dot(kt, q, preferred_element_type=F32) + bias
            _flash_step(r, s, vt, m_ref, l_ref, acc_ref)
        return carry

    lax.fori_loop(0, qb + 1, tile, 0)
    for r in range(GROUP):
        sl = slice(r * HEAD_DIM, (r + 1) * HEAD_DIM)
        o_ref[:, sl] = _flash_finish(r, gate_ref[:, sl], l_ref, acc_ref)


def _moba_attn_kernel(qT_ref, k_ref, vT_ref, kmean_ref, gate_ref, o_ref, m_ref, l_ref, acc_ref, selb_ref):
    own = pl.program_id(2)
    n_blocks = kmean_ref.shape[2]
    _flash_init(m_ref, l_ref, acc_ref)

    km = kmean_ref[0, 0].astype(BF16)
    gs = jnp.dot(km, qT_ref[0], preferred_element_type=F32)
    blk = lax.broadcasted_iota(I32, (n_blocks, TQ), 0)
    past = blk < own
    gs = jnp.where(past, gs, NEG)
    sel = jnp.zeros((n_blocks, TQ), jnp.bool_)
    for _ in range(min(MOBA_TOPK, n_blocks - 1)):
        mx = jnp.max(gs, axis=0, keepdims=True)
        first = jnp.min(jnp.where(gs == mx, blk, n_blocks), axis=0, keepdims=True)
        pick = blk == first
        sel = sel | pick
        gs = jnp.where(pick, -jnp.inf, gs)
    selb_ref[...] = jnp.where(sel & past, 0.0, NEG)

    def tile(j, carry):
        off = pl.multiple_of(j * TK, TK)
        kt = k_ref[0, pl.ds(off, TK), :]
        vt = vT_ref[0, :, pl.ds(off, TK)]
        bias = selb_ref[pl.ds(j, 1), :]
        for r in range(GROUP):
            q = qT_ref[0, r * HEAD_DIM:(r + 1) * HEAD_DIM, :]
            s = jnp.dot(kt, q, preferred_element_type=F32) + bias
            _flash_step(r, s, vt, m_ref, l_ref, acc_ref)
        return carry

    lax.fori_loop(0, own, tile, 0)

    off = pl.multiple_of(own * TK, TK)
    kt = k_ref[0, pl.ds(off, TK), :]
    vt = vT_ref[0, :, pl.ds(off, TK)]
    causal = _causal_tile()
    for r in range(GROUP):
        sl = slice(r * HEAD_DIM, (r + 1) * HEAD_DIM)
        s = jnp.where(causal, jnp.dot(kt, qT_ref[0, sl, :], preferred_element_type=F32), NEG)
        _flash_step(r, s, vt, m_ref, l_ref, acc_ref)
        o_ref[:, sl] = _flash_finish(r, gate_ref[:, sl], l_ref, acc_ref)


def _fox_attn_kernel(qT_ref, k_ref, vT_ref, cum_ref, cumT_ref, gate_ref, o_ref, m_ref, l_ref, acc_ref):
    h = pl.program_id(1)
    qb = pl.program_id(2)
    _flash_init(m_ref, l_ref, acc_ref)
    q = qT_ref[0]
    cq = cumT_ref[0, pl.ds(h, 1), :]
    onehot = (lax.broadcasted_iota(I32, (TK, LANE), 1) == h).astype(F32)

    def logits(j):
        off = pl.multiple_of(j * TK, TK)
        kt = k_ref[0, pl.ds(off, TK), :]
        vt = vT_ref[0, :, pl.ds(off, TK)]
        ck = jnp.sum(cum_ref[0, pl.ds(off, TK), :] * onehot, axis=1, keepdims=True)
        s = jnp.dot(kt, q, preferred_element_type=F32) + (cq - ck)
        return s, vt

    def tile(j, carry):
        s, vt = logits(j)
        _flash_step(0, s, vt, m_ref, l_ref, acc_ref)
        return carry

    lax.fori_loop(0, qb, tile, 0)
    s, vt = logits(qb)
    _flash_step(0, jnp.where(_causal_tile(), s, NEG), vt, m_ref, l_ref, acc_ref)
    o_ref[...] = _flash_finish(0, gate_ref[...], l_ref, acc_ref)


def _flash_scratch(r):
    return [pltpu.VMEM((r, 1, TQ), F32), pltpu.VMEM((r, 1, TQ), F32), pltpu.VMEM((r, HEAD_DIM, TQ), F32)]


def _gqa_specs(t, gate_blk):
    nq = t // TQ
    gw = GROUP * HEAD_DIM
    in_specs = [
        pl.BlockSpec((1, gw, TQ), lambda b, g, i: (b, g, i)),
        pl.BlockSpec((1, t, HEAD_DIM), lambda b, g, i: (b, 0, g)),
        pl.BlockSpec((1, HEAD_DIM, t), lambda b, g, i: (b, g, 0)),
    ]
    gate_spec = pl.BlockSpec((TQ, gw), lambda b, g, i: (b * nq + i, gate_blk + g))
    out_spec = pl.BlockSpec((TQ, gw), lambda b, g, i: (b * nq + i, g))
    return nq, in_specs, gate_spec, out_spec


def _dsa_attn(qT, k, vT, bias, z):
    b_count, t, _ = k.shape
    gate_blk = (INNER + 2 * KV_WIDTH) // (GROUP * HEAD_DIM)
    nq, in_specs, gate_spec, out_spec = _gqa_specs(t, gate_blk)
    in_specs += [pl.BlockSpec((1, t, TQ), lambda b, g, i: (b, 0, i)), gate_spec]
    return pl.pallas_call(
        _dsa_attn_kernel,
        grid=(b_count, N_KV_HEADS, nq),
        in_specs=in_specs, out_specs=out_spec,
        out_shape=jax.ShapeDtypeStruct((b_count * t, INNER), BF16),
        scratch_shapes=_flash_scratch(GROUP),
        compiler_params=_cparams("parallel", "parallel", "parallel"),
        name="dsa_attn",
    )(qT, k, vT, bias, z)


def _moba_attn(qT, k, vT, kmean, z):
    b_count, t, _ = k.shape
    gate_blk = (INNER + 2 * KV_WIDTH) // (GROUP * HEAD_DIM)
    nq, in_specs, gate_spec, out_spec = _gqa_specs(t, gate_blk)
    n_blocks = t // MOBA_BLOCK
    in_specs += [pl.BlockSpec((1, 1, n_blocks, GROUP * HEAD_DIM), lambda b, g, i: (b, g, 0, 0)), gate_spec]
    return pl.pallas_call(
        _moba_attn_kernel,
        grid=(b_count, N_KV_HEADS, nq),
        in_specs=in_specs, out_specs=out_spec,
        out_shape=jax.ShapeDtypeStruct((b_count * t, INNER), BF16),
        scratch_shapes=_flash_scratch(GROUP) + [pltpu.VMEM((n_blocks, TQ), F32)],
        compiler_params=_cparams("parallel", "parallel", "parallel"),
        name="moba_attn",
    )(qT, k, vT, kmean, z)


def _fox_attn(qT, k, vT, cum, cumT, z):
    b_count, t, _ = k.shape
    nq = t // TQ
    gate_blk = 3 * INNER // HEAD_DIM
    return pl.pallas_call(
        _fox_attn_kernel,
        grid=(b_count, N_HEADS, nq),
        in_specs=[pl.BlockSpec((1, HEAD_DIM, TQ), lambda b, h, i: (b, h, i)),
                  pl.BlockSpec((1, t, HEAD_DIM), lambda b, h, i: (b, 0, h)),
                  pl.BlockSpec((1, HEAD_DIM, t), lambda b, h, i: (b, h, 0)),
                  pl.BlockSpec((1, t, LANE), lambda b, h, i: (b, 0, 0)),
                  pl.BlockSpec((1, LANE, TQ), lambda b, h, i: (b, 0, i)),
                  pl.BlockSpec((TQ, HEAD_DIM), lambda b, h, i: (b * nq + i, gate_blk + h))],
        out_specs=pl.BlockSpec((TQ, HEAD_DIM), lambda b, h, i: (b * nq + i, h)),
        out_shape=jax.ShapeDtypeStruct((b_count * t, INNER), BF16),
        scratch_shapes=_flash_scratch(1),
        compiler_params=_cparams("parallel", "parallel", "parallel"),
        name="fox_attn",
    )(qT, k, vT, cum, cumT, z)


def _retention_kernel(q_ref, k_ref, v_ref, gate_ref, cos_ref, sin_ref, lg_ref, gn_ref, o_ref, state_ref):
    c = RET_CHUNK

    @pl.when(pl.program_id(2) == 0)
    def _():
        state_ref[...] = jnp.zeros_like(state_ref)

    lg = lg_ref[0][:, :1]
    cos = cos_ref[...]
    sin = sin_ref[...]
    half = RET_QK_DIM // 2

    def rope(ref):
        x1 = ref[:, :half]
        x2 = ref[:, half:]
        return jnp.concatenate([x1 * cos - x2 * sin, x1 * sin + x2 * cos], axis=1)

    q = rope(q_ref)
    k = rope(k_ref) * (RET_QK_DIM ** -0.5)
    v = v_ref[...].astype(BF16)
    i_col = lax.broadcasted_iota(I32, (c, 1), 0).astype(F32)
    q_decay = jnp.exp((i_col + 1.0) * lg)
    k_decay = jnp.exp((c - 1.0 - i_col) * lg)
    diff = (lax.broadcasted_iota(I32, (c, c), 0) - lax.broadcasted_iota(I32, (c, c), 1)).astype(F32)
    decay_mask = jnp.where(diff >= 0, jnp.exp(jnp.maximum(diff, 0.0) * lg), 0.0)

    inner = lax.dot_general(q.astype(BF16), k.astype(BF16), (((1,), (1,)), ((), ())),
                            preferred_element_type=F32) * decay_mask
    state = state_ref[...]
    o = (jnp.dot(inner.astype(BF16), v, preferred_element_type=F32)
         + jnp.dot((q * q_decay).astype(BF16), state.astype(BF16), preferred_element_type=F32))
    kdT = (k * k_decay).T.astype(BF16)
    state_ref[...] = state * jnp.exp(c * lg) + jnp.dot(kdT, v, preferred_element_type=F32)

    mu = jnp.mean(o, axis=-1, keepdims=True)
    var = jnp.mean(jnp.square(o - mu), axis=-1, keepdims=True)
    on = (o - mu) * lax.rsqrt(var + EPS) * gn_ref[...]
    gate = gate_ref[...]
    o_ref[...] = (gate * jax.nn.sigmoid(gate) * on).astype(BF16)


def _retention(z, b_count, t, gn_g):
    c = RET_CHUNK
    nc = t // c
    half = RET_QK_DIM // 2
    pos = jnp.arange(t, dtype=jnp.int32)
    inv = ROPE_THETA ** (-jnp.arange(0, RET_QK_DIM, 2, dtype=F32) / RET_QK_DIM)
    ang = pos.astype(F32)[:, None] * inv[None, :]
    cos, sin = jnp.cos(ang), jnp.sin(ang)
    log_gamma = jnp.log(1.0 - 2.0 ** (-5.0 - jnp.arange(RET_HEADS, dtype=F32)))
    lg = jnp.broadcast_to(log_gamma[:, None, None], (RET_HEADS, 1, LANE))
    qk_w = RET_HEADS * RET_QK_DIM
    row = lambda b, h, i: b * nc + i
    return pl.pallas_call(
        _retention_kernel,
        grid=(b_count, RET_HEADS, nc),
        in_specs=[pl.BlockSpec((c, RET_QK_DIM), lambda b, h, i: (row(b, h, i), h)),
                  pl.BlockSpec((c, RET_QK_DIM), lambda b, h, i: (row(b, h, i), RET_HEADS + h)),
                  pl.BlockSpec((c, RET_V_DIM), lambda b, h, i: (row(b, h, i), 2 * qk_w // RET_V_DIM + h)),
                  pl.BlockSpec((c, RET_V_DIM),
                               lambda b, h, i: (row(b, h, i), (2 * qk_w + RET_INNER) // RET_V_DIM + h)),
                  pl.BlockSpec((c, half), lambda b, h, i: (i, 0)),
                  pl.BlockSpec((c, half), lambda b, h, i: (i, 0)),
                  pl.BlockSpec((1, 1, LANE), lambda b, h, i: (h, 0, 0)),
                  pl.BlockSpec((1, RET_V_DIM), lambda b, h, i: (0, h))],
        out_specs=pl.BlockSpec((c, RET_V_DIM), lambda b, h, i: (row(b, h, i), h)),
        out_shape=jax.ShapeDtypeStruct((b_count * t, RET_INNER), BF16),
        scratch_shapes=[pltpu.VMEM((RET_QK_DIM, RET_V_DIM), F32)],
        compiler_params=_cparams("parallel", "parallel", "arbitrary"),
        name="retention",
    )(z, z, z, z, cos, sin, lg, gn_g.reshape(1, RET_INNER))


def _dsa_layer(h, b_count, t, norm_g, w_in, q_g, k_g, w_out):
    n_pad = 49 * LANE
    z = _matmul(_rmsnorm(h, norm_g), _weight_bf16(w_in, n_pad), 1024, 7 * LANE, "dsa_in_proj")
    qT, k, vT, qiT, kia, kib, wT = _dsa_prep(z, b_count, t, q_g, k_g)
    bias = _dsa_select(kia, kib, qiT, wT)
    gated = _dsa_attn(qT, k, vT, bias, z)
    return _matmul_residual(gated, _weight_bf16(w_out), h, 1024, 1024, "dsa_out_proj")


def _moba_layer(h, b_count, t, norm_g, w_in, q_g, k_g, w_out):
    z = _matmul(_rmsnorm(h, norm_g), _weight_bf16(w_in), 1024, 1024, "moba_in_proj")
    qT, k, vT, kmean = _moba_prep(z, b_count, t, q_g, k_g)
    gated = _moba_attn(qT, k, vT, kmean, z)
    return _matmul_residual(gated, _weight_bf16(w_out), h, 1024, 1024, "moba_out_proj")


def _ret_layer(h, b_count, t, norm_g, w_in, gn_g, w_out):
    z = _matmul(_rmsnorm(h, norm_g), _weight_bf16(w_in), 1024, 1024, "ret_in_proj")
    gated = _retention(z, b_count, t, gn_g)
    return _matmul_residual(gated, _weight_bf16(w_out), h, 512, 1024, "ret_out_proj")


def _fox_layer(h, b_count, t, norm_g, w_in, f_bias, q_g, k_g, w_out):
    n_pad = 65 * LANE
    z = _matmul(_rmsnorm(h, norm_g), _weight_bf16(w_in, n_pad), 1024, 5 * LANE, "fox_in_proj")
    qT, k, vT, cum, cumT = _fox_prep(z, b_count, t, f_bias, q_g, k_g)
    gated = _fox_attn(qT, k, vT, cum, cumT, z)
    return _matmul_residual(gated, _weight_bf16(w_out), h, 1024, 1024, "fox_out_proj")


def kernel(x, a_norm, a_w_in, a_q_norm, a_k_norm, a_w_out, b_norm, b_w_in, b_q_norm, b_k_norm, b_w_out,
           c_norm, c_w_in, c_gn, c_w_out, d_norm, d_w_in, d_f_bias, d_q_norm, d_k_norm, d_w_out):
    b_count, t, d = x.shape
    assert d == D_MODEL and t % TQ == 0 and t // 4 >= DSA_TOPK
    h = x.reshape(b_count * t, d)
    depth = 4
    for i in range(depth):
        m, j = i % 4, i // 4
        if m == 0:
            h = _dsa_layer(h, b_count, t, a_norm[j], a_w_in[j], a_q_norm[j], a_k_norm[j], a_w_out[j])
        elif m == 1:
            h = _moba_layer(h, b_count, t, b_norm[j], b_w_in[j], b_q_norm[j], b_k_norm[j], b_w_out[j])
        elif m == 2:
            h = _ret_layer(h, b_count, t, c_norm[j], c_w_in[j], c_gn[j], c_w_out[j])
        else:
            h = _fox_layer(h, b_count, t, d_norm[j], d_w_in[j], d_f_bias[j], d_q_norm[j], d_k_norm[j], d_w_out[j])
    return h.reshape(b_count, t, d)
```

```python
import functools

import jax
import jax.numpy as jnp
import numpy as np
from jax import lax
from jax.experimental import pallas as pl
from jax.experimental.pallas import tpu as pltpu

F32 = jnp.float32
BF16 = jnp.bfloat16
I32 = jnp.int32

D_MODEL = 2048
HEAD_DIM = 128
N_HEADS = 16
N_KV_HEADS = 4
GROUP = N_HEADS // N_KV_HEADS
INNER = N_HEADS * HEAD_DIM
KV_WIDTH = N_KV_HEADS * HEAD_DIM
IDX_HEADS = 16
IDX_DIM = 64
DSA_TOPK = 256
MOBA_BLOCK = 256
MOBA_TOPK = 3
RET_HEADS = 8
RET_QK_DIM = 256
RET_V_DIM = 512
RET_INNER = RET_HEADS * RET_V_DIM
ROPE_THETA = 10000.0
EPS = 1e-6
NEG = -1e30
INT_MIN = -(2 ** 31)

LANE = 128
TQ = 256
TK = 256
RET_CHUNK = 256
VMEM_LIMIT = 56 * 1024 * 1024


def _cparams(*sem):
    return pltpu.CompilerParams(dimension_semantics=sem, vmem_limit_bytes=VMEM_LIMIT)


def _rmsnorm_kernel(x_ref, g_ref, o_ref):
    x = x_ref[...]
    ms = jnp.mean(x * x, axis=-1, keepdims=True)
    o_ref[...] = (x * lax.rsqrt(ms + EPS) * g_ref[...]).astype(o_ref.dtype)


def _rmsnorm(x, g, tm=512):
    m, d = x.shape
    return pl.pallas_call(
        _rmsnorm_kernel,
        grid=(m // tm,),
        in_specs=[pl.BlockSpec((tm, d), lambda i: (i, 0)),
                  pl.BlockSpec((1, d), lambda i: (0, 0))],
        out_specs=pl.BlockSpec((tm, d), lambda i: (i, 0)),
        out_shape=jax.ShapeDtypeStruct((m, d), BF16),
        compiler_params=_cparams("parallel"),
        name="rmsnorm",
    )(x, g.reshape(1, d))


def _mm_kernel(a_ref, w_ref, o_ref):
    o_ref[...] = jnp.dot(a_ref[...], w_ref[...], preferred_element_type=F32).astype(o_ref.dtype)


def _mm_res_kernel(a_ref, w_ref, r_ref, o_ref):
    o_ref[...] = r_ref[...] + jnp.dot(a_ref[...], w_ref[...], preferred_element_type=F32)


def _matmul(a, w, tm, tn, name):
    m, k = a.shape
    n = w.shape[1]
    return pl.pallas_call(
        _mm_kernel,
        grid=(m // tm, n // tn),
        in_specs=[pl.BlockSpec((tm, k), lambda i, j: (i, 0)),
                  pl.BlockSpec((k, tn), lambda i, j: (0, j))],
        out_specs=pl.BlockSpec((tm, tn), lambda i, j: (i, j)),
        out_shape=jax.ShapeDtypeStruct((m, n), F32),
        compiler_params=_cparams("parallel", "parallel"),
        name=name,
    )(a, w)


def _matmul_residual(a, w, res, tm, tn, name):
    m, k = a.shape
    n = w.shape[1]
    return pl.pallas_call(
        _mm_res_kernel,
        grid=(m // tm, n // tn),
        in_specs=[pl.BlockSpec((tm, k), lambda i, j: (i, 0)),
                  pl.BlockSpec((k, tn), lambda i, j: (0, j)),
                  pl.BlockSpec((tm, tn), lambda i, j: (i, j))],
        out_specs=pl.BlockSpec((tm, tn), lambda i, j: (i, j)),
        out_shape=jax.ShapeDtypeStruct((m, n), F32),
        compiler_params=_cparams("parallel", "parallel"),
        name=name,
    )(a, w, res)


def _weight_bf16(w, n_pad=None):
    w = w.astype(BF16)
    if n_pad is not None and n_pad > w.shape[1]:
        w = jnp.pad(w, ((0, 0), (0, n_pad - w.shape[1])))
    return w


def _rope_tables(t, d, reps):
    pos = jnp.arange(t, dtype=jnp.int32)
    inv = ROPE_THETA ** (-jnp.arange(0, d, 2, dtype=F32) / d)
    ang = pos.astype(F32)[:, None] * inv[None, :]
    cos, sin = jnp.cos(ang), jnp.sin(ang)
    cos_t = jnp.tile(jnp.concatenate([cos, cos], axis=-1), (1, reps))
    sin_t = jnp.tile(jnp.concatenate([-sin, sin], axis=-1), (1, reps))
    return cos_t, sin_t


def _head_norm(x, g):
    ms = jnp.mean(x * x, axis=-1, keepdims=True)
    return x * lax.rsqrt(ms + EPS) * g


def _rope128(y, cos, sin):
    return y * cos + pltpu.roll(y, 64, 1) * sin


def _qkv_prep_body(q_ref, k_ref, v_ref, cos_ref, sin_ref, qg_ref, kg_ref,
                   qT_ref, ko_ref, vT_ref, *, n_kv, rope):
    qg = qg_ref[...]
    kg = kg_ref[...]
    scale = HEAD_DIM ** -0.5
    if rope:
        cos = cos_ref[...]
        sin = sin_ref[...]
    for h in range(N_HEADS):
        sl = slice(h * HEAD_DIM, (h + 1) * HEAD_DIM)
        y = _head_norm(q_ref[:, sl], qg)
        if rope:
            y = _rope128(y, cos, sin)
        qT_ref[0, sl, :] = (y * scale).T.astype(BF16)
    k_out = []
    for g in range(n_kv):
        sl = slice(g * HEAD_DIM, (g + 1) * HEAD_DIM)
        y = _head_norm(k_ref[:, sl], kg)
        if rope:
            y = _rope128(y, cos, sin)
        if ko_ref.ndim == 4:
            ko_ref[0, g] = y.astype(BF16)
        else:
            ko_ref[0, :, sl] = y.astype(BF16)
        vT_ref[0, sl, :] = v_ref[:, sl].T.astype(BF16)
        k_out.append(y)
    return k_out


def _dsa_prep_kernel(q_ref, k_ref, v_ref, qi_ref, tail_ref, cos_ref, sin_ref, cos64_ref, sin64_ref,
                     qg_ref, kg_ref, qT_ref, ko_ref, vT_ref, qiT_ref, kia_ref, kib_ref, wT_ref):
    _qkv_prep_body(q_ref, k_ref, v_ref, cos_ref, sin_ref, qg_ref, kg_ref, qT_ref, ko_ref, vT_ref,
                   n_kv=N_KV_HEADS, rope=True)
    tm = tail_ref.shape[0]
    lane = lax.broadcasted_iota(I32, (tm, LANE), 1)
    first_half = (lane % IDX_DIM) < (IDX_DIM // 2)
    c64 = cos64_ref[...]
    s64 = sin64_ref[...]

    def rope64(x):
        rot = jnp.where(first_half, pltpu.roll(x, LANE - IDX_DIM // 2, 1), pltpu.roll(x, IDX_DIM // 2, 1))
        return x * c64 + rot * s64

    for j in range(IDX_HEADS * IDX_DIM // LANE):
        sl = slice(j * LANE, (j + 1) * LANE)
        y = rope64(qi_ref[:, sl]) * (IDX_DIM ** -0.5)
        qiT_ref[0, sl, :] = y.T.astype(BF16)
    tail = tail_ref[...]
    ka = jnp.where(lane < IDX_DIM, rope64(tail), 0.0)
    kia_ref[0] = ka.astype(BF16)
    kib_ref[0] = pltpu.roll(ka, IDX_DIM, 1).astype(BF16)
    wT_ref[0] = (tail * (IDX_HEADS ** -0.5)).T


def _moba_prep_kernel(q_ref, k_ref, v_ref, cos_ref, sin_ref, qg_ref, kg_ref,
                      qT_ref, ko_ref, vT_ref, kmean_ref):
    k_out = _qkv_prep_body(q_ref, k_ref, v_ref, cos_ref, sin_ref, qg_ref, kg_ref, qT_ref, ko_ref, vT_ref,
                           n_kv=N_KV_HEADS, rope=True)
    for g in range(N_KV_HEADS):
        km = jnp.mean(k_out[g], axis=0, keepdims=True)
        kmean_ref[0, g, pl.ds(pl.program_id(1), 1), :] = jnp.concatenate([km] * GROUP, axis=1)


def _fox_prep_kernel(q_ref, k_ref, v_ref, f_ref, fb_ref, qg_ref, kg_ref,
                     qT_ref, ko_ref, vT_ref, cum_ref, cumT_ref, carry_ref):
    _qkv_prep_body(q_ref, k_ref, v_ref, None, None, qg_ref, kg_ref, qT_ref, ko_ref, vT_ref,
                   n_kv=N_HEADS, rope=False)

    @pl.when(pl.program_id(1) == 0)
    def _():
        carry_ref[...] = jnp.zeros_like(carry_ref)

    tm = f_ref.shape[0]
    x = f_ref[...] + fb_ref[...]
    lf = jnp.minimum(x, 0.0) - jnp.log(1.0 + jnp.exp(-jnp.abs(x)))
    hi = lf.astype(BF16)
    r1 = lf - hi.astype(F32)
    lo = r1.astype(BF16)
    lo2 = (r1 - lo.astype(F32)).astype(BF16)
    row = lax.broadcasted_iota(I32, (tm, tm), 0)
    col = lax.broadcasted_iota(I32, (tm, tm), 1)
    tri = (col <= row).astype(BF16)
    parts = jnp.dot(tri, jnp.concatenate([hi, lo, lo2], axis=1), preferred_element_type=F32)
    cum = parts[:, :LANE] + parts[:, LANE:2 * LANE] + parts[:, 2 * LANE:] + carry_ref[...]
    carry_ref[...] = cum[tm - 1:tm, :]
    cum_ref[0] = cum
    cumT_ref[0] = cum.T


def _prep_specs(b_count, t, tm, z_q_blk, z_k_blk, z_v_blk, kv_width):
    nt = t // tm
    row = lambda b, i: b * nt + i
    in_specs = [
        pl.BlockSpec((tm, INNER), lambda b, i: (row(b, i), z_q_blk)),
        pl.BlockSpec((tm, kv_width), lambda b, i: (row(b, i), z_k_blk)),
        pl.BlockSpec((tm, kv_width), lambda b, i: (row(b, i), z_v_blk)),
    ]
    out_specs = [
        pl.BlockSpec((1, INNER, tm), lambda b, i: (b, 0, i)),
        pl.BlockSpec((1, tm, kv_width), lambda b, i: (b, i, 0)),
        pl.BlockSpec((1, kv_width, tm), lambda b, i: (b, 0, i)),
    ]
    out_shape = [
        jax.ShapeDtypeStruct((b_count, INNER, t), BF16),
        jax.ShapeDtypeStruct((b_count, t, kv_width), BF16),
        jax.ShapeDtypeStruct((b_count, kv_width, t), BF16),
    ]
    return row, in_specs, out_specs, out_shape


def _dsa_prep(z, b_count, t, q_g, k_g, tm=256):
    row, in_specs, out_specs, out_shape = _prep_specs(b_count, t, tm, 0, INNER // KV_WIDTH,
                                                      INNER // KV_WIDTH + 1, KV_WIDTH)
    qi_w = IDX_HEADS * IDX_DIM
    qi_off = 2 * INNER + 2 * KV_WIDTH
    cos, sin = _rope_tables(t, HEAD_DIM, 1)
    cos64, sin64 = _rope_tables(t, IDX_DIM, LANE // IDX_DIM)
    tab = pl.BlockSpec((tm, LANE), lambda b, i: (i, 0))
    gain = pl.BlockSpec((1, HEAD_DIM), lambda b, i: (0, 0))
    in_specs += [
        pl.BlockSpec((tm, qi_w), lambda b, i: (row(b, i), qi_off // qi_w)),
        pl.BlockSpec((tm, LANE), lambda b, i: (row(b, i), (qi_off + qi_w) // LANE)),
        tab, tab, tab, tab, gain, gain,
    ]
    out_specs += [
        pl.BlockSpec((1, qi_w, tm), lambda b, i: (b, 0, i)),
        pl.BlockSpec((1, tm, LANE), lambda b, i: (b, i, 0)),
        pl.BlockSpec((1, tm, LANE), lambda b, i: (b, i, 0)),
        pl.BlockSpec((1, LANE, tm), lambda b, i: (b, 0, i)),
    ]
    out_shape += [
        jax.ShapeDtypeStruct((b_count, qi_w, t), BF16),
        jax.ShapeDtypeStruct((b_count, t, LANE), BF16),
        jax.ShapeDtypeStruct((b_count, t, LANE), BF16),
        jax.ShapeDtypeStruct((b_count, LANE, t), F32),
    ]
    return pl.pallas_call(
        _dsa_prep_kernel,
        grid=(b_count, t // tm),
        in_specs=in_specs, out_specs=out_specs, out_shape=out_shape,
        compiler_params=_cparams("parallel", "parallel"),
        name="dsa_prep",
    )(z, z, z, z, z, cos, sin, cos64, sin64, q_g.reshape(1, HEAD_DIM), k_g.reshape(1, HEAD_DIM))


def _moba_prep(z, b_count, t, q_g, k_g):
    tm = MOBA_BLOCK
    row, in_specs, out_specs, out_shape = _prep_specs(b_count, t, tm, 0, INNER // KV_WIDTH,
                                                      INNER // KV_WIDTH + 1, KV_WIDTH)
    cos, sin = _rope_tables(t, HEAD_DIM, 1)
    tab = pl.BlockSpec((tm, LANE), lambda b, i: (i, 0))
    gain = pl.BlockSpec((1, HEAD_DIM), lambda b, i: (0, 0))
    in_specs += [tab, tab, gain, gain]
    out_specs += [pl.BlockSpec((1, N_KV_HEADS, t // tm, GROUP * HEAD_DIM), lambda b, i: (b, 0, 0, 0))]
    out_shape += [jax.ShapeDtypeStruct((b_count, N_KV_HEADS, t // tm, GROUP * HEAD_DIM), F32)]
    return pl.pallas_call(
        _moba_prep_kernel,
        grid=(b_count, t // tm),
        in_specs=in_specs, out_specs=out_specs, out_shape=out_shape,
        compiler_params=_cparams("parallel", "arbitrary"),
        name="moba_prep",
    )(z, z, z, cos, sin, q_g.reshape(1, HEAD_DIM), k_g.reshape(1, HEAD_DIM))


def _fox_prep(z, b_count, t, f_bias, q_g, k_g, tm=256):
    row, in_specs, out_specs, out_shape = _prep_specs(b_count, t, tm, 0, 1, 2, INNER)
    out_specs[1] = pl.BlockSpec((1, N_HEADS, tm, HEAD_DIM), lambda b, i: (b, 0, i, 0))
    out_shape[1] = jax.ShapeDtypeStruct((b_count, N_HEADS, t, HEAD_DIM), BF16)
    gain = pl.BlockSpec((1, HEAD_DIM), lambda b, i: (0, 0))
    fb = jnp.pad(f_bias.reshape(1, N_HEADS), ((0, 0), (0, LANE - N_HEADS)))
    in_specs += [
        pl.BlockSpec((tm, LANE), lambda b, i: (row(b, i), 4 * INNER // LANE)),
        pl.BlockSpec((1, LANE), lambda b, i: (0, 0)),
        gain, gain,
    ]
    out_specs += [
        pl.BlockSpec((1, tm, LANE), lambda b, i: (b, i, 0)),
        pl.BlockSpec((1, LANE, tm), lambda b, i: (b, 0, i)),
    ]
    out_shape += [
        jax.ShapeDtypeStruct((b_count, t, LANE), F32),
        jax.ShapeDtypeStruct((b_count, LANE, t), F32),
    ]
    return pl.pallas_call(
        _fox_prep_kernel,
        grid=(b_count, t // tm),
        in_specs=in_specs, out_specs=out_specs, out_shape=out_shape,
        scratch_shapes=[pltpu.VMEM((1, LANE), F32)],
        compiler_params=_cparams("parallel", "arbitrary"),
        name="fox_prep",
    )(z, z, z, z, fb, q_g.reshape(1, HEAD_DIM), k_g.reshape(1, HEAD_DIM))


def _dsa_select_kernel(kia_ref, kib_ref, qiT_ref, wT_ref, bias_ref, key_ref, ans_ref):
    qb = pl.program_id(1)
    n_kt = key_ref.shape[0] // TK
    w = wT_ref[0]
    s_iota = lax.broadcasted_iota(I32, (TK, TQ), 0)
    t_iota = lax.broadcasted_iota(I32, (TK, TQ), 1)

    for kt in range(n_kt):
        rows = slice(kt * TK, (kt + 1) * TK)

        @pl.when(kt <= qb)
        def _():
            ka = kia_ref[0, rows, :]
            kb = kib_ref[0, rows, :]
            score = jnp.zeros((TK, TQ), F32)
            for j in range(IDX_HEADS // 2):
                qp = qiT_ref[0, j * LANE:(j + 1) * LANE, :]
                ra = jnp.maximum(jnp.dot(ka, qp, preferred_element_type=F32), 0.0)
                rb = jnp.maximum(jnp.dot(kb, qp, preferred_element_type=F32), 0.0)
                wa = w[IDX_DIM + 2 * j:IDX_DIM + 2 * j + 1, :]
                wb = w[IDX_DIM + 2 * j + 1:IDX_DIM + 2 * j + 2, :]
                score = score + (ra * wa + rb * wb)
            bits = lax.bitcast_convert_type(score + 0.0, I32)
            key = bits ^ ((bits >> 31) & 0x7FFFFFFF)
            causal = (kt * TK + s_iota) <= (qb * TQ + t_iota)
            key_ref[rows, :] = jnp.where(causal, key, INT_MIN)

        @pl.when(kt > qb)
        def _():
            key_ref[rows, :] = jnp.full((TK, TQ), INT_MIN, I32)

    def variant(v):
        n = (v + 1) * TK
        kf = float(DSA_TOPK)

        def count_ge(cand):
            return jnp.sum((key_ref[:n, :] >= cand).astype(F32), axis=0, keepdims=True)

        def bit_step(i, ans):
            cand = ans | (jnp.int32(1) << (30 - i))
            return jnp.where(count_ge(cand) >= kf, cand, ans)

        sign = jnp.where(count_ge(jnp.zeros((1, TQ), I32)) >= kf, 0, INT_MIN).astype(I32)
        ans_ref[...] = lax.fori_loop(0, 31, bit_step, sign)

    _for_each_query_tile(qb, n_kt, variant)
    ans = ans_ref[...]

    for kt in range(n_kt):
        rows = slice(kt * TK, (kt + 1) * TK)
        k = key_ref[rows, :]
        sel = (k >= ans) & (k > INT_MIN)
        bias_ref[0, rows, :] = jnp.where(sel, 0.0, NEG).astype(BF16)


def _dsa_select(kia, kib, qiT, wT):
    b_count, t, _ = kia.shape
    return pl.pallas_call(
        _dsa_select_kernel,
        grid=(b_count, t // TQ),
        in_specs=[pl.BlockSpec((1, t, LANE), lambda b, i: (b, 0, 0)),
                  pl.BlockSpec((1, t, LANE), lambda b, i: (b, 0, 0)),
                  pl.BlockSpec((1, IDX_HEADS * IDX_DIM, TQ), lambda b, i: (b, 0, i)),
                  pl.BlockSpec((1, LANE, TQ), lambda b, i: (b, 0, i))],
        out_specs=pl.BlockSpec((1, t, TQ), lambda b, i: (b, 0, i)),
        out_shape=jax.ShapeDtypeStruct((b_count, t, t), BF16),
        scratch_shapes=[pltpu.VMEM((t, TQ), I32), pltpu.VMEM((1, TQ), I32)],
        compiler_params=_cparams("parallel", "parallel"),
        name="dsa_select",
    )(kia, kib, qiT, wT)


def _causal_tile():
    s_iota = lax.broadcasted_iota(I32, (TK, TQ), 0)
    t_iota = lax.broadcasted_iota(I32, (TK, TQ), 1)
    return s_iota <= t_iota


def _softmax_attend(q, k_all, vt_all, bias_fn, n_tiles, causal_last):
    s_all = jnp.dot(k_all, q, preferred_element_type=F32)
    tiles = []
    for j in range(n_tiles):
        sj = bias_fn(j, s_all[j * TK:(j + 1) * TK])
        if causal_last and j == n_tiles - 1:
            sj = jnp.where(_causal_tile(), sj, NEG)
        tiles.append(sj)
    m = functools.reduce(jnp.maximum, [jnp.max(sj, axis=0, keepdims=True) for sj in tiles])
    ps = [jnp.exp(sj - m) for sj in tiles]
    l = functools.reduce(jnp.add, [jnp.sum(p, axis=0, keepdims=True) for p in ps])
    ps = [p.astype(BF16) for p in ps]
    p_all = ps[0] if n_tiles == 1 else jnp.concatenate(ps, axis=0)
    return jnp.dot(vt_all, p_all, preferred_element_type=F32) * (1.0 / l)


def _for_each_query_tile(qb, n_q, body):
    for v in range(n_q):
        pl.when(qb == v)(functools.partial(body, v))


def _gate_and_store(oT_ref, gate_ref, o_ref):
    for r in range(GROUP):
        sl = slice(r * HEAD_DIM, (r + 1) * HEAD_DIM)
        gate = gate_ref[:, sl]
        o_ref[:, sl] = (gate * jax.nn.sigmoid(gate) * oT_ref[sl, :].T).astype(BF16)


def _dsa_attn_kernel(qT_ref, k_ref, vT_ref, bias_ref, gate_ref, o_ref, oT_ref):
    def variant(v):
        n = (v + 1) * TK

        def bias_fn(j, s):
            return s + bias_ref[0, j * TK:(j + 1) * TK, :].astype(F32)

        def head(r, carry):
            rows = pl.ds(pl.multiple_of(r * HEAD_DIM, HEAD_DIM), HEAD_DIM)
            oT_ref[rows, :] = _softmax_attend(qT_ref[0, rows, :], k_ref[0, :n, :], vT_ref[0, :, :n],
                                              bias_fn, v + 1, False)
            return carry

        lax.fori_loop(0, GROUP, head, 0)

    _for_each_query_tile(pl.program_id(2), k_ref.shape[1] // TQ, variant)
    _gate_and_store(oT_ref, gate_ref, o_ref)


def _moba_attn_kernel(qT_ref, k_ref, vT_ref, kmean_ref, gate_ref, o_ref, oT_ref, selb_ref):
    own = pl.program_id(2)
    n_blocks = kmean_ref.shape[2]

    km = kmean_ref[0, 0].astype(BF16)
    gs = jnp.dot(km, qT_ref[0], preferred_element_type=F32)
    blk = lax.broadcasted_iota(I32, (n_blocks, TQ), 0)
    past = blk < own
    gs = jnp.where(past, gs, NEG)
    sel = jnp.zeros((n_blocks, TQ), jnp.bool_)
    for _ in range(min(MOBA_TOPK, n_blocks - 1)):
        mx = jnp.max(gs, axis=0, keepdims=True)
        first = jnp.min(jnp.where(gs == mx, blk, n_blocks), axis=0, keepdims=True)
        pick = blk == first
        sel = sel | pick
        gs = jnp.where(pick, -jnp.inf, gs)
    selb_ref[...] = jnp.where(sel & past, 0.0, NEG)

    def variant(v):
        n = (v + 1) * TK

        def bias_fn(j, s):
            return s if j == v else s + selb_ref[j:j + 1, :]

        def head(r, carry):
            rows = pl.ds(pl.multiple_of(r * HEAD_DIM, HEAD_DIM), HEAD_DIM)
            oT_ref[rows, :] = _softmax_attend(qT_ref[0, rows, :], k_ref[0, :n, :], vT_ref[0, :, :n],
                                              bias_fn, v + 1, True)
            return carry

        lax.fori_loop(0, GROUP, head, 0)

    _for_each_query_tile(own, n_blocks, variant)
    _gate_and_store(oT_ref, gate_ref, o_ref)


def _fox_attn_kernel(qT_ref, k_ref, vT_ref, cum_ref, cumT_ref, gate_ref, o_ref, oT_ref):
    g = pl.program_id(1)

    def variant(v):
        n = (v + 1) * TK

        def head(r, carry):
            h = g * GROUP + r
            rows = pl.ds(pl.multiple_of(r * HEAD_DIM, HEAD_DIM), HEAD_DIM)
            cq = cumT_ref[0, pl.ds(h, 1), :]
            onehot = (lax.broadcasted_iota(I32, (TK, LANE), 1) == h).astype(F32)

            def bias_fn(j, s):
                ck = jnp.sum(cum_ref[0, j * TK:(j + 1) * TK, :] * onehot, axis=1, keepdims=True)
                return s + (cq - ck)

            oT_ref[rows, :] = _softmax_attend(qT_ref[0, rows, :], k_ref[0, r, :n, :], vT_ref[0, rows, :n],
                                              bias_fn, v + 1, True)
            return carry

        lax.fori_loop(0, GROUP, head, 0)

    _for_each_query_tile(pl.program_id(2), k_ref.shape[2] // TQ, variant)
    _gate_and_store(oT_ref, gate_ref, o_ref)


def _attn_scratch():
    return [pltpu.VMEM((GROUP * HEAD_DIM, TQ), F32)]


def _gqa_specs(t, gate_blk):
    nq = t // TQ
    gw = GROUP * HEAD_DIM
    in_specs = [
        pl.BlockSpec((1, gw, TQ), lambda b, g, i: (b, g, i)),
        pl.BlockSpec((1, t, HEAD_DIM), lambda b, g, i: (b, 0, g)),
        pl.BlockSpec((1, HEAD_DIM, t), lambda b, g, i: (b, g, 0)),
    ]
    gate_spec = pl.BlockSpec((TQ, gw), lambda b, g, i: (b * nq + i, gate_blk + g))
    out_spec = pl.BlockSpec((TQ, gw), lambda b, g, i: (b * nq + i, g))
    return nq, in_specs, gate_spec, out_spec


def _dsa_attn(qT, k, vT, bias, z):
    b_count, t, _ = k.shape
    gate_blk = (INNER + 2 * KV_WIDTH) // (GROUP * HEAD_DIM)
    nq, in_specs, gate_spec, out_spec = _gqa_specs(t, gate_blk)
    in_specs += [pl.BlockSpec((1, t, TQ), lambda b, g, i: (b, 0, i)), gate_spec]
    return pl.pallas_call(
        _dsa_attn_kernel,
        grid=(b_count, N_KV_HEADS, nq),
        in_specs=in_specs, out_specs=out_spec,
        out_shape=jax.ShapeDtypeStruct((b_count * t, INNER), BF16),
        scratch_shapes=_attn_scratch(),
        compiler_params=_cparams("parallel", "parallel", "parallel"),
        name="dsa_attn",
    )(qT, k, vT, bias, z)


def _moba_attn(qT, k, vT, kmean, z):
    b_count, t, _ = k.shape
    gate_blk = (INNER + 2 * KV_WIDTH) // (GROUP * HEAD_DIM)
    nq, in_specs, gate_spec, out_spec = _gqa_specs(t, gate_blk)
    n_blocks = t // MOBA_BLOCK
    in_specs += [pl.BlockSpec((1, 1, n_blocks, GROUP * HEAD_DIM), lambda b, g, i: (b, g, 0, 0)), gate_spec]
    return pl.pallas_call(
        _moba_attn_kernel,
        grid=(b_count, N_KV_HEADS, nq),
        in_specs=in_specs, out_specs=out_spec,
        out_shape=jax.ShapeDtypeStruct((b_count * t, INNER), BF16),
        scratch_shapes=_attn_scratch() + [pltpu.VMEM((n_blocks, TQ), F32)],
        compiler_params=_cparams("parallel", "parallel", "parallel"),
        name="moba_attn",
    )(qT, k, vT, kmean, z)


def _fox_attn(qT, k, vT, cum, cumT, z):
    b_count, _, t, _ = k.shape
    nq = t // TQ
    gw = GROUP * HEAD_DIM
    gate_blk = 3 * INNER // gw
    return pl.pallas_call(
        _fox_attn_kernel,
        grid=(b_count, N_HEADS // GROUP, nq),
        in_specs=[pl.BlockSpec((1, gw, TQ), lambda b, g, i: (b, g, i)),
                  pl.BlockSpec((1, GROUP, t, HEAD_DIM), lambda b, g, i: (b, g, 0, 0)),
                  pl.BlockSpec((1, gw, t), lambda b, g, i: (b, g, 0)),
                  pl.BlockSpec((1, t, LANE), lambda b, g, i: (b, 0, 0)),
                  pl.BlockSpec((1, LANE, TQ), lambda b, g, i: (b, 0, i)),
                  pl.BlockSpec((TQ, gw), lambda b, g, i: (b * nq + i, gate_blk + g))],
        out_specs=pl.BlockSpec((TQ, gw), lambda b, g, i: (b * nq + i, g)),
        out_shape=jax.ShapeDtypeStruct((b_count * t, INNER), BF16),
        scratch_shapes=_attn_scratch(),
        compiler_params=_cparams("parallel", "parallel", "parallel"),
        name="fox_attn",
    )(qT, k, vT, cum, cumT, z)


def _retention_kernel(q_ref, k_ref, v_ref, gate_ref, cos_ref, sin_ref, lg_ref, gn_ref, o_ref,
                      state_ref, dmask_ref):
    c = RET_CHUNK
    lg = lg_ref[0][:, :1]

    @pl.when(pl.program_id(2) == 0)
    def _():
        state_ref[...] = jnp.zeros_like(state_ref)
        diff = (lax.broadcasted_iota(I32, (c, c), 0) - lax.broadcasted_iota(I32, (c, c), 1)).astype(F32)
        dmask_ref[...] = jnp.where(diff >= 0, jnp.exp(jnp.maximum(diff, 0.0) * lg), 0.0)

    cos = cos_ref[...]
    sin = sin_ref[...]
    half = RET_QK_DIM // 2

    def rope(ref):
        x1 = ref[:, :half]
        x2 = ref[:, half:]
        return jnp.concatenate([x1 * cos - x2 * sin, x1 * sin + x2 * cos], axis=1)

    q = rope(q_ref)
    k = rope(k_ref) * (RET_QK_DIM ** -0.5)
    v = v_ref[...].astype(BF16)
    i_col = lax.broadcasted_iota(I32, (c, 1), 0).astype(F32)
    q_decay = jnp.exp((i_col + 1.0) * lg)
    k_decay = jnp.exp((c - 1.0 - i_col) * lg)
    inner = lax.dot_general(q.astype(BF16), k.astype(BF16), (((1,), (1,)), ((), ())),
                            preferred_element_type=F32) * dmask_ref[...]
    state = state_ref[...]
    o = (jnp.dot(inner.astype(BF16), v, preferred_element_type=F32)
         + jnp.dot((q * q_decay).astype(BF16), state.astype(BF16), preferred_element_type=F32))
    kdT = (k * k_decay).T.astype(BF16)
    state_ref[...] = state * jnp.exp(c * lg) + jnp.dot(kdT, v, preferred_element_type=F32)

    mu = jnp.mean(o, axis=-1, keepdims=True)
    var = jnp.mean(jnp.square(o - mu), axis=-1, keepdims=True)
    on = (o - mu) * lax.rsqrt(var + EPS) * gn_ref[...]
    gate = gate_ref[...]
    o_ref[...] = (gate * jax.nn.sigmoid(gate) * on).astype(BF16)


def _retention(z, b_count, t, gn_g):
    c = RET_CHUNK
    nc = t // c
    half = RET_QK_DIM // 2
    pos = jnp.arange(t, dtype=jnp.int32)
    inv = ROPE_THETA ** (-jnp.arange(0, RET_QK_DIM, 2, dtype=F32) / RET_QK_DIM)
    ang = pos.astype(F32)[:, None] * inv[None, :]
    cos, sin = jnp.cos(ang), jnp.sin(ang)
    log_gamma = jnp.log(1.0 - 2.0 ** (-5.0 - jnp.arange(RET_HEADS, dtype=F32)))
    lg = jnp.broadcast_to(log_gamma[:, None, None], (RET_HEADS, 1, LANE))
    qk_w = RET_HEADS * RET_QK_DIM
    row = lambda b, h, i: b * nc + i
    return pl.pallas_call(
        _retention_kernel,
        grid=(b_count, RET_HEADS, nc),
        in_specs=[pl.BlockSpec((c, RET_QK_DIM), lambda b, h, i: (row(b, h, i), h)),
                  pl.BlockSpec((c, RET_QK_DIM), lambda b, h, i: (row(b, h, i), RET_HEADS + h)),
                  pl.BlockSpec((c, RET_V_DIM), lambda b, h, i: (row(b, h, i), 2 * qk_w // RET_V_DIM + h)),
                  pl.BlockSpec((c, RET_V_DIM),
                               lambda b, h, i: (row(b, h, i), (2 * qk_w + RET_INNER) // RET_V_DIM + h)),
                  pl.BlockSpec((c, half), lambda b, h, i: (i, 0)),
                  pl.BlockSpec((c, half), lambda b, h, i: (i, 0)),
                  pl.BlockSpec((1, 1, LANE), lambda b, h, i: (h, 0, 0)),
                  pl.BlockSpec((1, RET_V_DIM), lambda b, h, i: (0, h))],
        out_specs=pl.BlockSpec((c, RET_V_DIM), lambda b, h, i: (row(b, h, i), h)),
        out_shape=jax.ShapeDtypeStruct((b_count * t, RET_INNER), BF16),
        scratch_shapes=[pltpu.VMEM((RET_QK_DIM, RET_V_DIM), F32), pltpu.VMEM((c, c), F32)],
        compiler_params=_cparams("parallel", "parallel", "arbitrary"),
        name="retention",
    )(z, z, z, z, cos, sin, lg, gn_g.reshape(1, RET_INNER))


def _dsa_layer(h, b_count, t, norm_g, w_in, q_g, k_g, w_out):
    n_pad = 49 * LANE
    z = _matmul(_rmsnorm(h, norm_g), _weight_bf16(w_in, n_pad), 1024, 7 * LANE, "dsa_in_proj")
    qT, k, vT, qiT, kia, kib, wT = _dsa_prep(z, b_count, t, q_g, k_g)
    bias = _dsa_select(kia, kib, qiT, wT)
    gated = _dsa_attn(qT, k, vT, bias, z)
    return _matmul_residual(gated, _weight_bf16(w_out), h, 1024, 1024, "dsa_out_proj")


def _moba_layer(h, b_count, t, norm_g, w_in, q_g, k_g, w_out):
    z = _matmul(_rmsnorm(h, norm_g), _weight_bf16(w_in), 1024, 1024, "moba_in_proj")
    qT, k, vT, kmean = _moba_prep(z, b_count, t, q_g, k_g)
    gated = _moba_attn(qT, k, vT, kmean, z)
    return _matmul_residual(gated, _weight_bf16(w_out), h, 1024, 1024, "moba_out_proj")


def _ret_layer(h, b_count, t, norm_g, w_in, gn_g, w_out):
    z = _matmul(_rmsnorm(h, norm_g), _weight_bf16(w_in), 1024, 1024, "ret_in_proj")
    gated = _retention(z, b_count, t, gn_g)
    return _matmul_residual(gated, _weight_bf16(w_out), h, 512, 1024, "ret_out_proj")


def _fox_layer(h, b_count, t, norm_g, w_in, f_bias, q_g, k_g, w_out):
    n_pad = 65 * LANE
    z = _matmul(_rmsnorm(h, norm_g), _weight_bf16(w_in, n_pad), 1024, 5 * LANE, "fox_in_proj")
    qT, k, vT, cum, cumT = _fox_prep(z, b_count, t, f_bias, q_g, k_g)
    gated = _fox_attn(qT, k, vT, cum, cumT, z)
    return _matmul_residual(gated, _weight_bf16(w_out), h, 1024, 1024, "fox_out_proj")


def kernel(x, a_norm, a_w_in, a_q_norm, a_k_norm, a_w_out, b_norm, b_w_in, b_q_norm, b_k_norm, b_w_out,
           c_norm, c_w_in, c_gn, c_w_out, d_norm, d_w_in, d_f_bias, d_q_norm, d_k_norm, d_w_out):
    b_count, t, d = x.shape
    assert d == D_MODEL and t % TQ == 0 and t // 4 >= DSA_TOPK
    h = x.reshape(b_count * t, d)
    depth = 4
    for i in range(depth):
        m, j = i % 4, i // 4
        if m == 0:
            h = _dsa_layer(h, b_count, t, a_norm[j], a_w_in[j], a_q_norm[j], a_k_norm[j], a_w_out[j])
        elif m == 1:
            h = _moba_layer(h, b_count, t, b_norm[j], b_w_in[j], b_q_norm[j], b_k_norm[j], b_w_out[j])
        elif m == 2:
            h = _ret_layer(h, b_count, t, c_norm[j], c_w_in[j], c_gn[j], c_w_out[j])
        else:
            h = _fox_layer(h, b_count, t, d_norm[j], d_w_in[j], d_f_bias[j], d_q_norm[j], d_k_norm[j], d_w_out[j])
    return h.reshape(b_count, t, d)
```

```python
import functools

import jax
import jax.numpy as jnp
import numpy as np
from jax import lax
from jax.experimental import pallas as pl
from jax.experimental.pallas import tpu as pltpu

F32 = jnp.float32
BF16 = jnp.bfloat16
I32 = jnp.int32

D_MODEL = 2048
HEAD_DIM = 128
N_HEADS = 16
N_KV_HEADS = 4
GROUP = N_HEADS // N_KV_HEADS
INNER = N_HEADS * HEAD_DIM
KV_WIDTH = N_KV_HEADS * HEAD_DIM
IDX_HEADS = 16
IDX_DIM = 64
DSA_TOPK = 256
MOBA_BLOCK = 256
MOBA_TOPK = 3
RET_HEADS = 8
RET_QK_DIM = 256
RET_V_DIM = 512
RET_INNER = RET_HEADS * RET_V_DIM
ROPE_THETA = 10000.0
EPS = 1e-6
NEG = -1e30
INT_MIN = -(2 ** 31)
LOG2E = 1.4426950408889634

LANE = 128
TQ = 256
TK = 256
RET_CHUNK = 256
VMEM_LIMIT = 56 * 1024 * 1024


def _cparams(*sem):
    return pltpu.CompilerParams(dimension_semantics=sem, vmem_limit_bytes=VMEM_LIMIT)


def _rmsnorm_kernel(x_ref, g_ref, o_ref):
    x = x_ref[...]
    ms = jnp.mean(x * x, axis=-1, keepdims=True)
    o_ref[...] = (x * lax.rsqrt(ms + EPS) * g_ref[...]).astype(o_ref.dtype)


def _rmsnorm(x, g, tm=512):
    m, d = x.shape
    return pl.pallas_call(
        _rmsnorm_kernel,
        grid=(m // tm,),
        in_specs=[pl.BlockSpec((tm, d), lambda i: (i, 0)),
                  pl.BlockSpec((1, d), lambda i: (0, 0))],
        out_specs=pl.BlockSpec((tm, d), lambda i: (i, 0)),
        out_shape=jax.ShapeDtypeStruct((m, d), BF16),
        compiler_params=_cparams("parallel"),
        name="rmsnorm",
    )(x, g.reshape(1, d))


def _mm_kernel(a_ref, w_ref, o_ref):
    o_ref[...] = jnp.dot(a_ref[...], w_ref[...], preferred_element_type=F32).astype(o_ref.dtype)


def _in_proj_kernel(a_ref, w_ref, o_ref, wb_ref):
    @pl.when(pl.program_id(1) == 0)
    def _():
        wb_ref[...] = w_ref[...].astype(BF16)

    o_ref[...] = jnp.dot(a_ref[...], wb_ref[...], preferred_element_type=F32)


def _in_proj(a, w, n, tm, tn, name):
    m, k = a.shape
    return pl.pallas_call(
        _in_proj_kernel,
        grid=(n // tn, m // tm),
        in_specs=[pl.BlockSpec((tm, k), lambda j, i: (i, 0)),
                  pl.BlockSpec((k, tn), lambda j, i: (0, j))],
        out_specs=pl.BlockSpec((tm, tn), lambda j, i: (i, j)),
        out_shape=jax.ShapeDtypeStruct((m, n), F32),
        scratch_shapes=[pltpu.VMEM((k, tn), BF16)],
        compiler_params=_cparams("parallel", "arbitrary"),
        name=name,
    )(a, w)


def _out_proj_kernel(a_ref, w_ref, r_ref, o_ref):
    o_ref[...] = r_ref[...] + jnp.dot(a_ref[...], w_ref[...], preferred_element_type=F32)


def _out_proj_norm_kernel(a_ref, w_ref, r_ref, g_ref, o_ref, hn_ref):
    h = r_ref[...] + jnp.dot(a_ref[...], w_ref[...], preferred_element_type=F32)
    o_ref[...] = h
    ms = jnp.mean(h * h, axis=-1, keepdims=True)
    hn_ref[...] = (h * lax.rsqrt(ms + EPS) * g_ref[...]).astype(BF16)


def _out_proj(a, w, res, next_g, tm, name):
    m, k = a.shape
    n = w.shape[1]
    in_specs = [pl.BlockSpec((tm, k), lambda i: (i, 0)),
                pl.BlockSpec((k, n), lambda i: (0, 0)),
                pl.BlockSpec((tm, n), lambda i: (i, 0))]
    out_specs = [pl.BlockSpec((tm, n), lambda i: (i, 0))]
    out_shape = [jax.ShapeDtypeStruct((m, n), F32)]
    args = [a, w, res]
    if next_g is not None:
        in_specs.append(pl.BlockSpec((1, n), lambda i: (0, 0)))
        out_specs.append(pl.BlockSpec((tm, n), lambda i: (i, 0)))
        out_shape.append(jax.ShapeDtypeStruct((m, n), BF16))
        args.append(next_g.reshape(1, n))
    out = pl.pallas_call(
        _out_proj_kernel if next_g is None else _out_proj_norm_kernel,
        grid=(m // tm,),
        in_specs=in_specs, out_specs=out_specs, out_shape=out_shape,
        compiler_params=_cparams("parallel"),
        name=name,
    )(*args)
    return (out[0], out[1]) if next_g is not None else (out[0], None)


def _matmul(a, w, tm, tn, name):
    m, k = a.shape
    n = w.shape[1]
    return pl.pallas_call(
        _mm_kernel,
        grid=(m // tm, n // tn),
        in_specs=[pl.BlockSpec((tm, k), lambda i, j: (i, 0)),
                  pl.BlockSpec((k, tn), lambda i, j: (0, j))],
        out_specs=pl.BlockSpec((tm, tn), lambda i, j: (i, j)),
        out_shape=jax.ShapeDtypeStruct((m, n), F32),
        compiler_params=_cparams("parallel", "parallel"),
        name=name,
    )(a, w)


def _rope_tables(t, d, reps):
    pos = jnp.arange(t, dtype=jnp.int32)
    inv = ROPE_THETA ** (-jnp.arange(0, d, 2, dtype=F32) / d)
    ang = pos.astype(F32)[:, None] * inv[None, :]
    cos, sin = jnp.cos(ang), jnp.sin(ang)
    cos_t = jnp.tile(jnp.concatenate([cos, cos], axis=-1), (1, reps))
    sin_t = jnp.tile(jnp.concatenate([-sin, sin], axis=-1), (1, reps))
    return cos_t, sin_t


def _head_norm(x, g):
    ms = jnp.mean(x * x, axis=-1, keepdims=True)
    return x * lax.rsqrt(ms + EPS) * g


def _rope128(y, cos, sin):
    return y * cos + pltpu.roll(y, 64, 1) * sin


def _qkv_prep_body(q_ref, k_ref, v_ref, cos_ref, sin_ref, qg_ref, kg_ref,
                   qT_ref, ko_ref, vT_ref, *, n_kv, rope):
    qg = qg_ref[...]
    kg = kg_ref[...]
    scale = HEAD_DIM ** -0.5 * LOG2E
    if rope:
        cos = cos_ref[...]
        sin = sin_ref[...]
    for h in range(N_HEADS):
        sl = slice(h * HEAD_DIM, (h + 1) * HEAD_DIM)
        y = _head_norm(q_ref[:, sl], qg)
        if rope:
            y = _rope128(y, cos, sin)
        qT_ref[0, sl, :] = (y * scale).T.astype(BF16)
    k_out = []
    for g in range(n_kv):
        sl = slice(g * HEAD_DIM, (g + 1) * HEAD_DIM)
        y = _head_norm(k_ref[:, sl], kg)
        if rope:
            y = _rope128(y, cos, sin)
        if ko_ref.ndim == 4:
            ko_ref[0, g] = y.astype(BF16)
        else:
            ko_ref[0, :, sl] = y.astype(BF16)
        vT_ref[0, sl, :] = v_ref[:, sl].T.astype(BF16)
        k_out.append(y)
    return k_out


def _dsa_prep_kernel(q_ref, k_ref, v_ref, qi_ref, tail_ref, cos_ref, sin_ref, cos64_ref, sin64_ref,
                     qg_ref, kg_ref, qT_ref, ko_ref, vT_ref, qiT_ref, kia_ref, kib_ref, wT_ref):
    _qkv_prep_body(q_ref, k_ref, v_ref, cos_ref, sin_ref, qg_ref, kg_ref, qT_ref, ko_ref, vT_ref,
                   n_kv=N_KV_HEADS, rope=True)
    tm = tail_ref.shape[0]
    lane = lax.broadcasted_iota(I32, (tm, LANE), 1)
    first_half = (lane % IDX_DIM) < (IDX_DIM // 2)
    c64 = cos64_ref[...]
    s64 = sin64_ref[...]

    def rope64(x):
        rot = jnp.where(first_half, pltpu.roll(x, LANE - IDX_DIM // 2, 1), pltpu.roll(x, IDX_DIM // 2, 1))
        return x * c64 + rot * s64

    for j in range(IDX_HEADS * IDX_DIM // LANE):
        sl = slice(j * LANE, (j + 1) * LANE)
        y = rope64(qi_ref[:, sl]) * (IDX_DIM ** -0.5)
        qiT_ref[0, sl, :] = y.T.astype(BF16)
    tail = tail_ref[...]
    ka = jnp.where(lane < IDX_DIM, rope64(tail), 0.0)
    kia_ref[0] = ka.astype(BF16)
    kib_ref[0] = pltpu.roll(ka, IDX_DIM, 1).astype(BF16)
    wT_ref[0] = (tail * (IDX_HEADS ** -0.5)).T


def _moba_prep_kernel(q_ref, k_ref, v_ref, cos_ref, sin_ref, qg_ref, kg_ref,
                      qT_ref, ko_ref, vT_ref, kmean_ref):
    k_out = _qkv_prep_body(q_ref, k_ref, v_ref, cos_ref, sin_ref, qg_ref, kg_ref, qT_ref, ko_ref, vT_ref,
                           n_kv=N_KV_HEADS, rope=True)
    for g in range(N_KV_HEADS):
        km = jnp.mean(k_out[g], axis=0, keepdims=True)
        kmean_ref[0, g, pl.ds(pl.program_id(1), 1), :] = jnp.concatenate([km] * GROUP, axis=1)


def _fox_prep_kernel(q_ref, k_ref, v_ref, f_ref, fb_ref, qg_ref, kg_ref,
                     qT_ref, ko_ref, vT_ref, cum_ref, cumT_ref, carry_ref):
    _qkv_prep_body(q_ref, k_ref, v_ref, None, None, qg_ref, kg_ref, qT_ref, ko_ref, vT_ref,
                   n_kv=N_HEADS, rope=False)

    @pl.when(pl.program_id(1) == 0)
    def _():
        carry_ref[...] = jnp.zeros_like(carry_ref)

    tm = f_ref.shape[0]
    x = f_ref[...] + fb_ref[...]
    lf = jnp.minimum(x, 0.0) - jnp.log(1.0 + jnp.exp(-jnp.abs(x)))
    hi = lf.astype(BF16)
    r1 = lf - hi.astype(F32)
    lo = r1.astype(BF16)
    lo2 = (r1 - lo.astype(F32)).astype(BF16)
    row = lax.broadcasted_iota(I32, (tm, tm), 0)
    col = lax.broadcasted_iota(I32, (tm, tm), 1)
    tri = (col <= row).astype(BF16)
    parts = jnp.dot(tri, jnp.concatenate([hi, lo, lo2], axis=1), preferred_element_type=F32)
    cum = parts[:, :LANE] + parts[:, LANE:2 * LANE] + parts[:, 2 * LANE:] + carry_ref[...]
    carry_ref[...] = cum[tm - 1:tm, :]
    cum2 = cum * LOG2E
    cum_ref[0] = cum2
    cumT_ref[0] = cum2.T


def _prep_specs(b_count, t, tm, z_q_blk, z_k_blk, z_v_blk, kv_width):
    nt = t // tm
    row = lambda b, i: b * nt + i
    in_specs = [
        pl.BlockSpec((tm, INNER), lambda b, i: (row(b, i), z_q_blk)),
        pl.BlockSpec((tm, kv_width), lambda b, i: (row(b, i), z_k_blk)),
        pl.BlockSpec((tm, kv_width), lambda b, i: (row(b, i), z_v_blk)),
    ]
    out_specs = [
        pl.BlockSpec((1, INNER, tm), lambda b, i: (b, 0, i)),
        pl.BlockSpec((1, tm, kv_width), lambda b, i: (b, i, 0)),
        pl.BlockSpec((1, kv_width, tm), lambda b, i: (b, 0, i)),
    ]
    out_shape = [
        jax.ShapeDtypeStruct((b_count, INNER, t), BF16),
        jax.ShapeDtypeStruct((b_count, t, kv_width), BF16),
        jax.ShapeDtypeStruct((b_count, kv_width, t), BF16),
    ]
    return row, in_specs, out_specs, out_shape


def _dsa_prep(z, z_tail, b_count, t, q_g, k_g, tm=256):
    row, in_specs, out_specs, out_shape = _prep_specs(b_count, t, tm, 0, INNER // KV_WIDTH,
                                                      INNER // KV_WIDTH + 1, KV_WIDTH)
    qi_w = IDX_HEADS * IDX_DIM
    qi_off = 2 * INNER + 2 * KV_WIDTH
    cos, sin = _rope_tables(t, HEAD_DIM, 1)
    cos64, sin64 = _rope_tables(t, IDX_DIM, LANE // IDX_DIM)
    tab = pl.BlockSpec((tm, LANE), lambda b, i: (i, 0))
    gain = pl.BlockSpec((1, HEAD_DIM), lambda b, i: (0, 0))
    in_specs += [
        pl.BlockSpec((tm, qi_w), lambda b, i: (row(b, i), qi_off // qi_w)),
        pl.BlockSpec((tm, LANE), lambda b, i: (row(b, i), 0)),
        tab, tab, tab, tab, gain, gain,
    ]
    out_specs += [
        pl.BlockSpec((1, qi_w, tm), lambda b, i: (b, 0, i)),
        pl.BlockSpec((1, tm, LANE), lambda b, i: (b, i, 0)),
        pl.BlockSpec((1, tm, LANE), lambda b, i: (b, i, 0)),
        pl.BlockSpec((1, LANE, tm), lambda b, i: (b, 0, i)),
    ]
    out_shape += [
        jax.ShapeDtypeStruct((b_count, qi_w, t), BF16),
        jax.ShapeDtypeStruct((b_count, t, LANE), BF16),
        jax.ShapeDtypeStruct((b_count, t, LANE), BF16),
        jax.ShapeDtypeStruct((b_count, LANE, t), F32),
    ]
    return pl.pallas_call(
        _dsa_prep_kernel,
        grid=(b_count, t // tm),
        in_specs=in_specs, out_specs=out_specs, out_shape=out_shape,
        compiler_params=_cparams("parallel", "parallel"),
        name="dsa_prep",
    )(z, z, z, z, z_tail, cos, sin, cos64, sin64, q_g.reshape(1, HEAD_DIM), k_g.reshape(1, HEAD_DIM))


def _moba_prep(z, b_count, t, q_g, k_g):
    tm = MOBA_BLOCK
    row, in_specs, out_specs, out_shape = _prep_specs(b_count, t, tm, 0, INNER // KV_WIDTH,
                                                      INNER // KV_WIDTH + 1, KV_WIDTH)
    cos, sin = _rope_tables(t, HEAD_DIM, 1)
    tab = pl.BlockSpec((tm, LANE), lambda b, i: (i, 0))
    gain = pl.BlockSpec((1, HEAD_DIM), lambda b, i: (0, 0))
    in_specs += [tab, tab, gain, gain]
    out_specs += [pl.BlockSpec((1, N_KV_HEADS, t // tm, GROUP * HEAD_DIM), lambda b, i: (b, 0, 0, 0))]
    out_shape += [jax.ShapeDtypeStruct((b_count, N_KV_HEADS, t // tm, GROUP * HEAD_DIM), F32)]
    return pl.pallas_call(
        _moba_prep_kernel,
        grid=(b_count, t // tm),
        in_specs=in_specs, out_specs=out_specs, out_shape=out_shape,
        compiler_params=_cparams("parallel", "arbitrary"),
        name="moba_prep",
    )(z, z, z, cos, sin, q_g.reshape(1, HEAD_DIM), k_g.reshape(1, HEAD_DIM))


def _fox_prep(z, z_tail, b_count, t, f_bias, q_g, k_g, tm=256):
    row, in_specs, out_specs, out_shape = _prep_specs(b_count, t, tm, 0, 1, 2, INNER)
    out_specs[1] = pl.BlockSpec((1, N_HEADS, tm, HEAD_DIM), lambda b, i: (b, 0, i, 0))
    out_shape[1] = jax.ShapeDtypeStruct((b_count, N_HEADS, t, HEAD_DIM), BF16)
    gain = pl.BlockSpec((1, HEAD_DIM), lambda b, i: (0, 0))
    fb = jnp.pad(f_bias.reshape(1, N_HEADS), ((0, 0), (0, LANE - N_HEADS)))
    in_specs += [
        pl.BlockSpec((tm, LANE), lambda b, i: (row(b, i), 0)),
        pl.BlockSpec((1, LANE), lambda b, i: (0, 0)),
        gain, gain,
    ]
    out_specs += [
        pl.BlockSpec((1, tm, LANE), lambda b, i: (b, i, 0)),
        pl.BlockSpec((1, LANE, tm), lambda b, i: (b, 0, i)),
    ]
    out_shape += [
        jax.ShapeDtypeStruct((b_count, t, LANE), F32),
        jax.ShapeDtypeStruct((b_count, LANE, t), F32),
    ]
    return pl.pallas_call(
        _fox_prep_kernel,
        grid=(b_count, t // tm),
        in_specs=in_specs, out_specs=out_specs, out_shape=out_shape,
        scratch_shapes=[pltpu.VMEM((1, LANE), F32)],
        compiler_params=_cparams("parallel", "arbitrary"),
        name="fox_prep",
    )(z, z, z, z_tail, fb, q_g.reshape(1, HEAD_DIM), k_g.reshape(1, HEAD_DIM))


def _dsa_select_kernel(kia_ref, kib_ref, qiT_ref, wT_ref, bias_ref, key_ref, ans_ref):
    qb = pl.program_id(1)
    n_kt = key_ref.shape[0] // TK
    w = wT_ref[0]
    s_iota = lax.broadcasted_iota(I32, (TK, TQ), 0)
    t_iota = lax.broadcasted_iota(I32, (TK, TQ), 1)

    for kt in range(n_kt):
        rows = slice(kt * TK, (kt + 1) * TK)

        @pl.when(kt <= qb)
        def _():
            ka = kia_ref[0, rows, :]
            kb = kib_ref[0, rows, :]
            score = jnp.zeros((TK, TQ), F32)
            for j in range(IDX_HEADS // 2):
                qp = qiT_ref[0, j * LANE:(j + 1) * LANE, :]
                ra = jnp.maximum(jnp.dot(ka, qp, preferred_element_type=F32), 0.0)
                rb = jnp.maximum(jnp.dot(kb, qp, preferred_element_type=F32), 0.0)
                wa = w[IDX_DIM + 2 * j:IDX_DIM + 2 * j + 1, :]
                wb = w[IDX_DIM + 2 * j + 1:IDX_DIM + 2 * j + 2, :]
                score = score + (ra * wa + rb * wb)
            bits = lax.bitcast_convert_type(score + 0.0, I32)
            key = bits ^ ((bits >> 31) & 0x7FFFFFFF)
            causal = (kt * TK + s_iota) <= (qb * TQ + t_iota)
            key_ref[rows, :] = jnp.where(causal, key, INT_MIN)

        @pl.when(kt > qb)
        def _():
            key_ref[rows, :] = jnp.full((TK, TQ), INT_MIN, I32)

    def variant(v):
        n = (v + 1) * TK
        kf = float(DSA_TOPK)

        def count_ge(cand):
            return jnp.sum((key_ref[:n, :] >= cand).astype(F32), axis=0, keepdims=True)

        def bit_step(i, ans):
            cand = ans | (jnp.int32(1) << (30 - i))
            return jnp.where(count_ge(cand) >= kf, cand, ans)

        sign = jnp.where(count_ge(jnp.zeros((1, TQ), I32)) >= kf, 0, INT_MIN).astype(I32)
        ans_ref[...] = lax.fori_loop(0, 31, bit_step, sign)

    _for_each_query_tile(qb, n_kt, variant)
    ans = ans_ref[...]

    for kt in range(n_kt):
        rows = slice(kt * TK, (kt + 1) * TK)
        k = key_ref[rows, :]
        sel = (k >= ans) & (k > INT_MIN)
        bias_ref[0, rows, :] = jnp.where(sel, 0.0, NEG).astype(BF16)


def _dsa_select(kia, kib, qiT, wT):
    b_count, t, _ = kia.shape
    return pl.pallas_call(
        _dsa_select_kernel,
        grid=(b_count, t // TQ),
        in_specs=[pl.BlockSpec((1, t, LANE), lambda b, i: (b, 0, 0)),
                  pl.BlockSpec((1, t, LANE), lambda b, i: (b, 0, 0)),
                  pl.BlockSpec((1, IDX_HEADS * IDX_DIM, TQ), lambda b, i: (b, 0, i)),
                  pl.BlockSpec((1, LANE, TQ), lambda b, i: (b, 0, i))],
        out_specs=pl.BlockSpec((1, t, TQ), lambda b, i: (b, 0, i)),
        out_shape=jax.ShapeDtypeStruct((b_count, t, t), BF16),
        scratch_shapes=[pltpu.VMEM((t, TQ), I32), pltpu.VMEM((1, TQ), I32)],
        compiler_params=_cparams("parallel", "parallel"),
        name="dsa_select",
    )(kia, kib, qiT, wT)


def _causal_tile():
    s_iota = lax.broadcasted_iota(I32, (TK, TQ), 0)
    t_iota = lax.broadcasted_iota(I32, (TK, TQ), 1)
    return s_iota <= t_iota


def _softmax_attend(q, k_all, vt_all, bias_fn, n_tiles, causal_last):
    return _softmax_attend_heads([q], [k_all], [vt_all], [bias_fn], n_tiles, causal_last)[0]


def _softmax_attend_heads(qs, k_alls, vt_alls, bias_fns, n_tiles, causal_last):
    heads = range(len(qs))
    s_alls = [jnp.dot(k_alls[i], qs[i], preferred_element_type=F32) for i in heads]
    tiles = [[] for _ in heads]
    for i in heads:
        for j in range(n_tiles):
            sj = bias_fns[i](j, s_alls[i][j * TK:(j + 1) * TK])
            if causal_last and j == n_tiles - 1:
                sj = jnp.where(_causal_tile(), sj, NEG)
            tiles[i].append(sj)
    ms = [functools.reduce(jnp.maximum, [jnp.max(sj, axis=0, keepdims=True) for sj in tiles[i]]) for i in heads]
    outs = []
    for i in heads:
        ps = [jnp.exp2(sj - ms[i]) for sj in tiles[i]]
        l = functools.reduce(jnp.add, [jnp.sum(p, axis=0, keepdims=True) for p in ps])
        ps = [p.astype(BF16) for p in ps]
        p_all = ps[0] if n_tiles == 1 else jnp.concatenate(ps, axis=0)
        outs.append(jnp.dot(vt_alls[i], p_all, preferred_element_type=F32) * (1.0 / l))
    return outs


def _for_each_query_tile(qb, n_q, body):
    for v in range(n_q):
        pl.when(qb == v)(functools.partial(body, v))


def _pair_rows(i):
    base = pl.multiple_of(i * 2 * HEAD_DIM, 2 * HEAD_DIM)
    return [pl.ds(base, HEAD_DIM), pl.ds(base + HEAD_DIM, HEAD_DIM)]


def _gate_and_store(oT_ref, gate_ref, o_ref):
    for r in range(GROUP):
        sl = slice(r * HEAD_DIM, (r + 1) * HEAD_DIM)
        gate = gate_ref[:, sl]
        o_ref[:, sl] = (gate * jax.nn.sigmoid(gate) * oT_ref[sl, :].T).astype(BF16)


def _dsa_attn_kernel(qT_ref, k_ref, vT_ref, bias_ref, gate_ref, o_ref, oT_ref):
    def variant(v):
        n = (v + 1) * TK

        def bias_fn(j, s):
            return s + bias_ref[0, j * TK:(j + 1) * TK, :].astype(F32)

        def head_pair(i, carry):
            rows = _pair_rows(i)
            outs = _softmax_attend_heads([qT_ref[0, r, :] for r in rows], [k_ref[0, :n, :]] * 2,
                                         [vT_ref[0, :, :n]] * 2, [bias_fn] * 2, v + 1, False)
            for r, o in zip(rows, outs):
                oT_ref[r, :] = o
            return carry

        lax.fori_loop(0, GROUP // 2, head_pair, 0)

    _for_each_query_tile(pl.program_id(2), k_ref.shape[1] // TQ, variant)
    _gate_and_store(oT_ref, gate_ref, o_ref)


def _moba_attn_kernel(qT_ref, k_ref, vT_ref, kmean_ref, gate_ref, o_ref, oT_ref, selb_ref):
    own = pl.program_id(2)
    n_blocks = kmean_ref.shape[2]

    km = kmean_ref[0, 0].astype(BF16)
    gs = jnp.dot(km, qT_ref[0], preferred_element_type=F32)
    blk = lax.broadcasted_iota(I32, (n_blocks, TQ), 0)
    past = blk < own
    gs = jnp.where(past, gs, NEG)
    sel = jnp.zeros((n_blocks, TQ), jnp.bool_)
    for _ in range(min(MOBA_TOPK, n_blocks - 1)):
        mx = jnp.max(gs, axis=0, keepdims=True)
        first = jnp.min(jnp.where(gs == mx, blk, n_blocks), axis=0, keepdims=True)
        pick = blk == first
        sel = sel | pick
        gs = jnp.where(pick, -jnp.inf, gs)
    selb_ref[...] = jnp.where(sel & past, 0.0, NEG)

    def variant(v):
        n = (v + 1) * TK

        def bias_fn(j, s):
            return s if j == v else s + selb_ref[j:j + 1, :]

        def head_pair(i, carry):
            rows = _pair_rows(i)
            outs = _softmax_attend_heads([qT_ref[0, r, :] for r in rows], [k_ref[0, :n, :]] * 2,
                                         [vT_ref[0, :, :n]] * 2, [bias_fn] * 2, v + 1, True)
            for r, o in zip(rows, outs):
                oT_ref[r, :] = o
            return carry

        lax.fori_loop(0, GROUP // 2, head_pair, 0)

    _for_each_query_tile(own, n_blocks, variant)
    _gate_and_store(oT_ref, gate_ref, o_ref)


def _fox_attn_kernel(qT_ref, k_ref, vT_ref, cum_ref, cumT_ref, gate_ref, o_ref, oT_ref):
    g = pl.program_id(1)

    def variant(v):
        n = (v + 1) * TK

        def make_bias_fn(h):
            cq = cumT_ref[0, pl.ds(h, 1), :]
            onehot = (lax.broadcasted_iota(I32, (TK, LANE), 1) == h).astype(F32)

            def bias_fn(j, s):
                ck = jnp.sum(cum_ref[0, j * TK:(j + 1) * TK, :] * onehot, axis=1, keepdims=True)
                return s + (cq - ck)

            return bias_fn

        def head_pair(i, carry):
            rows = _pair_rows(i)
            heads = [2 * i, 2 * i + 1]
            outs = _softmax_attend_heads([qT_ref[0, r, :] for r in rows], [k_ref[0, r, :n, :] for r in heads],
                                         [vT_ref[0, r, :n] for r in rows],
                                         [make_bias_fn(g * GROUP + r) for r in heads], v + 1, True)
            for r, o in zip(rows, outs):
                oT_ref[r, :] = o
            return carry

        lax.fori_loop(0, GROUP // 2, head_pair, 0)

    _for_each_query_tile(pl.program_id(2), k_ref.shape[2] // TQ, variant)
    _gate_and_store(oT_ref, gate_ref, o_ref)


def _attn_scratch():
    return [pltpu.VMEM((GROUP * HEAD_DIM, TQ), F32)]


def _gqa_specs(t, gate_blk):
    nq = t // TQ
    gw = GROUP * HEAD_DIM
    in_specs = [
        pl.BlockSpec((1, gw, TQ), lambda b, g, i: (b, g, i)),
        pl.BlockSpec((1, t, HEAD_DIM), lambda b, g, i: (b, 0, g)),
        pl.BlockSpec((1, HEAD_DIM, t), lambda b, g, i: (b, g, 0)),
    ]
    gate_spec = pl.BlockSpec((TQ, gw), lambda b, g, i: (b * nq + i, gate_blk + g))
    out_spec = pl.BlockSpec((TQ, gw), lambda b, g, i: (b * nq + i, g))
    return nq, in_specs, gate_spec, out_spec


def _dsa_attn(qT, k, vT, bias, z):
    b_count, t, _ = k.shape
    gate_blk = (INNER + 2 * KV_WIDTH) // (GROUP * HEAD_DIM)
    nq, in_specs, gate_spec, out_spec = _gqa_specs(t, gate_blk)
    in_specs += [pl.BlockSpec((1, t, TQ), lambda b, g, i: (b, 0, i)), gate_spec]
    return pl.pallas_call(
        _dsa_attn_kernel,
        grid=(b_count, N_KV_HEADS, nq),
        in_specs=in_specs, out_specs=out_spec,
        out_shape=jax.ShapeDtypeStruct((b_count * t, INNER), BF16),
        scratch_shapes=_attn_scratch(),
        compiler_params=_cparams("parallel", "parallel", "parallel"),
        name="dsa_attn",
    )(qT, k, vT, bias, z)


def _moba_attn(qT, k, vT, kmean, z):
    b_count, t, _ = k.shape
    gate_blk = (INNER + 2 * KV_WIDTH) // (GROUP * HEAD_DIM)
    nq, in_specs, gate_spec, out_spec = _gqa_specs(t, gate_blk)
    n_blocks = t // MOBA_BLOCK
    in_specs += [pl.BlockSpec((1, 1, n_blocks, GROUP * HEAD_DIM), lambda b, g, i: (b, g, 0, 0)), gate_spec]
    return pl.pallas_call(
        _moba_attn_kernel,
        grid=(b_count, N_KV_HEADS, nq),
        in_specs=in_specs, out_specs=out_spec,
        out_shape=jax.ShapeDtypeStruct((b_count * t, INNER), BF16),
        scratch_shapes=_attn_scratch() + [pltpu.VMEM((n_blocks, TQ), F32)],
        compiler_params=_cparams("parallel", "parallel", "parallel"),
        name="moba_attn",
    )(qT, k, vT, kmean, z)


def _fox_attn(qT, k, vT, cum, cumT, z):
    b_count, _, t, _ = k.shape
    nq = t // TQ
    gw = GROUP * HEAD_DIM
    gate_blk = 3 * INNER // gw
    return pl.pallas_call(
        _fox_attn_kernel,
        grid=(b_count, N_HEADS // GROUP, nq),
        in_specs=[pl.BlockSpec((1, gw, TQ), lambda b, g, i: (b, g, i)),
                  pl.BlockSpec((1, GROUP, t, HEAD_DIM), lambda b, g, i: (b, g, 0, 0)),
                  pl.BlockSpec((1, gw, t), lambda b, g, i: (b, g, 0)),
                  pl.BlockSpec((1, t, LANE), lambda b, g, i: (b, 0, 0)),
                  pl.BlockSpec((1, LANE, TQ), lambda b, g, i: (b, 0, i)),
                  pl.BlockSpec((TQ, gw), lambda b, g, i: (b * nq + i, gate_blk + g))],
        out_specs=pl.BlockSpec((TQ, gw), lambda b, g, i: (b * nq + i, g)),
        out_shape=jax.ShapeDtypeStruct((b_count * t, INNER), BF16),
        scratch_shapes=_attn_scratch(),
        compiler_params=_cparams("parallel", "parallel", "parallel"),
        name="fox_attn",
    )(qT, k, vT, cum, cumT, z)


def _retention_kernel(q_ref, k_ref, v_ref, gate_ref, cos_ref, sin_ref, lg_ref, gn_ref, o_ref,
                      state_ref, dmask_ref):
    c = RET_CHUNK
    lg = lg_ref[0][:, :1]

    @pl.when(pl.program_id(2) == 0)
    def _():
        state_ref[...] = jnp.zeros_like(state_ref)
        diff = (lax.broadcasted_iota(I32, (c, c), 0) - lax.broadcasted_iota(I32, (c, c), 1)).astype(F32)
        dmask_ref[...] = jnp.where(diff >= 0, jnp.exp(jnp.maximum(diff, 0.0) * lg), 0.0)

    cos = cos_ref[...]
    sin = sin_ref[...]
    half = RET_QK_DIM // 2

    def rope(ref):
        x1 = ref[:, :half]
        x2 = ref[:, half:]
        return jnp.concatenate([x1 * cos - x2 * sin, x1 * sin + x2 * cos], axis=1)

    q = rope(q_ref)
    k = rope(k_ref) * (RET_QK_DIM ** -0.5)
    v = v_ref[...].astype(BF16)
    i_col = lax.broadcasted_iota(I32, (c, 1), 0).astype(F32)
    q_decay = jnp.exp((i_col + 1.0) * lg)
    k_decay = jnp.exp((c - 1.0 - i_col) * lg)
    inner = lax.dot_general(q.astype(BF16), k.astype(BF16), (((1,), (1,)), ((), ())),
                            preferred_element_type=F32) * dmask_ref[...]
    state = state_ref[...]
    o = (jnp.dot(inner.astype(BF16), v, preferred_element_type=F32)
         + jnp.dot((q * q_decay).astype(BF16), state.astype(BF16), preferred_element_type=F32))
    kdT = (k * k_decay).T.astype(BF16)
    state_ref[...] = state * jnp.exp(c * lg) + jnp.dot(kdT, v, preferred_element_type=F32)

    mu = jnp.mean(o, axis=-1, keepdims=True)
    var = jnp.mean(jnp.square(o - mu), axis=-1, keepdims=True)
    on = (o - mu) * lax.rsqrt(var + EPS) * gn_ref[...]
    gate = gate_ref[...]
    o_ref[...] = (gate * jax.nn.sigmoid(gate) * on).astype(BF16)


def _retention(z, b_count, t, gn_g):
    c = RET_CHUNK
    nc = t // c
    half = RET_QK_DIM // 2
    pos = jnp.arange(t, dtype=jnp.int32)
    inv = ROPE_THETA ** (-jnp.arange(0, RET_QK_DIM, 2, dtype=F32) / RET_QK_DIM)
    ang = pos.astype(F32)[:, None] * inv[None, :]
    cos, sin = jnp.cos(ang), jnp.sin(ang)
    log_gamma = jnp.log(1.0 - 2.0 ** (-5.0 - jnp.arange(RET_HEADS, dtype=F32)))
    lg = jnp.broadcast_to(log_gamma[:, None, None], (RET_HEADS, 1, LANE))
    qk_w = RET_HEADS * RET_QK_DIM
    row = lambda b, h, i: b * nc + i
    return pl.pallas_call(
        _retention_kernel,
        grid=(b_count, RET_HEADS, nc),
        in_specs=[pl.BlockSpec((c, RET_QK_DIM), lambda b, h, i: (row(b, h, i), h)),
                  pl.BlockSpec((c, RET_QK_DIM), lambda b, h, i: (row(b, h, i), RET_HEADS + h)),
                  pl.BlockSpec((c, RET_V_DIM), lambda b, h, i: (row(b, h, i), 2 * qk_w // RET_V_DIM + h)),
                  pl.BlockSpec((c, RET_V_DIM),
                               lambda b, h, i: (row(b, h, i), (2 * qk_w + RET_INNER) // RET_V_DIM + h)),
                  pl.BlockSpec((c, half), lambda b, h, i: (i, 0)),
                  pl.BlockSpec((c, half), lambda b, h, i: (i, 0)),
                  pl.BlockSpec((1, 1, LANE), lambda b, h, i: (h, 0, 0)),
                  pl.BlockSpec((1, RET_V_DIM), lambda b, h, i: (0, h))],
        out_specs=pl.BlockSpec((c, RET_V_DIM), lambda b, h, i: (row(b, h, i), h)),
        out_shape=jax.ShapeDtypeStruct((b_count * t, RET_INNER), BF16),
        scratch_shapes=[pltpu.VMEM((RET_QK_DIM, RET_V_DIM), F32), pltpu.VMEM((c, c), F32)],
        compiler_params=_cparams("parallel", "parallel", "arbitrary"),
        name="retention",
    )(z, z, z, z, cos, sin, lg, gn_g.reshape(1, RET_INNER))


def _tail_proj(hn, w_in, n_main, name):
    w_tail = jnp.pad(w_in[:, n_main:], ((0, 0), (0, LANE - (w_in.shape[1] - n_main)))).astype(BF16)
    return _matmul(hn, w_tail, 1024, LANE, name)


def _dsa_layer(h, hn, b_count, t, w_in, q_g, k_g, w_out, next_g):
    n_main = 2 * INNER + 2 * KV_WIDTH + IDX_HEADS * IDX_DIM
    z = _in_proj(hn, w_in, n_main, 1024, 1024, "dsa_in_proj")
    z_tail = _tail_proj(hn, w_in, n_main, "dsa_tail_proj")
    qT, k, vT, qiT, kia, kib, wT = _dsa_prep(z, z_tail, b_count, t, q_g, k_g)
    bias = _dsa_select(kia, kib, qiT, wT)
    gated = _dsa_attn(qT, k, vT, bias, z)
    return _out_proj(gated, w_out.astype(BF16), h, next_g, 512, "dsa_out_proj")


def _moba_layer(h, hn, b_count, t, w_in, q_g, k_g, w_out, next_g):
    z = _in_proj(hn, w_in, w_in.shape[1], 1024, 1024, "moba_in_proj")
    qT, k, vT, kmean = _moba_prep(z, b_count, t, q_g, k_g)
    gated = _moba_attn(qT, k, vT, kmean, z)
    return _out_proj(gated, w_out.astype(BF16), h, next_g, 512, "moba_out_proj")


def _ret_layer(h, hn, b_count, t, w_in, gn_g, w_out, next_g):
    z = _in_proj(hn, w_in, w_in.shape[1], 1024, 1024, "ret_in_proj")
    gated = _retention(z, b_count, t, gn_g)
    return _out_proj(gated, w_out.astype(BF16), h, next_g, 256, "ret_out_proj")


def _fox_layer(h, hn, b_count, t, w_in, f_bias, q_g, k_g, w_out, next_g):
    n_main = 4 * INNER
    z = _in_proj(hn, w_in, n_main, 1024, 1024, "fox_in_proj")
    z_tail = _tail_proj(hn, w_in, n_main, "fox_tail_proj")
    qT, k, vT, cum, cumT = _fox_prep(z, z_tail, b_count, t, f_bias, q_g, k_g)
    gated = _fox_attn(qT, k, vT, cum, cumT, z)
    return _out_proj(gated, w_out.astype(BF16), h, next_g, 512, "fox_out_proj")


def kernel(x, a_norm, a_w_in, a_q_norm, a_k_norm, a_w_out, b_norm, b_w_in, b_q_norm, b_k_norm, b_w_out,
           c_norm, c_w_in, c_gn, c_w_out, d_norm, d_w_in, d_f_bias, d_q_norm, d_k_norm, d_w_out):
    b_count, t, d = x.shape
    assert d == D_MODEL and t % TQ == 0 and t // 4 >= DSA_TOPK
    depth = 4
    norms = (a_norm, b_norm, c_norm, d_norm)
    h = x.reshape(b_count * t, d)
    hn = _rmsnorm(h, a_norm[0])
    for i in range(depth):
        m, j = i % 4, i // 4
        next_g = norms[(i + 1) % 4][(i + 1) // 4] if i + 1 < depth else None
        if m == 0:
            h, hn = _dsa_layer(h, hn, b_count, t, a_w_in[j], a_q_norm[j], a_k_norm[j], a_w_out[j], next_g)
        elif m == 1:
            h, hn = _moba_layer(h, hn, b_count, t, b_w_in[j], b_q_norm[j], b_k_norm[j], b_w_out[j], next_g)
        elif m == 2:
            h, hn = _ret_layer(h, hn, b_count, t, c_w_in[j], c_gn[j], c_w_out[j], next_g)
        else:
            h, hn = _fox_layer(h, hn, b_count, t, d_w_in[j], d_f_bias[j], d_q_norm[j], d_k_norm[j], d_w_out[j],
                               next_g)
    return h.reshape(b_count, t, d)
```

```python
import functools

import jax
import jax.numpy as jnp
import numpy as np
from jax import lax
from jax.experimental import pallas as pl
from jax.experimental.pallas import tpu as pltpu

F32 = jnp.float32
BF16 = jnp.bfloat16
I32 = jnp.int32

D_MODEL = 2048
HEAD_DIM = 128
N_HEADS = 16
N_KV_HEADS = 4
GROUP = N_HEADS // N_KV_HEADS
INNER = N_HEADS * HEAD_DIM
KV_WIDTH = N_KV_HEADS * HEAD_DIM
IDX_HEADS = 16
IDX_DIM = 64
DSA_TOPK = 256
MOBA_BLOCK = 256
MOBA_TOPK = 3
RET_HEADS = 8
RET_QK_DIM = 256
RET_V_DIM = 512
RET_INNER = RET_HEADS * RET_V_DIM
ROPE_THETA = 10000.0
EPS = 1e-6
NEG = -1e30
INT_MIN = -(2 ** 31)
LOG2E = 1.4426950408889634

LANE = 128
TQ = 256
TK = 256
RET_CHUNK = 256
RET_PAIR = 2
VMEM_LIMIT = 56 * 1024 * 1024


def _cparams(*sem):
    return pltpu.CompilerParams(dimension_semantics=sem, vmem_limit_bytes=VMEM_LIMIT)


def _rmsnorm_kernel(x_ref, g_ref, o_ref):
    x = x_ref[...]
    ms = jnp.mean(x * x, axis=-1, keepdims=True)
    o_ref[...] = (x * lax.rsqrt(ms + EPS) * g_ref[...]).astype(o_ref.dtype)


def _rmsnorm(x, g, tm=512):
    m, d = x.shape
    return pl.pallas_call(
        _rmsnorm_kernel,
        grid=(m // tm,),
        in_specs=[pl.BlockSpec((tm, d), lambda i: (i, 0)),
                  pl.BlockSpec((1, d), lambda i: (0, 0))],
        out_specs=pl.BlockSpec((tm, d), lambda i: (i, 0)),
        out_shape=jax.ShapeDtypeStruct((m, d), BF16),
        compiler_params=_cparams("parallel"),
        name="rmsnorm",
    )(x, g.reshape(1, d))


def _mm_kernel(a_ref, w_ref, o_ref):
    o_ref[...] = jnp.dot(a_ref[...], w_ref[...], preferred_element_type=F32).astype(o_ref.dtype)


def _in_proj_kernel(a_ref, w_ref, o_ref, wb_ref):
    @pl.when(pl.program_id(1) == 0)
    def _():
        wb_ref[...] = w_ref[...].astype(BF16)

    o_ref[...] = jnp.dot(a_ref[...], wb_ref[...], preferred_element_type=F32)


def _in_proj(a, w, n, tm, tn, name):
    m, k = a.shape
    return pl.pallas_call(
        _in_proj_kernel,
        grid=(n // tn, m // tm),
        in_specs=[pl.BlockSpec((tm, k), lambda j, i: (i, 0)),
                  pl.BlockSpec((k, tn), lambda j, i: (0, j))],
        out_specs=pl.BlockSpec((tm, tn), lambda j, i: (i, j)),
        out_shape=jax.ShapeDtypeStruct((m, n), F32),
        scratch_shapes=[pltpu.VMEM((k, tn), BF16)],
        compiler_params=_cparams("parallel", "arbitrary"),
        name=name,
    )(a, w)


def _in_proj_wt_kernel(a_ref, wt_ref, o_ref, wb_ref):
    @pl.when(pl.program_id(1) == 0)
    def _():
        wb_ref[...] = wt_ref[...].T.astype(BF16)

    o_ref[...] = jnp.dot(a_ref[...], wb_ref[...], preferred_element_type=F32)


def _in_proj_wt(a, wt, n, tm, tn, name):
    m, k = a.shape
    return pl.pallas_call(
        _in_proj_wt_kernel,
        grid=(n // tn, m // tm),
        in_specs=[pl.BlockSpec((tm, k), lambda j, i: (i, 0)),
                  pl.BlockSpec((tn, k), lambda j, i: (j, 0))],
        out_specs=pl.BlockSpec((tm, tn), lambda j, i: (i, j)),
        out_shape=jax.ShapeDtypeStruct((m, n), F32),
        scratch_shapes=[pltpu.VMEM((k, tn), BF16)],
        compiler_params=_cparams("parallel", "arbitrary"),
        name=name,
    )(a, wt)


def _out_proj_kernel(a_ref, w_ref, r_ref, o_ref):
    o_ref[...] = r_ref[...] + jnp.dot(a_ref[...], w_ref[...], preferred_element_type=F32)


def _out_proj_norm_kernel(a_ref, w_ref, r_ref, g_ref, o_ref, hn_ref):
    h = r_ref[...] + jnp.dot(a_ref[...], w_ref[...], preferred_element_type=F32)
    o_ref[...] = h
    ms = jnp.mean(h * h, axis=-1, keepdims=True)
    hn_ref[...] = (h * lax.rsqrt(ms + EPS) * g_ref[...]).astype(BF16)


def _out_proj(a, w, res, next_g, tm, name):
    m, k = a.shape
    n = w.shape[1]
    in_specs = [pl.BlockSpec((tm, k), lambda i: (i, 0)),
                pl.BlockSpec((k, n), lambda i: (0, 0)),
                pl.BlockSpec((tm, n), lambda i: (i, 0))]
    out_specs = [pl.BlockSpec((tm, n), lambda i: (i, 0))]
    out_shape = [jax.ShapeDtypeStruct((m, n), F32)]
    args = [a, w, res]
    if next_g is not None:
        in_specs.append(pl.BlockSpec((1, n), lambda i: (0, 0)))
        out_specs.append(pl.BlockSpec((tm, n), lambda i: (i, 0)))
        out_shape.append(jax.ShapeDtypeStruct((m, n), BF16))
        args.append(next_g.reshape(1, n))
    out = pl.pallas_call(
        _out_proj_kernel if next_g is None else _out_proj_norm_kernel,
        grid=(m // tm,),
        in_specs=in_specs, out_specs=out_specs, out_shape=out_shape,
        compiler_params=_cparams("parallel"),
        name=name,
    )(*args)
    return (out[0], out[1]) if next_g is not None else (out[0], None)


def _matmul(a, w, tm, tn, name):
    m, k = a.shape
    n = w.shape[1]
    return pl.pallas_call(
        _mm_kernel,
        grid=(m // tm, n // tn),
        in_specs=[pl.BlockSpec((tm, k), lambda i, j: (i, 0)),
                  pl.BlockSpec((k, tn), lambda i, j: (0, j))],
        out_specs=pl.BlockSpec((tm, tn), lambda i, j: (i, j)),
        out_shape=jax.ShapeDtypeStruct((m, n), F32),
        compiler_params=_cparams("parallel", "parallel"),
        name=name,
    )(a, w)


def _rope_tables(t, d, reps):
    pos = jnp.arange(t, dtype=jnp.int32)
    inv = ROPE_THETA ** (-jnp.arange(0, d, 2, dtype=F32) / d)
    ang = pos.astype(F32)[:, None] * inv[None, :]
    cos, sin = jnp.cos(ang), jnp.sin(ang)
    cos_t = jnp.tile(jnp.concatenate([cos, cos], axis=-1), (1, reps))
    sin_t = jnp.tile(jnp.concatenate([-sin, sin], axis=-1), (1, reps))
    return cos_t, sin_t


def _head_norm(x, g):
    ms = jnp.mean(x * x, axis=-1, keepdims=True)
    return x * lax.rsqrt(ms + EPS) * g


def _rope128(y, cos, sin):
    return y * cos + pltpu.roll(y, 64, 1) * sin


def _qkv_prep_body(q_ref, k_ref, v_ref, cos_ref, sin_ref, qg_ref, kg_ref,
                   qT_ref, ko_ref, vT_ref, *, n_kv, rope):
    qg = qg_ref[...]
    kg = kg_ref[...]
    scale = HEAD_DIM ** -0.5 * LOG2E
    if rope:
        cos = cos_ref[...]
        sin = sin_ref[...]
    for h in range(N_HEADS):
        sl = slice(h * HEAD_DIM, (h + 1) * HEAD_DIM)
        y = _head_norm(q_ref[:, sl], qg)
        if rope:
            y = _rope128(y, cos, sin)
        qT_ref[0, sl, :] = (y * scale).T.astype(BF16)
    k_out = []
    for g in range(n_kv):
        sl = slice(g * HEAD_DIM, (g + 1) * HEAD_DIM)
        y = _head_norm(k_ref[:, sl], kg)
        if rope:
            y = _rope128(y, cos, sin)
        if ko_ref.ndim == 4:
            ko_ref[0, g] = y.astype(BF16)
        else:
            ko_ref[0, :, sl] = y.astype(BF16)
        vT_ref[0, sl, :] = v_ref[:, sl].T.astype(BF16)
        k_out.append(y)
    return k_out


def _dsa_prep_kernel(q_ref, k_ref, v_ref, qi_ref, tail_ref, cos_ref, sin_ref, cos64_ref, sin64_ref,
                     qg_ref, kg_ref, qT_ref, ko_ref, vT_ref, qiT_ref, kia_ref, kib_ref, wT_ref):
    _qkv_prep_body(q_ref, k_ref, v_ref, cos_ref, sin_ref, qg_ref, kg_ref, qT_ref, ko_ref, vT_ref,
                   n_kv=N_KV_HEADS, rope=True)
    tm = tail_ref.shape[0]
    lane = lax.broadcasted_iota(I32, (tm, LANE), 1)
    first_half = (lane % IDX_DIM) < (IDX_DIM // 2)
    c64 = cos64_ref[...]
    s64 = sin64_ref[...]

    def rope64(x):
        rot = jnp.where(first_half, pltpu.roll(x, LANE - IDX_DIM // 2, 1), pltpu.roll(x, IDX_DIM // 2, 1))
        return x * c64 + rot * s64

    for j in range(IDX_HEADS * IDX_DIM // LANE):
        sl = slice(j * LANE, (j + 1) * LANE)
        y = rope64(qi_ref[:, sl]) * (IDX_DIM ** -0.5)
        qiT_ref[0, sl, :] = y.T.astype(BF16)
    tail = tail_ref[...]
    ka = jnp.where(lane < IDX_DIM, rope64(tail), 0.0)
    kia_ref[0] = ka.astype(BF16)
    kib_ref[0] = pltpu.roll(ka, IDX_DIM, 1).astype(BF16)
    wT_ref[0] = (tail * (IDX_HEADS ** -0.5)).T


def _moba_prep_kernel(q_ref, k_ref, v_ref, cos_ref, sin_ref, qg_ref, kg_ref,
                      qT_ref, ko_ref, vT_ref, kmean_ref):
    k_out = _qkv_prep_body(q_ref, k_ref, v_ref, cos_ref, sin_ref, qg_ref, kg_ref, qT_ref, ko_ref, vT_ref,
                           n_kv=N_KV_HEADS, rope=True)
    for g in range(N_KV_HEADS):
        km = jnp.mean(k_out[g], axis=0, keepdims=True)
        kmean_ref[0, g, pl.ds(pl.program_id(1), 1), :] = jnp.concatenate([km] * GROUP, axis=1)


def _fox_prep_kernel(q_ref, k_ref, v_ref, f_ref, fb_ref, qg_ref, kg_ref,
                     qT_ref, ko_ref, vT_ref, cum_ref, cumT_ref, carry_ref):
    _qkv_prep_body(q_ref, k_ref, v_ref, None, None, qg_ref, kg_ref, qT_ref, ko_ref, vT_ref,
                   n_kv=N_HEADS, rope=False)

    @pl.when(pl.program_id(1) == 0)
    def _():
        carry_ref[...] = jnp.zeros_like(carry_ref)

    tm = f_ref.shape[0]
    x = f_ref[...] + fb_ref[...]
    lf = jnp.minimum(x, 0.0) - jnp.log(1.0 + jnp.exp(-jnp.abs(x)))
    hi = lf.astype(BF16)
    r1 = lf - hi.astype(F32)
    lo = r1.astype(BF16)
    lo2 = (r1 - lo.astype(F32)).astype(BF16)
    row = lax.broadcasted_iota(I32, (tm, tm), 0)
    col = lax.broadcasted_iota(I32, (tm, tm), 1)
    tri = (col <= row).astype(BF16)
    parts = jnp.dot(tri, jnp.concatenate([hi, lo, lo2], axis=1), preferred_element_type=F32)
    cum = parts[:, :LANE] + parts[:, LANE:2 * LANE] + parts[:, 2 * LANE:] + carry_ref[...]
    carry_ref[...] = cum[tm - 1:tm, :]
    cum2 = cum * LOG2E
    cum_ref[0] = cum2
    cumT_ref[0] = cum2.T


def _prep_specs(b_count, t, tm, z_q_blk, z_k_blk, z_v_blk, kv_width):
    nt = t // tm
    row = lambda b, i: b * nt + i
    in_specs = [
        pl.BlockSpec((tm, INNER), lambda b, i: (row(b, i), z_q_blk)),
        pl.BlockSpec((tm, kv_width), lambda b, i: (row(b, i), z_k_blk)),
        pl.BlockSpec((tm, kv_width), lambda b, i: (row(b, i), z_v_blk)),
    ]
    out_specs = [
        pl.BlockSpec((1, INNER, tm), lambda b, i: (b, 0, i)),
        pl.BlockSpec((1, tm, kv_width), lambda b, i: (b, i, 0)),
        pl.BlockSpec((1, kv_width, tm), lambda b, i: (b, 0, i)),
    ]
    out_shape = [
        jax.ShapeDtypeStruct((b_count, INNER, t), BF16),
        jax.ShapeDtypeStruct((b_count, t, kv_width), BF16),
        jax.ShapeDtypeStruct((b_count, kv_width, t), BF16),
    ]
    return row, in_specs, out_specs, out_shape


def _dsa_prep(z, z_tail, b_count, t, q_g, k_g, tm=256):
    row, in_specs, out_specs, out_shape = _prep_specs(b_count, t, tm, 0, INNER // KV_WIDTH,
                                                      INNER // KV_WIDTH + 1, KV_WIDTH)
    qi_w = IDX_HEADS * IDX_DIM
    qi_off = 2 * INNER + 2 * KV_WIDTH
    cos, sin = _rope_tables(t, HEAD_DIM, 1)
    cos64, sin64 = _rope_tables(t, IDX_DIM, LANE // IDX_DIM)
    tab = pl.BlockSpec((tm, LANE), lambda b, i: (i, 0))
    gain = pl.BlockSpec((1, HEAD_DIM), lambda b, i: (0, 0))
    in_specs += [
        pl.BlockSpec((tm, qi_w), lambda b, i: (row(b, i), qi_off // qi_w)),
        pl.BlockSpec((tm, LANE), lambda b, i: (row(b, i), 0)),
        tab, tab, tab, tab, gain, gain,
    ]
    out_specs += [
        pl.BlockSpec((1, qi_w, tm), lambda b, i: (b, 0, i)),
        pl.BlockSpec((1, tm, LANE), lambda b, i: (b, i, 0)),
        pl.BlockSpec((1, tm, LANE), lambda b, i: (b, i, 0)),
        pl.BlockSpec((1, LANE, tm), lambda b, i: (b, 0, i)),
    ]
    out_shape += [
        jax.ShapeDtypeStruct((b_count, qi_w, t), BF16),
        jax.ShapeDtypeStruct((b_count, t, LANE), BF16),
        jax.ShapeDtypeStruct((b_count, t, LANE), BF16),
        jax.ShapeDtypeStruct((b_count, LANE, t), F32),
    ]
    return pl.pallas_call(
        _dsa_prep_kernel,
        grid=(b_count, t // tm),
        in_specs=in_specs, out_specs=out_specs, out_shape=out_shape,
        compiler_params=_cparams("parallel", "parallel"),
        name="dsa_prep",
    )(z, z, z, z, z_tail, cos, sin, cos64, sin64, q_g.reshape(1, HEAD_DIM), k_g.reshape(1, HEAD_DIM))


def _moba_prep(z, b_count, t, q_g, k_g):
    tm = MOBA_BLOCK
    row, in_specs, out_specs, out_shape = _prep_specs(b_count, t, tm, 0, INNER // KV_WIDTH,
                                                      INNER // KV_WIDTH + 1, KV_WIDTH)
    cos, sin = _rope_tables(t, HEAD_DIM, 1)
    tab = pl.BlockSpec((tm, LANE), lambda b, i: (i, 0))
    gain = pl.BlockSpec((1, HEAD_DIM), lambda b, i: (0, 0))
    in_specs += [tab, tab, gain, gain]
    out_specs += [pl.BlockSpec((1, N_KV_HEADS, t // tm, GROUP * HEAD_DIM), lambda b, i: (b, 0, 0, 0))]
    out_shape += [jax.ShapeDtypeStruct((b_count, N_KV_HEADS, t // tm, GROUP * HEAD_DIM), F32)]
    return pl.pallas_call(
        _moba_prep_kernel,
        grid=(b_count, t // tm),
        in_specs=in_specs, out_specs=out_specs, out_shape=out_shape,
        compiler_params=_cparams("parallel", "arbitrary"),
        name="moba_prep",
    )(z, z, z, cos, sin, q_g.reshape(1, HEAD_DIM), k_g.reshape(1, HEAD_DIM))


def _fox_prep(z, z_tail, b_count, t, f_bias, q_g, k_g, tm=256):
    row, in_specs, out_specs, out_shape = _prep_specs(b_count, t, tm, 0, 1, 2, INNER)
    out_specs[1] = pl.BlockSpec((1, N_HEADS, tm, HEAD_DIM), lambda b, i: (b, 0, i, 0))
    out_shape[1] = jax.ShapeDtypeStruct((b_count, N_HEADS, t, HEAD_DIM), BF16)
    gain = pl.BlockSpec((1, HEAD_DIM), lambda b, i: (0, 0))
    fb = jnp.pad(f_bias.reshape(1, N_HEADS), ((0, 0), (0, LANE - N_HEADS)))
    in_specs += [
        pl.BlockSpec((tm, LANE), lambda b, i: (row(b, i), 0)),
        pl.BlockSpec((1, LANE), lambda b, i: (0, 0)),
        gain, gain,
    ]
    out_specs += [
        pl.BlockSpec((1, tm, LANE), lambda b, i: (b, i, 0)),
        pl.BlockSpec((1, LANE, tm), lambda b, i: (b, 0, i)),
    ]
    out_shape += [
        jax.ShapeDtypeStruct((b_count, t, LANE), F32),
        jax.ShapeDtypeStruct((b_count, LANE, t), F32),
    ]
    return pl.pallas_call(
        _fox_prep_kernel,
        grid=(b_count, t // tm),
        in_specs=in_specs, out_specs=out_specs, out_shape=out_shape,
        scratch_shapes=[pltpu.VMEM((1, LANE), F32)],
        compiler_params=_cparams("parallel", "arbitrary"),
        name="fox_prep",
    )(z, z, z, z_tail, fb, q_g.reshape(1, HEAD_DIM), k_g.reshape(1, HEAD_DIM))


def _dsa_select_kernel(kia_ref, kib_ref, qiT_ref, wT_ref, bias_ref, key_ref, ans_ref):
    qb = pl.program_id(1)
    n_kt = key_ref.shape[0] // TK
    w = wT_ref[0]
    s_iota = lax.broadcasted_iota(I32, (TK, TQ), 0)
    t_iota = lax.broadcasted_iota(I32, (TK, TQ), 1)

    for kt in range(n_kt):
        rows = slice(kt * TK, (kt + 1) * TK)

        @pl.when(kt <= qb)
        def _():
            ka = kia_ref[0, rows, :]
            kb = kib_ref[0, rows, :]
            score = jnp.zeros((TK, TQ), F32)
            for j in range(IDX_HEADS // 2):
                qp = qiT_ref[0, j * LANE:(j + 1) * LANE, :]
                ra = jnp.maximum(jnp.dot(ka, qp, preferred_element_type=F32), 0.0)
                rb = jnp.maximum(jnp.dot(kb, qp, preferred_element_type=F32), 0.0)
                wa = w[IDX_DIM + 2 * j:IDX_DIM + 2 * j + 1, :]
                wb = w[IDX_DIM + 2 * j + 1:IDX_DIM + 2 * j + 2, :]
                score = score + (ra * wa + rb * wb)
            bits = lax.bitcast_convert_type(score + 0.0, I32)
            key = bits ^ ((bits >> 31) & 0x7FFFFFFF)
            causal = (kt * TK + s_iota) <= (qb * TQ + t_iota)
            key_ref[rows, :] = jnp.where(causal, key, INT_MIN)

        @pl.when(kt > qb)
        def _():
            key_ref[rows, :] = jnp.full((TK, TQ), INT_MIN, I32)

    def variant(v):
        n = (v + 1) * TK
        kf = float(DSA_TOPK)

        def count_ge(cand):
            return jnp.sum((key_ref[:n, :] >= cand).astype(F32), axis=0, keepdims=True)

        def bit_step(i, ans):
            cand = ans | (jnp.int32(1) << (30 - i))
            return jnp.where(count_ge(cand) >= kf, cand, ans)

        sign = jnp.where(count_ge(jnp.zeros((1, TQ), I32)) >= kf, 0, INT_MIN).astype(I32)
        ans_ref[...] = lax.fori_loop(0, 31, bit_step, sign)

    _for_each_query_tile(qb, n_kt, variant)
    ans = ans_ref[...]

    for kt in range(n_kt):
        rows = slice(kt * TK, (kt + 1) * TK)
        k = key_ref[rows, :]
        sel = (k >= ans) & (k > INT_MIN)
        bias_ref[0, rows, :] = jnp.where(sel, 0.0, NEG).astype(BF16)


def _dsa_select(kia, kib, qiT, wT):
    b_count, t, _ = kia.shape
    return pl.pallas_call(
        _dsa_select_kernel,
        grid=(b_count, t // TQ),
        in_specs=[pl.BlockSpec((1, t, LANE), lambda b, i: (b, 0, 0)),
                  pl.BlockSpec((1, t, LANE), lambda b, i: (b, 0, 0)),
                  pl.BlockSpec((1, IDX_HEADS * IDX_DIM, TQ), lambda b, i: (b, 0, i)),
                  pl.BlockSpec((1, LANE, TQ), lambda b, i: (b, 0, i))],
        out_specs=pl.BlockSpec((1, t, TQ), lambda b, i: (b, 0, i)),
        out_shape=jax.ShapeDtypeStruct((b_count, t, t), BF16),
        scratch_shapes=[pltpu.VMEM((t, TQ), I32), pltpu.VMEM((1, TQ), I32)],
        compiler_params=_cparams("parallel", "parallel"),
        name="dsa_select",
    )(kia, kib, qiT, wT)


def _causal_tile():
    s_iota = lax.broadcasted_iota(I32, (TK, TQ), 0)
    t_iota = lax.broadcasted_iota(I32, (TK, TQ), 1)
    return s_iota <= t_iota


def _softmax_attend(q, k_all, vt_all, bias_fn, n_tiles, causal_last):
    return _softmax_attend_heads([q], [k_all], [vt_all], [bias_fn], n_tiles, causal_last)[0]


def _softmax_attend_heads(qs, k_alls, vt_alls, bias_fns, n_tiles, causal_last):
    heads = range(len(qs))
    s_alls = [jnp.dot(k_alls[i], qs[i], preferred_element_type=F32) for i in heads]
    tiles = [[] for _ in heads]
    for i in heads:
        for j in range(n_tiles):
            sj = bias_fns[i](j, s_alls[i][j * TK:(j + 1) * TK])
            if causal_last and j == n_tiles - 1:
                sj = jnp.where(_causal_tile(), sj, NEG)
            tiles[i].append(sj)
    ms = [functools.reduce(jnp.maximum, [jnp.max(sj, axis=0, keepdims=True) for sj in tiles[i]]) for i in heads]
    outs = []
    for i in heads:
        ps = [jnp.exp2(sj - ms[i]) for sj in tiles[i]]
        l = functools.reduce(jnp.add, [jnp.sum(p, axis=0, keepdims=True) for p in ps])
        ps = [p.astype(BF16) for p in ps]
        p_all = ps[0] if n_tiles == 1 else jnp.concatenate(ps, axis=0)
        outs.append(jnp.dot(vt_alls[i], p_all, preferred_element_type=F32) * (1.0 / l))
    return outs


def _for_each_query_tile(qb, n_q, body):
    for v in range(n_q):
        pl.when(qb == v)(functools.partial(body, v))


def _pair_rows(i):
    base = pl.multiple_of(i * 2 * HEAD_DIM, 2 * HEAD_DIM)
    return [pl.ds(base, HEAD_DIM), pl.ds(base + HEAD_DIM, HEAD_DIM)]

def _gate_and_store(oT_ref, gate_ref, o_ref):
    for r in range(GROUP):
        sl = slice(r * HEAD_DIM, (r + 1) * HEAD_DIM)
        gate = gate_ref[:, sl]
        o_ref[:, sl] = (gate * jax.nn.sigmoid(gate) * oT_ref[sl, :].T).astype(BF16)


def _dsa_attn_kernel(qT_ref, k_ref, vT_ref, bias_ref, gate_ref, o_ref, oT_ref):
    def variant(v):
        n = (v + 1) * TK

        def bias_fn(j, s):
            return s + bias_ref[0, j * TK:(j + 1) * TK, :].astype(F32)

        def head_pair(i, carry):
            rows = _pair_rows(i)
            outs = _softmax_attend_heads([qT_ref[0, r, :] for r in rows], [k_ref[0, :n, :]] * 2,
                                         [vT_ref[0, :, :n]] * 2, [bias_fn] * 2, v + 1, False)
            for r, o in zip(rows, outs):
                oT_ref[r, :] = o
            return carry

        lax.fori_loop(0, GROUP // 2, head_pair, 0)

    _for_each_query_tile(pl.program_id(2), k_ref.shape[1] // TQ, variant)
    _gate_and_store(oT_ref, gate_ref, o_ref)


def _moba_attn_kernel(qT_ref, k_ref, vT_ref, kmean_ref, gate_ref, o_ref, oT_ref, selb_ref):
    own = pl.program_id(2)
    n_blocks = kmean_ref.shape[2]

    km = kmean_ref[0, 0].astype(BF16)
    gs = jnp.dot(km, qT_ref[0], preferred_element_type=F32)
    blk = lax.broadcasted_iota(I32, (n_blocks, TQ), 0)
    past = blk < own
    gs = jnp.where(past, gs, NEG)
    sel = jnp.zeros((n_blocks, TQ), jnp.bool_)
    for _ in range(min(MOBA_TOPK, n_blocks - 1)):
        mx = jnp.max(gs, axis=0, keepdims=True)
        first = jnp.min(jnp.where(gs == mx, blk, n_blocks), axis=0, keepdims=True)
        pick = blk == first
        sel = sel | pick
        gs = jnp.where(pick, -jnp.inf, gs)
    selb_ref[...] = jnp.where(sel & past, 0.0, NEG)

    def variant(v):
        n = (v + 1) * TK

        def bias_fn(j, s):
            return s if j == v else s + selb_ref[j:j + 1, :]

        def head_pair(i, carry):
            rows = _pair_rows(i)
            outs = _softmax_attend_heads([qT_ref[0, r, :] for r in rows], [k_ref[0, :n, :]] * 2,
                                         [vT_ref[0, :, :n]] * 2, [bias_fn] * 2, v + 1, True)
            for r, o in zip(rows, outs):
                oT_ref[r, :] = o
            return carry

        lax.fori_loop(0, GROUP // 2, head_pair, 0)

    _for_each_query_tile(own, n_blocks, variant)
    _gate_and_store(oT_ref, gate_ref, o_ref)


def _fox_attn_kernel(qT_ref, k_ref, vT_ref, cum_ref, cumT_ref, gate_ref, o_ref, oT_ref):
    g = pl.program_id(1)

    def variant(v):
        n = (v + 1) * TK

        def make_bias_fn(h):
            cq = cumT_ref[0, pl.ds(h, 1), :]
            onehot = (lax.broadcasted_iota(I32, (TK, LANE), 1) == h).astype(F32)

            def bias_fn(j, s):
                ck = jnp.sum(cum_ref[0, j * TK:(j + 1) * TK, :] * onehot, axis=1, keepdims=True)
                return s + (cq - ck)

            return bias_fn

        def head_pair(i, carry):
            rows = _pair_rows(i)
            heads = [2 * i, 2 * i + 1]
            outs = _softmax_attend_heads([qT_ref[0, r, :] for r in rows], [k_ref[0, r, :n, :] for r in heads],
                                         [vT_ref[0, r, :n] for r in rows],
                                         [make_bias_fn(g * GROUP + r) for r in heads], v + 1, True)
            for r, o in zip(rows, outs):
                oT_ref[r, :] = o
            return carry

        lax.fori_loop(0, GROUP // 2, head_pair, 0)

    _for_each_query_tile(pl.program_id(2), k_ref.shape[2] // TQ, variant)
    _gate_and_store(oT_ref, gate_ref, o_ref)


def _attn_scratch():
    return [pltpu.VMEM((GROUP * HEAD_DIM, TQ), F32)]


def _gqa_specs(t, gate_blk):
    nq = t // TQ
    gw = GROUP * HEAD_DIM
    in_specs = [
        pl.BlockSpec((1, gw, TQ), lambda b, g, i: (b, g, i)),
        pl.BlockSpec((1, t, HEAD_DIM), lambda b, g, i: (b, 0, g)),
        pl.BlockSpec((1, HEAD_DIM, t), lambda b, g, i: (b, g, 0)),
    ]
    gate_spec = pl.BlockSpec((TQ, gw), lambda b, g, i: (b * nq + i, gate_blk + g))
    out_spec = pl.BlockSpec((TQ, gw), lambda b, g, i: (b * nq + i, g))
    return nq, in_specs, gate_spec, out_spec


def _dsa_attn(qT, k, vT, bias, z):
    b_count, t, _ = k.shape
    gate_blk = (INNER + 2 * KV_WIDTH) // (GROUP * HEAD_DIM)
    nq, in_specs, gate_spec, out_spec = _gqa_specs(t, gate_blk)
    in_specs += [pl.BlockSpec((1, t, TQ), lambda b, g, i: (b, 0, i)), gate_spec]
    return pl.pallas_call(
        _dsa_attn_kernel,
        grid=(b_count, N_KV_HEADS, nq),
        in_specs=in_specs, out_specs=out_spec,
        out_shape=jax.ShapeDtypeStruct((b_count * t, INNER), BF16),
        scratch_shapes=_attn_scratch(),
        compiler_params=_cparams("parallel", "parallel", "parallel"),
        name="dsa_attn",
    )(qT, k, vT, bias, z)


def _moba_attn(qT, k, vT, kmean, z):
    b_count, t, _ = k.shape
    gate_blk = (INNER + 2 * KV_WIDTH) // (GROUP * HEAD_DIM)
    nq, in_specs, gate_spec, out_spec = _gqa_specs(t, gate_blk)
    n_blocks = t // MOBA_BLOCK
    in_specs += [pl.BlockSpec((1, 1, n_blocks, GROUP * HEAD_DIM), lambda b, g, i: (b, g, 0, 0)), gate_spec]
    return pl.pallas_call(
        _moba_attn_kernel,
        grid=(b_count, N_KV_HEADS, nq),
        in_specs=in_specs, out_specs=out_spec,
        out_shape=jax.ShapeDtypeStruct((b_count * t, INNER), BF16),
        scratch_shapes=_attn_scratch() + [pltpu.VMEM((n_blocks, TQ), F32)],
        compiler_params=_cparams("parallel", "parallel", "parallel"),
        name="moba_attn",
    )(qT, k, vT, kmean, z)


def _fox_attn(qT, k, vT, cum, cumT, z):
    b_count, _, t, _ = k.shape
    nq = t // TQ
    gw = GROUP * HEAD_DIM
    gate_blk = 3 * INNER // gw
    return pl.pallas_call(
        _fox_attn_kernel,
        grid=(b_count, N_HEADS // GROUP, nq),
        in_specs=[pl.BlockSpec((1, gw, TQ), lambda b, g, i: (b, g, i)),
                  pl.BlockSpec((1, GROUP, t, HEAD_DIM), lambda b, g, i: (b, g, 0, 0)),
                  pl.BlockSpec((1, gw, t), lambda b, g, i: (b, g, 0)),
                  pl.BlockSpec((1, t, LANE), lambda b, g, i: (b, 0, 0)),
                  pl.BlockSpec((1, LANE, TQ), lambda b, g, i: (b, 0, i)),
                  pl.BlockSpec((TQ, gw), lambda b, g, i: (b * nq + i, gate_blk + g))],
        out_specs=pl.BlockSpec((TQ, gw), lambda b, g, i: (b * nq + i, g)),
        out_shape=jax.ShapeDtypeStruct((b_count * t, INNER), BF16),
        scratch_shapes=_attn_scratch(),
        compiler_params=_cparams("parallel", "parallel", "parallel"),
        name="fox_attn",
    )(qT, k, vT, cum, cumT, z)


def _retention_kernel(q_ref, k_ref, v_ref, gate_ref, cos_ref, sin_ref, lg_ref, gn_ref, o_ref,
                      state_ref, dmask_ref):
    c = RET_CHUNK
    heads = range(RET_PAIR)
    lgs = [lg_ref[h][:, :1] for h in heads]

    @pl.when(pl.program_id(2) == 0)
    def _():
        state_ref[...] = jnp.zeros_like(state_ref)
        diff = (lax.broadcasted_iota(I32, (c, c), 0) - lax.broadcasted_iota(I32, (c, c), 1)).astype(F32)
        for h in heads:
            dmask_ref[h] = jnp.where(diff >= 0, jnp.exp(jnp.maximum(diff, 0.0) * lgs[h]), 0.0)

    cos = cos_ref[...]
    sin = sin_ref[...]
    half = RET_QK_DIM // 2

    def rope(ref, h):
        x1 = ref[:, h * RET_QK_DIM:h * RET_QK_DIM + half]
        x2 = ref[:, h * RET_QK_DIM + half:(h + 1) * RET_QK_DIM]
        return jnp.concatenate([x1 * cos - x2 * sin, x1 * sin + x2 * cos], axis=1)

    i_col = lax.broadcasted_iota(I32, (c, 1), 0).astype(F32)
    vsl = [slice(h * RET_V_DIM, (h + 1) * RET_V_DIM) for h in heads]
    qs = [rope(q_ref, h) for h in heads]
    ks = [rope(k_ref, h) * (RET_QK_DIM ** -0.5) for h in heads]
    inners = [lax.dot_general(qs[h].astype(BF16), ks[h].astype(BF16), (((1,), (1,)), ((), ())),
                              preferred_element_type=F32) for h in heads]
    vs = [v_ref[:, vsl[h]].astype(BF16) for h in heads]
    cross = [jnp.dot((qs[h] * jnp.exp((i_col + 1.0) * lgs[h])).astype(BF16), state_ref[h].astype(BF16),
                     preferred_element_type=F32) for h in heads]
    kdTs = [(ks[h] * jnp.exp((c - 1.0 - i_col) * lgs[h])).T.astype(BF16) for h in heads]
    os_ = [jnp.dot((inners[h] * dmask_ref[h]).astype(BF16), vs[h], preferred_element_type=F32) + cross[h]
           for h in heads]
    for h in heads:
        state_ref[h] = state_ref[h] * jnp.exp(c * lgs[h]) + jnp.dot(kdTs[h], vs[h], preferred_element_type=F32)
    for h in heads:
        o = os_[h]
        mu = jnp.mean(o, axis=-1, keepdims=True)
        var = jnp.mean(jnp.square(o - mu), axis=-1, keepdims=True)
        on = (o - mu) * lax.rsqrt(var + EPS) * gn_ref[:, vsl[h]]
        gate = gate_ref[:, vsl[h]]
        o_ref[:, vsl[h]] = (gate * jax.nn.sigmoid(gate) * on).astype(BF16)


def _retention(z, b_count, t, gn_g):
    c = RET_CHUNK
    nc = t // c
    half = RET_QK_DIM // 2
    pos = jnp.arange(t, dtype=jnp.int32)
    inv = ROPE_THETA ** (-jnp.arange(0, RET_QK_DIM, 2, dtype=F32) / RET_QK_DIM)
    ang = pos.astype(F32)[:, None] * inv[None, :]
    cos, sin = jnp.cos(ang), jnp.sin(ang)
    log_gamma = jnp.log(1.0 - 2.0 ** (-5.0 - jnp.arange(RET_HEADS, dtype=F32)))
    lg = jnp.broadcast_to(log_gamma[:, None, None], (RET_HEADS, 1, LANE))
    qk_w = RET_HEADS * RET_QK_DIM
    qk_blk = RET_PAIR * RET_QK_DIM
    v_blk = RET_PAIR * RET_V_DIM
    row = lambda b, h, i: b * nc + i
    return pl.pallas_call(
        _retention_kernel,
        grid=(b_count, RET_HEADS // RET_PAIR, nc),
        in_specs=[pl.BlockSpec((c, qk_blk), lambda b, h, i: (row(b, h, i), h)),
                  pl.BlockSpec((c, qk_blk), lambda b, h, i: (row(b, h, i), qk_w // qk_blk + h)),
                  pl.BlockSpec((c, v_blk), lambda b, h, i: (row(b, h, i), 2 * qk_w // v_blk + h)),
                  pl.BlockSpec((c, v_blk), lambda b, h, i: (row(b, h, i), (2 * qk_w + RET_INNER) // v_blk + h)),
                  pl.BlockSpec((c, half), lambda b, h, i: (i, 0)),
                  pl.BlockSpec((c, half), lambda b, h, i: (i, 0)),
                  pl.BlockSpec((RET_PAIR, 1, LANE), lambda b, h, i: (h, 0, 0)),
                  pl.BlockSpec((1, v_blk), lambda b, h, i: (0, h))],
        out_specs=pl.BlockSpec((c, v_blk), lambda b, h, i: (row(b, h, i), h)),
        out_shape=jax.ShapeDtypeStruct((b_count * t, RET_INNER), BF16),
        scratch_shapes=[pltpu.VMEM((RET_PAIR, RET_QK_DIM, RET_V_DIM), F32), pltpu.VMEM((RET_PAIR, c, c), F32)],
        compiler_params=_cparams("parallel", "parallel", "arbitrary"),
        name="retention",
    )(z, z, z, z, cos, sin, lg, gn_g.reshape(1, RET_INNER))


def _tail_proj(hn, w_in, n_main, name):
    w_tail = jnp.pad(w_in[:, n_main:], ((0, 0), (0, LANE - (w_in.shape[1] - n_main)))).astype(BF16)
    return _matmul(hn, w_tail, 1024, LANE, name)


def _dsa_layer(h, hn, b_count, t, w_in, q_g, k_g, w_out, next_g):
    n_main = 2 * INNER + 2 * KV_WIDTH + IDX_HEADS * IDX_DIM
    z = _in_proj_wt(hn, w_in.T, n_main, 1024, 1024, "dsa_in_proj")
    z_tail = _tail_proj(hn, w_in, n_main, "dsa_tail_proj")
    qT, k, vT, qiT, kia, kib, wT = _dsa_prep(z, z_tail, b_count, t, q_g, k_g)
    bias = _dsa_select(kia, kib, qiT, wT)
    gated = _dsa_attn(qT, k, vT, bias, z)
    return _out_proj(gated, w_out.astype(BF16), h, next_g, 512, "dsa_out_proj")


def _moba_layer(h, hn, b_count, t, w_in, q_g, k_g, w_out, next_g):
    z = _in_proj(hn, w_in, w_in.shape[1], 1024, 1024, "moba_in_proj")
    qT, k, vT, kmean = _moba_prep(z, b_count, t, q_g, k_g)
    gated = _moba_attn(qT, k, vT, kmean, z)
    return _out_proj(gated, w_out.astype(BF16), h, next_g, 512, "moba_out_proj")


def _ret_layer(h, hn, b_count, t, w_in, gn_g, w_out, next_g):
    z = _in_proj(hn, w_in, w_in.shape[1], 1024, 1024, "ret_in_proj")
    gated = _retention(z, b_count, t, gn_g)
    return _out_proj(gated, w_out.astype(BF16), h, next_g, 256, "ret_out_proj")


def _fox_layer(h, hn, b_count, t, w_in, f_bias, q_g, k_g, w_out, next_g):
    n_main = 4 * INNER
    z = _in_proj_wt(hn, w_in.T, n_main, 1024, 1024, "fox_in_proj")
    z_tail = _tail_proj(hn, w_in, n_main, "fox_tail_proj")
    qT, k, vT, cum, cumT = _fox_prep(z, z_tail, b_count, t, f_bias, q_g, k_g)
    gated = _fox_attn(qT, k, vT, cum, cumT, z)
    return _out_proj(gated, w_out.astype(BF16), h, next_g, 512, "fox_out_proj")


def kernel(x, a_norm, a_w_in, a_q_norm, a_k_norm, a_w_out, b_norm, b_w_in, b_q_norm, b_k_norm, b_w_out,
           c_norm, c_w_in, c_gn, c_w_out, d_norm, d_w_in, d_f_bias, d_q_norm, d_k_norm, d_w_out):
    b_count, t, d = x.shape
    assert d == D_MODEL and t % TQ == 0 and t // 4 >= DSA_TOPK
    depth = 4
    norms = (a_norm, b_norm, c_norm, d_norm)
    h = x.reshape(b_count * t, d)
    hn = _rmsnorm(h, a_norm[0])
    for i in range(depth):
        m, j = i % 4, i // 4
        next_g = norms[(i + 1) % 4][(i + 1) // 4] if i + 1 < depth else None
        if m == 0:
            h, hn = _dsa_layer(h, hn, b_count, t, a_w_in[j], a_q_norm[j], a_k_norm[j], a_w_out[j], next_g)
        elif m == 1:
            h, hn = _moba_layer(h, hn, b_count, t, b_w_in[j], b_q_norm[j], b_k_norm[j], b_w_out[j], next_g)
        elif m == 2:
            h, hn = _ret_layer(h, hn, b_count, t, c_w_in[j], c_gn[j], c_w_out[j], next_g)
        else:
            h, hn = _fox_layer(h, hn, b_count, t, d_w_in[j], d_f_bias[j], d_q_norm[j], d_k_norm[j], d_w_out[j],
                               next_g)
    return h.reshape(b_count, t, d)
```

```python
import functools

import jax
import jax.numpy as jnp
import numpy as np
from jax import lax
from jax.experimental import pallas as pl
from jax.experimental.pallas import tpu as pltpu

F32 = jnp.float32
BF16 = jnp.bfloat16
I32 = jnp.int32

D_MODEL = 2048
HEAD_DIM = 128
N_HEADS = 16
N_KV_HEADS = 4
GROUP = N_HEADS // N_KV_HEADS
INNER = N_HEADS * HEAD_DIM
KV_WIDTH = N_KV_HEADS * HEAD_DIM
IDX_HEADS = 16
IDX_DIM = 64
DSA_TOPK = 256
MOBA_BLOCK = 256
MOBA_TOPK = 3
RET_HEADS = 8
RET_QK_DIM = 256
RET_V_DIM = 512
RET_INNER = RET_HEADS * RET_V_DIM
ROPE_THETA = 10000.0
EPS = 1e-6
NEG = -1e30
INT_MIN = -(2 ** 31)
LOG2E = 1.4426950408889634

LANE = 128
TQ = 256
TK = 256
RET_CHUNK = 256
RET_PAIR = 2
VMEM_LIMIT = 56 * 1024 * 1024


def _cparams(*sem):
    return pltpu.CompilerParams(dimension_semantics=sem, vmem_limit_bytes=VMEM_LIMIT)


def _rmsnorm_kernel(x_ref, g_ref, o_ref):
    x = x_ref[...]
    ms = jnp.mean(x * x, axis=-1, keepdims=True)
    o_ref[...] = (x * lax.rsqrt(ms + EPS) * g_ref[...]).astype(o_ref.dtype)


def _rmsnorm(x, g, tm=512):
    m, d = x.shape
    return pl.pallas_call(
        _rmsnorm_kernel,
        grid=(m // tm,),
        in_specs=[pl.BlockSpec((tm, d), lambda i: (i, 0)),
                  pl.BlockSpec((1, d), lambda i: (0, 0))],
        out_specs=pl.BlockSpec((tm, d), lambda i: (i, 0)),
        out_shape=jax.ShapeDtypeStruct((m, d), BF16),
        compiler_params=_cparams("parallel"),
        name="rmsnorm",
    )(x, g.reshape(1, d))


def _in_proj_kernel(a_ref, w_ref, o_ref, wb_ref):
    @pl.when(pl.program_id(1) == 0)
    def _():
        wb_ref[...] = w_ref[...].astype(BF16)

    o_ref[...] = jnp.dot(a_ref[...], wb_ref[...], preferred_element_type=F32)


def _in_proj(a, w, n, tm, tn, name):
    m, k = a.shape
    return pl.pallas_call(
        _in_proj_kernel,
        grid=(n // tn, m // tm),
        in_specs=[pl.BlockSpec((tm, k), lambda j, i: (i, 0)),
                  pl.BlockSpec((k, tn), lambda j, i: (0, j))],
        out_specs=pl.BlockSpec((tm, tn), lambda j, i: (i, j)),
        out_shape=jax.ShapeDtypeStruct((m, n), F32),
        scratch_shapes=[pltpu.VMEM((k, tn), BF16)],
        compiler_params=_cparams("parallel", "arbitrary"),
        name=name,
    )(a, w)


def _in_proj_wt_kernel(a_ref, wt_ref, o_ref, wb_ref):
    @pl.when(pl.program_id(1) == 0)
    def _():
        wb_ref[...] = wt_ref[...].T.astype(BF16)

    o_ref[...] = jnp.dot(a_ref[...], wb_ref[...], preferred_element_type=F32)


def _in_proj_wt(a, wt, n, tm, tn, name):
    m, k = a.shape
    return pl.pallas_call(
        _in_proj_wt_kernel,
        grid=(n // tn, m // tm),
        in_specs=[pl.BlockSpec((tm, k), lambda j, i: (i, 0)),
                  pl.BlockSpec((tn, k), lambda j, i: (j, 0))],
        out_specs=pl.BlockSpec((tm, tn), lambda j, i: (i, j)),
        out_shape=jax.ShapeDtypeStruct((m, n), F32),
        scratch_shapes=[pltpu.VMEM((k, tn), BF16)],
        compiler_params=_cparams("parallel", "arbitrary"),
        name=name,
    )(a, wt)


def _out_proj_kernel(a_ref, w_ref, r_ref, o_ref):
    o_ref[...] = r_ref[...] + jnp.dot(a_ref[...], w_ref[...], preferred_element_type=F32)


def _out_proj_norm_kernel(a_ref, w_ref, r_ref, g_ref, o_ref, hn_ref):
    h = r_ref[...] + jnp.dot(a_ref[...], w_ref[...], preferred_element_type=F32)
    o_ref[...] = h
    ms = jnp.mean(h * h, axis=-1, keepdims=True)
    hn_ref[...] = (h * lax.rsqrt(ms + EPS) * g_ref[...]).astype(BF16)


def _out_proj(a, w, res, next_g, tm, name):
    m, k = a.shape
    n = w.shape[1]
    in_specs = [pl.BlockSpec((tm, k), lambda i: (i, 0)),
                pl.BlockSpec((k, n), lambda i: (0, 0)),
                pl.BlockSpec((tm, n), lambda i: (i, 0))]
    out_specs = [pl.BlockSpec((tm, n), lambda i: (i, 0))]
    out_shape = [jax.ShapeDtypeStruct((m, n), F32)]
    args = [a, w, res]
    if next_g is not None:
        in_specs.append(pl.BlockSpec((1, n), lambda i: (0, 0)))
        out_specs.append(pl.BlockSpec((tm, n), lambda i: (i, 0)))
        out_shape.append(jax.ShapeDtypeStruct((m, n), BF16))
        args.append(next_g.reshape(1, n))
    out = pl.pallas_call(
        _out_proj_kernel if next_g is None else _out_proj_norm_kernel,
        grid=(m // tm,),
        in_specs=in_specs, out_specs=out_specs, out_shape=out_shape,
        compiler_params=_cparams("parallel"),
        name=name,
    )(*args)
    return (out[0], out[1]) if next_g is not None else (out[0], None)


def _rope_tables(t, d, reps):
    pos = jnp.arange(t, dtype=jnp.int32)
    inv = ROPE_THETA ** (-jnp.arange(0, d, 2, dtype=F32) / d)
    ang = pos.astype(F32)[:, None] * inv[None, :]
    cos, sin = jnp.cos(ang), jnp.sin(ang)
    cos_t = jnp.tile(jnp.concatenate([cos, cos], axis=-1), (1, reps))
    sin_t = jnp.tile(jnp.concatenate([-sin, sin], axis=-1), (1, reps))
    return cos_t, sin_t


def _rope_tables_t(t, d):
    pos = jnp.arange(t, dtype=jnp.int32)
    inv = ROPE_THETA ** (-jnp.arange(0, d, 2, dtype=F32) / d)
    ang = pos.astype(F32)[:, None] * inv[None, :]
    return jnp.cos(ang).T, jnp.sin(ang).T


def _head_norm(x, g):
    ms = jnp.mean(x * x, axis=-1, keepdims=True)
    return x * lax.rsqrt(ms + EPS) * g


def _rope128(y, cos, sin):
    return y * cos + pltpu.roll(y, 64, 1) * sin


def _rope_rows(y, cos_t, sin_t):
    half = y.shape[0] // 2
    y1, y2 = y[:half], y[half:]
    return jnp.concatenate([y1 * cos_t - y2 * sin_t, y1 * sin_t + y2 * cos_t], axis=0)


def _qkv_prep_body(q_ref, k_ref, v_ref, cos_ref, sin_ref, cos_t_ref, sin_t_ref, qg_col_ref, kg_ref,
                   qT_ref, ko_ref, vT_ref, *, n_kv, rope):
    tm = q_ref.shape[0]
    kg = kg_ref[...]
    scale = HEAD_DIM ** -0.5 * LOG2E
    if rope:
        cos = cos_ref[...]
        sin = sin_ref[...]
        cos_t = cos_t_ref[...]
        sin_t = sin_t_ref[...]
    qg_t = jnp.broadcast_to(qg_col_ref[...], (HEAD_DIM, tm)) * scale
    for h in range(N_HEADS):
        sl = slice(h * HEAD_DIM, (h + 1) * HEAD_DIM)
        x = q_ref[:, sl].T
        ms = jnp.mean(x * x, axis=0, keepdims=True)
        y = x * lax.rsqrt(ms + EPS) * qg_t
        if rope:
            y = _rope_rows(y, cos_t, sin_t)
        qT_ref[0, sl, :] = y.astype(BF16)
    k_out = []
    for g in range(n_kv):
        sl = slice(g * HEAD_DIM, (g + 1) * HEAD_DIM)
        y = _head_norm(k_ref[:, sl], kg)
        if rope:
            y = _rope128(y, cos, sin)
        if ko_ref.ndim == 4:
            ko_ref[0, g] = y.astype(BF16)
        else:
            ko_ref[0, :, sl] = y.astype(BF16)
        vT_ref[0, sl, :] = v_ref[:, sl].T.astype(BF16)
        k_out.append(y)
    return k_out


def _dsa_prep_kernel(q_ref, k_ref, v_ref, qi_ref, tail_ref, cos_ref, sin_ref, cos_t_ref, sin_t_ref,
                     cos64_ref, sin64_ref, cos32_t_ref, sin32_t_ref,
                     qg_ref, kg_ref, qT_ref, ko_ref, vT_ref, qiT_ref, kia_ref, kib_ref, wT_ref):
    _qkv_prep_body(q_ref, k_ref, v_ref, cos_ref, sin_ref, cos_t_ref, sin_t_ref, qg_ref, kg_ref,
                   qT_ref, ko_ref, vT_ref, n_kv=N_KV_HEADS, rope=True)
    tm = tail_ref.shape[0]
    lane = lax.broadcasted_iota(I32, (tm, LANE), 1)
    first_half = (lane % IDX_DIM) < (IDX_DIM // 2)
    c64 = cos64_ref[...]
    s64 = sin64_ref[...]

    def rope64(x):
        rot = jnp.where(first_half, pltpu.roll(x, LANE - IDX_DIM // 2, 1), pltpu.roll(x, IDX_DIM // 2, 1))
        return x * c64 + rot * s64

    c32_t = cos32_t_ref[...]
    s32_t = sin32_t_ref[...]
    for j in range(IDX_HEADS * IDX_DIM // LANE):
        sl = slice(j * LANE, (j + 1) * LANE)
        x = qi_ref[:, sl].T * (IDX_DIM ** -0.5)
        y = jnp.concatenate([_rope_rows(x[:IDX_DIM], c32_t, s32_t), _rope_rows(x[IDX_DIM:], c32_t, s32_t)], axis=0)
        qiT_ref[0, sl, :] = y.astype(BF16)
    tail = tail_ref[...]
    ka = jnp.where(lane < IDX_DIM, rope64(tail), 0.0)
    kia_ref[0] = ka.astype(BF16)
    kib_ref[0] = pltpu.roll(ka, IDX_DIM, 1).astype(BF16)
    wT_ref[0] = (tail * (IDX_HEADS ** -0.5)).T


def _moba_prep_kernel(q_ref, k_ref, v_ref, cos_ref, sin_ref, cos_t_ref, sin_t_ref, qg_ref, kg_ref,
                      qT_ref, ko_ref, vT_ref, kmean_ref):
    k_out = _qkv_prep_body(q_ref, k_ref, v_ref, cos_ref, sin_ref, cos_t_ref, sin_t_ref, qg_ref, kg_ref,
                           qT_ref, ko_ref, vT_ref, n_kv=N_KV_HEADS, rope=True)
    for g in range(N_KV_HEADS):
        km = jnp.mean(k_out[g], axis=0, keepdims=True)
        kmean_ref[0, g, pl.ds(pl.program_id(1), 1), :] = jnp.concatenate([km] * GROUP, axis=1)


def _fox_prep_kernel(q_ref, k_ref, v_ref, f_ref, fb_ref, qg_ref, kg_ref,
                     qT_ref, ko_ref, vT_ref, cum_ref, cumT_ref, carry_ref):
    _qkv_prep_body(q_ref, k_ref, v_ref, None, None, None, None, qg_ref, kg_ref, qT_ref, ko_ref, vT_ref,
                   n_kv=N_HEADS, rope=False)

    @pl.when(pl.program_id(1) == 0)
    def _():
        carry_ref[...] = jnp.zeros_like(carry_ref)

    tm = f_ref.shape[0]
    x = f_ref[...] + fb_ref[...]
    lf = jnp.minimum(x, 0.0) - jnp.log(1.0 + jnp.exp(-jnp.abs(x)))
    hi = lf.astype(BF16)
    r1 = lf - hi.astype(F32)
    lo = r1.astype(BF16)
    lo2 = (r1 - lo.astype(F32)).astype(BF16)
    row = lax.broadcasted_iota(I32, (tm, tm), 0)
    col = lax.broadcasted_iota(I32, (tm, tm), 1)
    tri = (col <= row).astype(BF16)
    parts = jnp.dot(tri, jnp.concatenate([hi, lo, lo2], axis=1), preferred_element_type=F32)
    cum = parts[:, :LANE] + parts[:, LANE:2 * LANE] + parts[:, 2 * LANE:] + carry_ref[...]
    carry_ref[...] = cum[tm - 1:tm, :]
    cum2 = cum * LOG2E
    cum_ref[0] = cum2
    cumT_ref[0] = cum2.T


def _prep_specs(b_count, t, tm, z_q_blk, z_k_blk, z_v_blk, kv_width):
    nt = t // tm
    row = lambda b, i: b * nt + i
    in_specs = [
        pl.BlockSpec((tm, INNER), lambda b, i: (row(b, i), z_q_blk)),
        pl.BlockSpec((tm, kv_width), lambda b, i: (row(b, i), z_k_blk)),
        pl.BlockSpec((tm, kv_width), lambda b, i: (row(b, i), z_v_blk)),
    ]
    out_specs = [
        pl.BlockSpec((1, INNER, tm), lambda b, i: (b, 0, i)),
        pl.BlockSpec((1, tm, kv_width), lambda b, i: (b, i, 0)),
        pl.BlockSpec((1, kv_width, tm), lambda b, i: (b, 0, i)),
    ]
    out_shape = [
        jax.ShapeDtypeStruct((b_count, INNER, t), BF16),
        jax.ShapeDtypeStruct((b_count, t, kv_width), BF16),
        jax.ShapeDtypeStruct((b_count, kv_width, t), BF16),
    ]
    return row, in_specs, out_specs, out_shape


def _dsa_prep(z, z_tail, b_count, t, q_g, k_g, tm=256):
    row, in_specs, out_specs, out_shape = _prep_specs(b_count, t, tm, 0, INNER // KV_WIDTH,
                                                      INNER // KV_WIDTH + 1, KV_WIDTH)
    qi_w = IDX_HEADS * IDX_DIM
    qi_off = 2 * INNER + 2 * KV_WIDTH
    cos, sin = _rope_tables(t, HEAD_DIM, 1)
    cos64, sin64 = _rope_tables(t, IDX_DIM, LANE // IDX_DIM)
    cos_t, sin_t = _rope_tables_t(t, HEAD_DIM)
    cos32_t, sin32_t = _rope_tables_t(t, IDX_DIM)
    tab = pl.BlockSpec((tm, LANE), lambda b, i: (i, 0))
    tab_t = pl.BlockSpec((HEAD_DIM // 2, tm), lambda b, i: (0, i))
    tab32_t = pl.BlockSpec((IDX_DIM // 2, tm), lambda b, i: (0, i))
    gain = pl.BlockSpec((1, HEAD_DIM), lambda b, i: (0, 0))
    gain_col = pl.BlockSpec((HEAD_DIM, 1), lambda b, i: (0, 0))
    in_specs += [
        pl.BlockSpec((tm, qi_w), lambda b, i: (row(b, i), qi_off // qi_w)),
        pl.BlockSpec((tm, LANE), lambda b, i: (row(b, i), 0)),
        tab, tab, tab_t, tab_t, tab, tab, tab32_t, tab32_t, gain_col, gain,
    ]
    out_specs += [
        pl.BlockSpec((1, qi_w, tm), lambda b, i: (b, 0, i)),
        pl.BlockSpec((1, tm, LANE), lambda b, i: (b, i, 0)),
        pl.BlockSpec((1, tm, LANE), lambda b, i: (b, i, 0)),
        pl.BlockSpec((1, LANE, tm), lambda b, i: (b, 0, i)),
    ]
    out_shape += [
        jax.ShapeDtypeStruct((b_count, qi_w, t), BF16),
        jax.ShapeDtypeStruct((b_count, t, LANE), BF16),
        jax.ShapeDtypeStruct((b_count, t, LANE), BF16),
        jax.ShapeDtypeStruct((b_count, LANE, t), F32),
    ]
    return pl.pallas_call(
        _dsa_prep_kernel,
        grid=(b_count, t // tm),
        in_specs=in_specs, out_specs=out_specs, out_shape=out_shape,
        compiler_params=_cparams("parallel", "parallel"),
        name="dsa_prep",
    )(z, z, z, z, z_tail, cos, sin, cos_t, sin_t, cos64, sin64, cos32_t, sin32_t,
      q_g.reshape(HEAD_DIM, 1), k_g.reshape(1, HEAD_DIM))


def _moba_prep(z, b_count, t, q_g, k_g):
    tm = MOBA_BLOCK
    row, in_specs, out_specs, out_shape = _prep_specs(b_count, t, tm, 0, INNER // KV_WIDTH,
                                                      INNER // KV_WIDTH + 1, KV_WIDTH)
    cos, sin = _rope_tables(t, HEAD_DIM, 1)
    cos_t, sin_t = _rope_tables_t(t, HEAD_DIM)
    tab = pl.BlockSpec((tm, LANE), lambda b, i: (i, 0))
    tab_t = pl.BlockSpec((HEAD_DIM // 2, tm), lambda b, i: (0, i))
    gain = pl.BlockSpec((1, HEAD_DIM), lambda b, i: (0, 0))
    gain_col = pl.BlockSpec((HEAD_DIM, 1), lambda b, i: (0, 0))
    in_specs += [tab, tab, tab_t, tab_t, gain_col, gain]
    out_specs += [pl.BlockSpec((1, N_KV_HEADS, t // tm, GROUP * HEAD_DIM), lambda b, i: (b, 0, 0, 0))]
    out_shape += [jax.ShapeDtypeStruct((b_count, N_KV_HEADS, t // tm, GROUP * HEAD_DIM), F32)]
    return pl.pallas_call(
        _moba_prep_kernel,
        grid=(b_count, t // tm),
        in_specs=in_specs, out_specs=out_specs, out_shape=out_shape,
        compiler_params=_cparams("parallel", "arbitrary"),
        name="moba_prep",
    )(z, z, z, cos, sin, cos_t, sin_t, q_g.reshape(HEAD_DIM, 1), k_g.reshape(1, HEAD_DIM))


def _fox_prep(z, z_tail, b_count, t, f_bias, q_g, k_g, tm=256):
    row, in_specs, out_specs, out_shape = _prep_specs(b_count, t, tm, 0, 1, 2, INNER)
    out_specs[1] = pl.BlockSpec((1, N_HEADS, tm, HEAD_DIM), lambda b, i: (b, 0, i, 0))
    out_shape[1] = jax.ShapeDtypeStruct((b_count, N_HEADS, t, HEAD_DIM), BF16)
    gain = pl.BlockSpec((1, HEAD_DIM), lambda b, i: (0, 0))
    fb = jnp.pad(f_bias.reshape(1, N_HEADS), ((0, 0), (0, LANE - N_HEADS)))
    in_specs += [
        pl.BlockSpec((tm, LANE), lambda b, i: (row(b, i), 0)),
        pl.BlockSpec((1, LANE), lambda b, i: (0, 0)),
        pl.BlockSpec((HEAD_DIM, 1), lambda b, i: (0, 0)), gain,
    ]
    out_specs += [
        pl.BlockSpec((1, tm, LANE), lambda b, i: (b, i, 0)),
        pl.BlockSpec((1, LANE, tm), lambda b, i: (b, 0, i)),
    ]
    out_shape += [
        jax.ShapeDtypeStruct((b_count, t, LANE), F32),
        jax.ShapeDtypeStruct((b_count, LANE, t), F32),
    ]
    return pl.pallas_call(
        _fox_prep_kernel,
        grid=(b_count, t // tm),
        in_specs=in_specs, out_specs=out_specs, out_shape=out_shape,
        scratch_shapes=[pltpu.VMEM((1, LANE), F32)],
        compiler_params=_cparams("parallel", "arbitrary"),
        name="fox_prep",
    )(z, z, z, z_tail, fb, q_g.reshape(HEAD_DIM, 1), k_g.reshape(1, HEAD_DIM))


def _dsa_select_kernel(kia_ref, kib_ref, qiT_ref, wT_ref, bias_ref, key_ref, ans_ref):
    qb = pl.program_id(1)
    n_kt = key_ref.shape[0] // TK
    w = wT_ref[0]
    s_iota = lax.broadcasted_iota(I32, (TK, TQ), 0)
    t_iota = lax.broadcasted_iota(I32, (TK, TQ), 1)

    for kt in range(n_kt):
        rows = slice(kt * TK, (kt + 1) * TK)

        @pl.when(kt <= qb)
        def _():
            ka = kia_ref[0, rows, :]
            kb = kib_ref[0, rows, :]
            score = jnp.zeros((TK, TQ), F32)
            for j in range(IDX_HEADS // 2):
                qp = qiT_ref[0, j * LANE:(j + 1) * LANE, :]
                ra = jnp.maximum(jnp.dot(ka, qp, preferred_element_type=F32), 0.0)
                rb = jnp.maximum(jnp.dot(kb, qp, preferred_element_type=F32), 0.0)
                wa = w[IDX_DIM + 2 * j:IDX_DIM + 2 * j + 1, :]
                wb = w[IDX_DIM + 2 * j + 1:IDX_DIM + 2 * j + 2, :]
                score = score + (ra * wa + rb * wb)
            bits = lax.bitcast_convert_type(score + 0.0, I32)
            key = bits ^ ((bits >> 31) & 0x7FFFFFFF)
            causal = (kt * TK + s_iota) <= (qb * TQ + t_iota)
            key_ref[rows, :] = jnp.where(causal, key, INT_MIN)

        @pl.when(kt > qb)
        def _():
            key_ref[rows, :] = jnp.full((TK, TQ), INT_MIN, I32)

    def variant(v):
        n = (v + 1) * TK
        kf = float(DSA_TOPK)

        def count_ge(cand):
            return jnp.sum((key_ref[:n, :] >= cand).astype(F32), axis=0, keepdims=True)

        def bit_step(i, ans):
            cand = ans | (jnp.int32(1) << (30 - i))
            return jnp.where(count_ge(cand) >= kf, cand, ans)

        sign = jnp.where(count_ge(jnp.zeros((1, TQ), I32)) >= kf, 0, INT_MIN).astype(I32)
        ans_ref[...] = lax.fori_loop(0, 31, bit_step, sign)

    _for_each_query_tile(qb, n_kt, variant)
    ans = ans_ref[...]

    for kt in range(n_kt):
        rows = slice(kt * TK, (kt + 1) * TK)
        k = key_ref[rows, :]
        sel = (k >= ans) & (k > INT_MIN)
        bias_ref[0, rows, :] = jnp.where(sel, 0.0, NEG).astype(BF16)


def _dsa_select(kia, kib, qiT, wT):
    b_count, t, _ = kia.shape
    return pl.pallas_call(
        _dsa_select_kernel,
        grid=(b_count, t // TQ),
        in_specs=[pl.BlockSpec((1, t, LANE), lambda b, i: (b, 0, 0)),
                  pl.BlockSpec((1, t, LANE), lambda b, i: (b, 0, 0)),
                  pl.BlockSpec((1, IDX_HEADS * IDX_DIM, TQ), lambda b, i: (b, 0, i)),
                  pl.BlockSpec((1, LANE, TQ), lambda b, i: (b, 0, i))],
        out_specs=pl.BlockSpec((1, t, TQ), lambda b, i: (b, 0, i)),
        out_shape=jax.ShapeDtypeStruct((b_count, t, t), BF16),
        scratch_shapes=[pltpu.VMEM((t, TQ), I32), pltpu.VMEM((1, TQ), I32)],
        compiler_params=_cparams("parallel", "parallel"),
        name="dsa_select",
    )(kia, kib, qiT, wT)


def _causal_tile():
    s_iota = lax.broadcasted_iota(I32, (TK, TQ), 0)
    t_iota = lax.broadcasted_iota(I32, (TK, TQ), 1)
    return s_iota <= t_iota


def _softmax_attend(q, k_all, vt_all, bias_fn, n_tiles, causal_last):
    return _softmax_attend_heads([q], [k_all], [vt_all], [bias_fn], n_tiles, causal_last)[0]


def _softmax_attend_heads(qs, k_alls, vt_alls, bias_fns, n_tiles, causal_last):
    heads = range(len(qs))
    s_alls = [jnp.dot(k_alls[i], qs[i], preferred_element_type=F32) for i in heads]
    tiles = [[] for _ in heads]
    for i in heads:
        for j in range(n_tiles):
            sj = bias_fns[i](j, s_alls[i][j * TK:(j + 1) * TK])
            if causal_last and j == n_tiles - 1:
                sj = jnp.where(_causal_tile(), sj, NEG)
            tiles[i].append(sj)
    ms = [functools.reduce(jnp.maximum, [jnp.max(sj, axis=0, keepdims=True) for sj in tiles[i]]) for i in heads]
    outs = []
    for i in heads:
        ps = [jnp.exp2(sj - ms[i]) for sj in tiles[i]]
        l = functools.reduce(jnp.add, [jnp.sum(p, axis=0, keepdims=True) for p in ps])
        ps = [p.astype(BF16) for p in ps]
        p_all = ps[0] if n_tiles == 1 else jnp.concatenate(ps, axis=0)
        outs.append(jnp.dot(vt_alls[i], p_all, preferred_element_type=F32) * (1.0 / l))
    return outs


def _for_each_query_tile(qb, n_q, body):
    for v in range(n_q):
        pl.when(qb == v)(functools.partial(body, v))


def _pair_rows(i):
    base = pl.multiple_of(i * 2 * HEAD_DIM, 2 * HEAD_DIM)
    return [pl.ds(base, HEAD_DIM), pl.ds(base + HEAD_DIM, HEAD_DIM)]

def _gate_and_store(oT_ref, gate_ref, o_ref):
    for r in range(GROUP):
        sl = slice(r * HEAD_DIM, (r + 1) * HEAD_DIM)
        gate = gate_ref[:, sl]
        o_ref[:, sl] = (gate * jax.nn.sigmoid(gate) * oT_ref[sl, :].T).astype(BF16)


def _dsa_attn_kernel(qT_ref, k_ref, vT_ref, bias_ref, gate_ref, o_ref, oT_ref):
    def variant(v):
        n = (v + 1) * TK

        def bias_fn(j, s):
            return s + bias_ref[0, j * TK:(j + 1) * TK, :].astype(F32)

        def head_pair(i, carry):
            rows = _pair_rows(i)
            outs = _softmax_attend_heads([qT_ref[0, r, :] for r in rows], [k_ref[0, :n, :]] * 2,
                                         [vT_ref[0, :, :n]] * 2, [bias_fn] * 2, v + 1, False)
            for r, o in zip(rows, outs):
                oT_ref[r, :] = o
            return carry

        lax.fori_loop(0, GROUP // 2, head_pair, 0)

    _for_each_query_tile(pl.program_id(2), k_ref.shape[1] // TQ, variant)
    _gate_and_store(oT_ref, gate_ref, o_ref)


def _moba_attn_kernel(qT_ref, k_ref, vT_ref, kmean_ref, gate_ref, o_ref, oT_ref, selb_ref):
    own = pl.program_id(2)
    n_blocks = kmean_ref.shape[2]

    km = kmean_ref[0, 0].astype(BF16)
    gs = jnp.dot(km, qT_ref[0], preferred_element_type=F32)
    blk = lax.broadcasted_iota(I32, (n_blocks, TQ), 0)
    past = blk < own
    gs = jnp.where(past, gs, NEG)
    sel = jnp.zeros((n_blocks, TQ), jnp.bool_)
    for _ in range(min(MOBA_TOPK, n_blocks - 1)):
        mx = jnp.max(gs, axis=0, keepdims=True)
        first = jnp.min(jnp.where(gs == mx, blk, n_blocks), axis=0, keepdims=True)
        pick = blk == first
        sel = sel | pick
        gs = jnp.where(pick, -jnp.inf, gs)
    selb_ref[...] = jnp.where(sel & past, 0.0, NEG)

    def variant(v):
        n = (v + 1) * TK

        def bias_fn(j, s):
            return s if j == v else s + selb_ref[j:j + 1, :]

        def head_pair(i, carry):
            rows = _pair_rows(i)
            outs = _softmax_attend_heads([qT_ref[0, r, :] for r in rows], [k_ref[0, :n, :]] * 2,
                                         [vT_ref[0, :, :n]] * 2, [bias_fn] * 2, v + 1, True)
            for r, o in zip(rows, outs):
                oT_ref[r, :] = o
            return carry

        lax.fori_loop(0, GROUP // 2, head_pair, 0)

    _for_each_query_tile(own, n_blocks, variant)
    _gate_and_store(oT_ref, gate_ref, o_ref)


def _fox_attn_kernel(qT_ref, k_ref, vT_ref, cum_ref, cumT_ref, gate_ref, o_ref, oT_ref):
    g = pl.program_id(1)

    def variant(v):
        n = (v + 1) * TK

        def make_bias_fn(h):
            cq = cumT_ref[0, pl.ds(h, 1), :]
            onehot = (lax.broadcasted_iota(I32, (TK, LANE), 1) == h).astype(F32)

            def bias_fn(j, s):
                ck = jnp.sum(cum_ref[0, j * TK:(j + 1) * TK, :] * onehot, axis=1, keepdims=True)
                return s + (cq - ck)

            return bias_fn

        def head_pair(i, carry):
            rows = _pair_rows(i)
            heads = [2 * i, 2 * i + 1]
            outs = _softmax_attend_heads([qT_ref[0, r, :] for r in rows], [k_ref[0, r, :n, :] for r in heads],
                                         [vT_ref[0, r, :n] for r in rows],
                                         [make_bias_fn(g * GROUP + r) for r in heads], v + 1, True)
            for r, o in zip(rows, outs):
                oT_ref[r, :] = o
            return carry

        lax.fori_loop(0, GROUP // 2, head_pair, 0)

    _for_each_query_tile(pl.program_id(2), k_ref.shape[2] // TQ, variant)
    _gate_and_store(oT_ref, gate_ref, o_ref)


def _attn_scratch():
    return [pltpu.VMEM((GROUP * HEAD_DIM, TQ), F32)]


def _gqa_specs(t, gate_blk):
    nq = t // TQ
    gw = GROUP * HEAD_DIM
    in_specs = [
        pl.BlockSpec((1, gw, TQ), lambda b, g, i: (b, g, i)),
        pl.BlockSpec((1, t, HEAD_DIM), lambda b, g, i: (b, 0, g)),
        pl.BlockSpec((1, HEAD_DIM, t), lambda b, g, i: (b, g, 0)),
    ]
    gate_spec = pl.BlockSpec((TQ, gw), lambda b, g, i: (b * nq + i, gate_blk + g))
    out_spec = pl.BlockSpec((TQ, gw), lambda b, g, i: (b * nq + i, g))
    return nq, in_specs, gate_spec, out_spec


def _dsa_attn(qT, k, vT, bias, z):
    b_count, t, _ = k.shape
    gate_blk = (INNER + 2 * KV_WIDTH) // (GROUP * HEAD_DIM)
    nq, in_specs, gate_spec, out_spec = _gqa_specs(t, gate_blk)
    in_specs += [pl.BlockSpec((1, t, TQ), lambda b, g, i: (b, 0, i)), gate_spec]
    return pl.pallas_call(
        _dsa_attn_kernel,
        grid=(b_count, N_KV_HEADS, nq),
        in_specs=in_specs, out_specs=out_spec,
        out_shape=jax.ShapeDtypeStruct((b_count * t, INNER), BF16),
        scratch_shapes=_attn_scratch(),
        compiler_params=_cparams("parallel", "parallel", "parallel"),
        name="dsa_attn",
    )(qT, k, vT, bias, z)


def _moba_attn(qT, k, vT, kmean, z):
    b_count, t, _ = k.shape
    gate_blk = (INNER + 2 * KV_WIDTH) // (GROUP * HEAD_DIM)
    nq, in_specs, gate_spec, out_spec = _gqa_specs(t, gate_blk)
    n_blocks = t // MOBA_BLOCK
    in_specs += [pl.BlockSpec((1, 1, n_blocks, GROUP * HEAD_DIM), lambda b, g, i: (b, g, 0, 0)), gate_spec]
    return pl.pallas_call(
        _moba_attn_kernel,
        grid=(b_count, N_KV_HEADS, nq),
        in_specs=in_specs, out_specs=out_spec,
        out_shape=jax.ShapeDtypeStruct((b_count * t, INNER), BF16),
        scratch_shapes=_attn_scratch() + [pltpu.VMEM((n_blocks, TQ), F32)],
        compiler_params=_cparams("parallel", "parallel", "parallel"),
        name="moba_attn",
    )(qT, k, vT, kmean, z)


def _fox_attn(qT, k, vT, cum, cumT, z):
    b_count, _, t, _ = k.shape
    nq = t // TQ
    gw = GROUP * HEAD_DIM
    gate_blk = 3 * INNER // gw
    return pl.pallas_call(
        _fox_attn_kernel,
        grid=(b_count, N_HEADS // GROUP, nq),
        in_specs=[pl.BlockSpec((1, gw, TQ), lambda b, g, i: (b, g, i)),
                  pl.BlockSpec((1, GROUP, t, HEAD_DIM), lambda b, g, i: (b, g, 0, 0)),
                  pl.BlockSpec((1, gw, t), lambda b, g, i: (b, g, 0)),
                  pl.BlockSpec((1, t, LANE), lambda b, g, i: (b, 0, 0)),
                  pl.BlockSpec((1, LANE, TQ), lambda b, g, i: (b, 0, i)),
                  pl.BlockSpec((TQ, gw), lambda b, g, i: (b * nq + i, gate_blk + g))],
        out_specs=pl.BlockSpec((TQ, gw), lambda b, g, i: (b * nq + i, g)),
        out_shape=jax.ShapeDtypeStruct((b_count * t, INNER), BF16),
        scratch_shapes=_attn_scratch(),
        compiler_params=_cparams("parallel", "parallel", "parallel"),
        name="fox_attn",
    )(qT, k, vT, cum, cumT, z)


def _retention_kernel(q_ref, k_ref, v_ref, gate_ref, cos_ref, sin_ref, lg_ref, gn_ref, o_ref,
                      state_ref, dmask_ref):
    c = RET_CHUNK
    heads = range(RET_PAIR)
    lgs = [lg_ref[h][:, :1] for h in heads]

    @pl.when(pl.program_id(2) == 0)
    def _():
        state_ref[...] = jnp.zeros_like(state_ref)
        diff = (lax.broadcasted_iota(I32, (c, c), 0) - lax.broadcasted_iota(I32, (c, c), 1)).astype(F32)
        for h in heads:
            dmask_ref[h] = jnp.where(diff >= 0, jnp.exp(jnp.maximum(diff, 0.0) * lgs[h]), 0.0)

    cos = cos_ref[...]
    sin = sin_ref[...]
    half = RET_QK_DIM // 2

    def rope(ref, h):
        x1 = ref[:, h * RET_QK_DIM:h * RET_QK_DIM + half]
        x2 = ref[:, h * RET_QK_DIM + half:(h + 1) * RET_QK_DIM]
        return jnp.concatenate([x1 * cos - x2 * sin, x1 * sin + x2 * cos], axis=1)

    i_col = lax.broadcasted_iota(I32, (c, 1), 0).astype(F32)
    vsl = [slice(h * RET_V_DIM, (h + 1) * RET_V_DIM) for h in heads]
    qs = [rope(q_ref, h) for h in heads]
    ks = [rope(k_ref, h) * (RET_QK_DIM ** -0.5) for h in heads]
    inners = [lax.dot_general(qs[h].astype(BF16), ks[h].astype(BF16), (((1,), (1,)), ((), ())),
                              preferred_element_type=F32) for h in heads]
    vs = [v_ref[:, vsl[h]].astype(BF16) for h in heads]
    cross = [jnp.dot((qs[h] * jnp.exp((i_col + 1.0) * lgs[h])).astype(BF16), state_ref[h].astype(BF16),
                     preferred_element_type=F32) for h in heads]
    kdTs = [(ks[h] * jnp.exp((c - 1.0 - i_col) * lgs[h])).T.astype(BF16) for h in heads]
    os_ = [jnp.dot((inners[h] * dmask_ref[h]).astype(BF16), vs[h], preferred_element_type=F32) + cross[h]
           for h in heads]
    for h in heads:
        state_ref[h] = state_ref[h] * jnp.exp(c * lgs[h]) + jnp.dot(kdTs[h], vs[h], preferred_element_type=F32)
    for h in heads:
        o = os_[h]
        mu = jnp.mean(o, axis=-1, keepdims=True)
        var = jnp.mean(jnp.square(o - mu), axis=-1, keepdims=True)
        on = (o - mu) * lax.rsqrt(var + EPS) * gn_ref[:, vsl[h]]
        gate = gate_ref[:, vsl[h]]
        o_ref[:, vsl[h]] = (gate * jax.nn.sigmoid(gate) * on).astype(BF16)


def _retention(z, b_count, t, gn_g):
    c = RET_CHUNK
    nc = t // c
    half = RET_QK_DIM // 2
    pos = jnp.arange(t, dtype=jnp.int32)
    inv = ROPE_THETA ** (-jnp.arange(0, RET_QK_DIM, 2, dtype=F32) / RET_QK_DIM)
    ang = pos.astype(F32)[:, None] * inv[None, :]
    cos, sin = jnp.cos(ang), jnp.sin(ang)
    log_gamma = jnp.log(1.0 - 2.0 ** (-5.0 - jnp.arange(RET_HEADS, dtype=F32)))
    lg = jnp.broadcast_to(log_gamma[:, None, None], (RET_HEADS, 1, LANE))
    qk_w = RET_HEADS * RET_QK_DIM
    qk_blk = RET_PAIR * RET_QK_DIM
    v_blk = RET_PAIR * RET_V_DIM
    row = lambda b, h, i: b * nc + i
    return pl.pallas_call(
        _retention_kernel,
        grid=(b_count, RET_HEADS // RET_PAIR, nc),
        in_specs=[pl.BlockSpec((c, qk_blk), lambda b, h, i: (row(b, h, i), h)),
                  pl.BlockSpec((c, qk_blk), lambda b, h, i: (row(b, h, i), qk_w // qk_blk + h)),
                  pl.BlockSpec((c, v_blk), lambda b, h, i: (row(b, h, i), 2 * qk_w // v_blk + h)),
                  pl.BlockSpec((c, v_blk), lambda b, h, i: (row(b, h, i), (2 * qk_w + RET_INNER) // v_blk + h)),
                  pl.BlockSpec((c, half), lambda b, h, i: (i, 0)),
                  pl.BlockSpec((c, half), lambda b, h, i: (i, 0)),
                  pl.BlockSpec((RET_PAIR, 1, LANE), lambda b, h, i: (h, 0, 0)),
                  pl.BlockSpec((1, v_blk), lambda b, h, i: (0, h))],
        out_specs=pl.BlockSpec((c, v_blk), lambda b, h, i: (row(b, h, i), h)),
        out_shape=jax.ShapeDtypeStruct((b_count * t, RET_INNER), BF16),
        scratch_shapes=[pltpu.VMEM((RET_PAIR, RET_QK_DIM, RET_V_DIM), F32), pltpu.VMEM((RET_PAIR, c, c), F32)],
        compiler_params=_cparams("parallel", "parallel", "arbitrary"),
        name="retention",
    )(z, z, z, z, cos, sin, lg, gn_g.reshape(1, RET_INNER))


def _tail_proj_kernel(a_ref, wt_ref, o_ref):
    o_ref[...] = lax.dot_general(a_ref[...], wt_ref[...], (((1,), (1,)), ((), ())), preferred_element_type=F32)


def _tail_proj(hn, w_in_t, n_main, name, tm=1024):
    m, k = hn.shape
    wt_tail = jnp.pad(w_in_t[n_main:, :], ((0, LANE - (w_in_t.shape[0] - n_main)), (0, 0))).astype(BF16)
    return pl.pallas_call(
        _tail_proj_kernel,
        grid=(m // tm,),
        in_specs=[pl.BlockSpec((tm, k), lambda i: (i, 0)),
                  pl.BlockSpec((LANE, k), lambda i: (0, 0))],
        out_specs=pl.BlockSpec((tm, LANE), lambda i: (i, 0)),
        out_shape=jax.ShapeDtypeStruct((m, LANE), F32),
        compiler_params=_cparams("parallel"),
        name=name,
    )(hn, wt_tail)


def _dsa_layer(h, hn, b_count, t, w_in, q_g, k_g, w_out, next_g):
    n_main = 2 * INNER + 2 * KV_WIDTH + IDX_HEADS * IDX_DIM
    w_in_t = w_in.T
    z = _in_proj_wt(hn, w_in_t, n_main, 1024, 1024, "dsa_in_proj")
    z_tail = _tail_proj(hn, w_in_t, n_main, "dsa_tail_proj")
    qT, k, vT, qiT, kia, kib, wT = _dsa_prep(z, z_tail, b_count, t, q_g, k_g)
    bias = _dsa_select(kia, kib, qiT, wT)
    gated = _dsa_attn(qT, k, vT, bias, z)
    return _out_proj(gated, w_out.astype(BF16), h, next_g, 512, "dsa_out_proj")


def _moba_layer(h, hn, b_count, t, w_in, q_g, k_g, w_out, next_g):
    z = _in_proj(hn, w_in, w_in.shape[1], 1024, 1024, "moba_in_proj")
    qT, k, vT, kmean = _moba_prep(z, b_count, t, q_g, k_g)
    gated = _moba_attn(qT, k, vT, kmean, z)
    return _out_proj(gated, w_out.astype(BF16), h, next_g, 512, "moba_out_proj")


def _ret_layer(h, hn, b_count, t, w_in, gn_g, w_out, next_g):
    z = _in_proj(hn, w_in, w_in.shape[1], 1024, 1024, "ret_in_proj")
    gated = _retention(z, b_count, t, gn_g)
    return _out_proj(gated, w_out.astype(BF16), h, next_g, 256, "ret_out_proj")


def _fox_layer(h, hn, b_count, t, w_in, f_bias, q_g, k_g, w_out, next_g):
    n_main = 4 * INNER
    w_in_t = w_in.T
    z = _in_proj_wt(hn, w_in_t, n_main, 1024, 1024, "fox_in_proj")
    z_tail = _tail_proj(hn, w_in_t, n_main, "fox_tail_proj")
    qT, k, vT, cum, cumT = _fox_prep(z, z_tail, b_count, t, f_bias, q_g, k_g)
    gated = _fox_attn(qT, k, vT, cum, cumT, z)
    return _out_proj(gated, w_out.astype(BF16), h, next_g, 512, "fox_out_proj")


def kernel(x, a_norm, a_w_in, a_q_norm, a_k_norm, a_w_out, b_norm, b_w_in, b_q_norm, b_k_norm, b_w_out,
           c_norm, c_w_in, c_gn, c_w_out, d_norm, d_w_in, d_f_bias, d_q_norm, d_k_norm, d_w_out):
    b_count, t, d = x.shape
    assert d == D_MODEL and t % TQ == 0 and t // 4 >= DSA_TOPK
    depth = 4
    norms = (a_norm, b_norm, c_norm, d_norm)
    h = x.reshape(b_count * t, d)
    hn = _rmsnorm(h, a_norm[0])
    for i in range(depth):
        m, j = i % 4, i // 4
        next_g = norms[(i + 1) % 4][(i + 1) // 4] if i + 1 < depth else None
        if m == 0:
            h, hn = _dsa_layer(h, hn, b_count, t, a_w_in[j], a_q_norm[j], a_k_norm[j], a_w_out[j], next_g)
        elif m == 1:
            h, hn = _moba_layer(h, hn, b_count, t, b_w_in[j], b_q_norm[j], b_k_norm[j], b_w_out[j], next_g)
        elif m == 2:
            h, hn = _ret_layer(h, hn, b_count, t, c_w_in[j], c_gn[j], c_w_out[j], next_g)
        else:
            h, hn = _fox_layer(h, hn, b_count, t, d_w_in[j], d_f_bias[j], d_q_norm[j], d_k_norm[j], d_w_out[j],
                               next_g)
    return h.reshape(b_count, t, d)
```

```python
import functools

import jax
import jax.numpy as jnp
import numpy as np
from jax import lax
from jax.experimental import pallas as pl
from jax.experimental.pallas import tpu as pltpu

F32 = jnp.float32
BF16 = jnp.bfloat16
I32 = jnp.int32

D_MODEL = 2048
HEAD_DIM = 128
N_HEADS = 16
N_KV_HEADS = 4
GROUP = N_HEADS // N_KV_HEADS
INNER = N_HEADS * HEAD_DIM
KV_WIDTH = N_KV_HEADS * HEAD_DIM
IDX_HEADS = 16
IDX_DIM = 64
DSA_TOPK = 256
MOBA_BLOCK = 256
MOBA_TOPK = 3
RET_HEADS = 8
RET_QK_DIM = 256
RET_V_DIM = 512
RET_INNER = RET_HEADS * RET_V_DIM
ROPE_THETA = 10000.0
EPS = 1e-6
NEG = -1e30
INT_MIN = -(2 ** 31)
LOG2E = 1.4426950408889634

LANE = 128
TQ = 256
TK = 256
KEY_CHUNK = 256
RET_CHUNK = 256
RET_PAIR = 2
VMEM_LIMIT = 56 * 1024 * 1024


def _cparams(*sem):
    return pltpu.CompilerParams(dimension_semantics=sem, vmem_limit_bytes=VMEM_LIMIT)


def _rmsnorm_kernel(x_ref, g_ref, o_ref):
    x = x_ref[...]
    ms = jnp.mean(x * x, axis=-1, keepdims=True)
    o_ref[...] = (x * lax.rsqrt(ms + EPS) * g_ref[...]).astype(o_ref.dtype)


def _rmsnorm(x, g, tm=512):
    m, d = x.shape
    return pl.pallas_call(
        _rmsnorm_kernel,
        grid=(m // tm,),
        in_specs=[pl.BlockSpec((tm, d), lambda i: (i, 0)),
                  pl.BlockSpec((1, d), lambda i: (0, 0))],
        out_specs=pl.BlockSpec((tm, d), lambda i: (i, 0)),
        out_shape=jax.ShapeDtypeStruct((m, d), BF16),
        compiler_params=_cparams("parallel"),
        name="rmsnorm",
    )(x, g.reshape(1, d))


def _in_proj_kernel(a_ref, w_ref, o_ref, wb_ref):
    @pl.when(pl.program_id(1) == 0)
    def _():
        wb_ref[...] = w_ref[...].astype(BF16)

    o_ref[...] = jnp.dot(a_ref[...], wb_ref[...], preferred_element_type=F32)


def _in_proj(a, w, n, tm, tn, name):
    m, k = a.shape
    return pl.pallas_call(
        _in_proj_kernel,
        grid=(n // tn, m // tm),
        in_specs=[pl.BlockSpec((tm, k), lambda j, i: (i, 0)),
                  pl.BlockSpec((k, tn), lambda j, i: (0, j))],
        out_specs=pl.BlockSpec((tm, tn), lambda j, i: (i, j)),
        out_shape=jax.ShapeDtypeStruct((m, n), F32),
        scratch_shapes=[pltpu.VMEM((k, tn), BF16)],
        compiler_params=_cparams("parallel", "arbitrary"),
        name=name,
    )(a, w)


def _in_proj_wt_kernel(a_ref, wt_ref, o_ref, wb_ref):
    @pl.when(pl.program_id(1) == 0)
    def _():
        wb_ref[...] = wt_ref[...].T.astype(BF16)

    o_ref[...] = jnp.dot(a_ref[...], wb_ref[...], preferred_element_type=F32)


def _in_proj_wt(a, wt, n, tm, tn, name):
    m, k = a.shape
    return pl.pallas_call(
        _in_proj_wt_kernel,
        grid=(n // tn, m // tm),
        in_specs=[pl.BlockSpec((tm, k), lambda j, i: (i, 0)),
                  pl.BlockSpec((tn, k), lambda j, i: (j, 0))],
        out_specs=pl.BlockSpec((tm, tn), lambda j, i: (i, j)),
        out_shape=jax.ShapeDtypeStruct((m, n), F32),
        scratch_shapes=[pltpu.VMEM((k, tn), BF16)],
        compiler_params=_cparams("parallel", "arbitrary"),
        name=name,
    )(a, wt)


def _out_proj_kernel(a_ref, w_ref, r_ref, o_ref):
    o_ref[...] = r_ref[...] + jnp.dot(a_ref[...], w_ref[...], preferred_element_type=F32)


def _out_proj_norm_kernel(a_ref, w_ref, r_ref, g_ref, o_ref, hn_ref):
    h = r_ref[...] + jnp.dot(a_ref[...], w_ref[...], preferred_element_type=F32)
    o_ref[...] = h
    ms = jnp.mean(h * h, axis=-1, keepdims=True)
    hn_ref[...] = (h * lax.rsqrt(ms + EPS) * g_ref[...]).astype(BF16)


def _out_proj(a, w, res, next_g, tm, name):
    m, k = a.shape
    n = w.shape[1]
    in_specs = [pl.BlockSpec((tm, k), lambda i: (i, 0)),
                pl.BlockSpec((k, n), lambda i: (0, 0)),
                pl.BlockSpec((tm, n), lambda i: (i, 0))]
    out_specs = [pl.BlockSpec((tm, n), lambda i: (i, 0))]
    out_shape = [jax.ShapeDtypeStruct((m, n), F32)]
    args = [a, w, res]
    if next_g is not None:
        in_specs.append(pl.BlockSpec((1, n), lambda i: (0, 0)))
        out_specs.append(pl.BlockSpec((tm, n), lambda i: (i, 0)))
        out_shape.append(jax.ShapeDtypeStruct((m, n), BF16))
        args.append(next_g.reshape(1, n))
    out = pl.pallas_call(
        _out_proj_kernel if next_g is None else _out_proj_norm_kernel,
        grid=(m // tm,),
        in_specs=in_specs, out_specs=out_specs, out_shape=out_shape,
        compiler_params=_cparams("parallel"),
        name=name,
    )(*args)
    return (out[0], out[1]) if next_g is not None else (out[0], None)


def _rope_tables(t, d, reps):
    pos = jnp.arange(t, dtype=jnp.int32)
    inv = ROPE_THETA ** (-jnp.arange(0, d, 2, dtype=F32) / d)
    ang = pos.astype(F32)[:, None] * inv[None, :]
    cos, sin = jnp.cos(ang), jnp.sin(ang)
    cos_t = jnp.tile(jnp.concatenate([cos, cos], axis=-1), (1, reps))
    sin_t = jnp.tile(jnp.concatenate([-sin, sin], axis=-1), (1, reps))
    return cos_t, sin_t


def _rope_tables_t(t, d):
    pos = jnp.arange(t, dtype=jnp.int32)
    inv = ROPE_THETA ** (-jnp.arange(0, d, 2, dtype=F32) / d)
    ang = pos.astype(F32)[:, None] * inv[None, :]
    return jnp.cos(ang).T, jnp.sin(ang).T


def _silu(x):
    hx = 0.5 * x
    return hx + hx * jnp.tanh(hx)


def _head_norm(x, g):
    ms = jnp.mean(x * x, axis=-1, keepdims=True)
    return x * lax.rsqrt(ms + EPS) * g


def _rope128(y, cos, sin):
    return y * cos + pltpu.roll(y, 64, 1) * sin


def _rope_rows(y, cos_t, sin_t):
    half = y.shape[0] // 2
    y1, y2 = y[:half], y[half:]
    return jnp.concatenate([y1 * cos_t - y2 * sin_t, y1 * sin_t + y2 * cos_t], axis=0)


def _qkv_prep_body(q_ref, k_ref, v_ref, cos_ref, sin_ref, cos_t_ref, sin_t_ref, qg_col_ref, kg_ref,
                   qT_ref, ko_ref, vT_ref, *, n_kv, rope):
    tm = q_ref.shape[0]
    kg = kg_ref[...]
    scale = HEAD_DIM ** -0.5 * LOG2E
    if rope:
        cos = cos_ref[...]
        sin = sin_ref[...]
        cos_t = cos_t_ref[...]
        sin_t = sin_t_ref[...]
    qg_t = jnp.broadcast_to(qg_col_ref[...], (HEAD_DIM, tm)) * scale
    for h in range(N_HEADS):
        sl = slice(h * HEAD_DIM, (h + 1) * HEAD_DIM)
        x = q_ref[:, sl].T
        ms = jnp.mean(x * x, axis=0, keepdims=True)
        y = x * lax.rsqrt(ms + EPS) * qg_t
        if rope:
            y = _rope_rows(y, cos_t, sin_t)
        qT_ref[0, sl, :] = y.astype(BF16)
    k_out = []
    for g in range(n_kv):
        sl = slice(g * HEAD_DIM, (g + 1) * HEAD_DIM)
        y = _head_norm(k_ref[:, sl], kg)
        if rope:
            y = _rope128(y, cos, sin)
        if ko_ref.ndim == 4:
            ko_ref[0, g] = y.astype(BF16)
        else:
            ko_ref[0, :, sl] = y.astype(BF16)
        vT_ref[0, sl, :] = v_ref[:, sl].T.astype(BF16)
        k_out.append(y)
    return k_out


def _dsa_prep_kernel(q_ref, k_ref, v_ref, qi_ref, tail_ref, cos_ref, sin_ref, cos_t_ref, sin_t_ref,
                     cos64_ref, sin64_ref, cos32_t_ref, sin32_t_ref,
                     qg_ref, kg_ref, qT_ref, ko_ref, vT_ref, qiT_ref, kia_ref, kib_ref, wT_ref):
    _qkv_prep_body(q_ref, k_ref, v_ref, cos_ref, sin_ref, cos_t_ref, sin_t_ref, qg_ref, kg_ref,
                   qT_ref, ko_ref, vT_ref, n_kv=N_KV_HEADS, rope=True)
    tm = tail_ref.shape[0]
    lane = lax.broadcasted_iota(I32, (tm, LANE), 1)
    first_half = (lane % IDX_DIM) < (IDX_DIM // 2)
    c64 = cos64_ref[...]
    s64 = sin64_ref[...]

    def rope64(x):
        rot = jnp.where(first_half, pltpu.roll(x, LANE - IDX_DIM // 2, 1), pltpu.roll(x, IDX_DIM // 2, 1))
        return x * c64 + rot * s64

    c32_t = cos32_t_ref[...]
    s32_t = sin32_t_ref[...]
    for j in range(IDX_HEADS * IDX_DIM // LANE):
        sl = slice(j * LANE, (j + 1) * LANE)
        x = qi_ref[:, sl].T * (IDX_DIM ** -0.5)
        y = jnp.concatenate([_rope_rows(x[:IDX_DIM], c32_t, s32_t), _rope_rows(x[IDX_DIM:], c32_t, s32_t)], axis=0)
        qiT_ref[0, sl, :] = y.astype(BF16)
    tail = tail_ref[...]
    ka = jnp.where(lane < IDX_DIM, rope64(tail), 0.0)
    kia_ref[0] = ka.astype(BF16)
    kib_ref[0] = pltpu.roll(ka, IDX_DIM, 1).astype(BF16)
    wT_ref[0] = (tail * (IDX_HEADS ** -0.5)).T


def _moba_prep_kernel(q_ref, k_ref, v_ref, cos_ref, sin_ref, cos_t_ref, sin_t_ref, qg_ref, kg_ref,
                      qT_ref, ko_ref, vT_ref, kmean_ref):
    k_out = _qkv_prep_body(q_ref, k_ref, v_ref, cos_ref, sin_ref, cos_t_ref, sin_t_ref, qg_ref, kg_ref,
                           qT_ref, ko_ref, vT_ref, n_kv=N_KV_HEADS, rope=True)
    for g in range(N_KV_HEADS):
        km = jnp.mean(k_out[g], axis=0, keepdims=True)
        kmean_ref[0, g, pl.ds(pl.program_id(1), 1), :] = jnp.concatenate([km] * GROUP, axis=1)


def _fox_prep_kernel(q_ref, k_ref, v_ref, f_ref, fb_ref, qg_ref, kg_ref,
                     qT_ref, ko_ref, vT_ref, cum_ref, cumT_ref, carry_ref):
    _qkv_prep_body(q_ref, k_ref, v_ref, None, None, None, None, qg_ref, kg_ref, qT_ref, ko_ref, vT_ref,
                   n_kv=N_HEADS, rope=False)

    @pl.when(pl.program_id(1) == 0)
    def _():
        carry_ref[...] = jnp.zeros_like(carry_ref)

    tm = f_ref.shape[0]
    x = f_ref[...] + fb_ref[...]
    lf = jnp.minimum(x, 0.0) - jnp.log(1.0 + jnp.exp(-jnp.abs(x)))
    hi = lf.astype(BF16)
    r1 = lf - hi.astype(F32)
    lo = r1.astype(BF16)
    lo2 = (r1 - lo.astype(F32)).astype(BF16)
    row = lax.broadcasted_iota(I32, (tm, tm), 0)
    col = lax.broadcasted_iota(I32, (tm, tm), 1)
    tri = (col <= row).astype(BF16)
    parts = jnp.dot(tri, jnp.concatenate([hi, lo, lo2], axis=1), preferred_element_type=F32)
    cum = parts[:, :LANE] + parts[:, LANE:2 * LANE] + parts[:, 2 * LANE:] + carry_ref[...]
    carry_ref[...] = cum[tm - 1:tm, :]
    cum2 = cum * LOG2E
    cum_ref[0] = cum2
    cumT_ref[0] = cum2.T


def _prep_specs(b_count, t, tm, z_q_blk, z_k_blk, z_v_blk, kv_width):
    nt = t // tm
    row = lambda b, i: b * nt + i
    in_specs = [
        pl.BlockSpec((tm, INNER), lambda b, i: (row(b, i), z_q_blk)),
        pl.BlockSpec((tm, kv_width), lambda b, i: (row(b, i), z_k_blk)),
        pl.BlockSpec((tm, kv_width), lambda b, i: (row(b, i), z_v_blk)),
    ]
    out_specs = [
        pl.BlockSpec((1, INNER, tm), lambda b, i: (b, 0, i)),
        pl.BlockSpec((1, tm, kv_width), lambda b, i: (b, i, 0)),
        pl.BlockSpec((1, kv_width, tm), lambda b, i: (b, 0, i)),
    ]
    out_shape = [
        jax.ShapeDtypeStruct((b_count, INNER, t), BF16),
        jax.ShapeDtypeStruct((b_count, t, kv_width), BF16),
        jax.ShapeDtypeStruct((b_count, kv_width, t), BF16),
    ]
    return row, in_specs, out_specs, out_shape


def _dsa_prep(z, z_tail, b_count, t, q_g, k_g, tm=256):
    row, in_specs, out_specs, out_shape = _prep_specs(b_count, t, tm, 0, INNER // KV_WIDTH,
                                                      INNER // KV_WIDTH + 1, KV_WIDTH)
    qi_w = IDX_HEADS * IDX_DIM
    qi_off = 2 * INNER + 2 * KV_WIDTH
    cos, sin = _rope_tables(t, HEAD_DIM, 1)
    cos64, sin64 = _rope_tables(t, IDX_DIM, LANE // IDX_DIM)
    cos_t, sin_t = _rope_tables_t(t, HEAD_DIM)
    cos32_t, sin32_t = _rope_tables_t(t, IDX_DIM)
    tab = pl.BlockSpec((tm, LANE), lambda b, i: (i, 0))
    tab_t = pl.BlockSpec((HEAD_DIM // 2, tm), lambda b, i: (0, i))
    tab32_t = pl.BlockSpec((IDX_DIM // 2, tm), lambda b, i: (0, i))
    gain = pl.BlockSpec((1, HEAD_DIM), lambda b, i: (0, 0))
    gain_col = pl.BlockSpec((HEAD_DIM, 1), lambda b, i: (0, 0))
    in_specs += [
        pl.BlockSpec((tm, qi_w), lambda b, i: (row(b, i), qi_off // qi_w)),
        pl.BlockSpec((tm, LANE), lambda b, i: (row(b, i), 0)),
        tab, tab, tab_t, tab_t, tab, tab, tab32_t, tab32_t, gain_col, gain,
    ]
    out_specs += [
        pl.BlockSpec((1, qi_w, tm), lambda b, i: (b, 0, i)),
        pl.BlockSpec((1, tm, LANE), lambda b, i: (b, i, 0)),
        pl.BlockSpec((1, tm, LANE), lambda b, i: (b, i, 0)),
        pl.BlockSpec((1, LANE, tm), lambda b, i: (b, 0, i)),
    ]
    out_shape += [
        jax.ShapeDtypeStruct((b_count, qi_w, t), BF16),
        jax.ShapeDtypeStruct((b_count, t, LANE), BF16),
        jax.ShapeDtypeStruct((b_count, t, LANE), BF16),
        jax.ShapeDtypeStruct((b_count, LANE, t), F32),
    ]
    return pl.pallas_call(
        _dsa_prep_kernel,
        grid=(b_count, t // tm),
        in_specs=in_specs, out_specs=out_specs, out_shape=out_shape,
        compiler_params=_cparams("parallel", "parallel"),
        name="dsa_prep",
    )(z, z, z, z, z_tail, cos, sin, cos_t, sin_t, cos64, sin64, cos32_t, sin32_t,
      q_g.reshape(HEAD_DIM, 1), k_g.reshape(1, HEAD_DIM))


def _moba_prep(z, b_count, t, q_g, k_g):
    tm = MOBA_BLOCK
    row, in_specs, out_specs, out_shape = _prep_specs(b_count, t, tm, 0, INNER // KV_WIDTH,
                                                      INNER // KV_WIDTH + 1, KV_WIDTH)
    cos, sin = _rope_tables(t, HEAD_DIM, 1)
    cos_t, sin_t = _rope_tables_t(t, HEAD_DIM)
    tab = pl.BlockSpec((tm, LANE), lambda b, i: (i, 0))
    tab_t = pl.BlockSpec((HEAD_DIM // 2, tm), lambda b, i: (0, i))
    gain = pl.BlockSpec((1, HEAD_DIM), lambda b, i: (0, 0))
    gain_col = pl.BlockSpec((HEAD_DIM, 1), lambda b, i: (0, 0))
    in_specs += [tab, tab, tab_t, tab_t, gain_col, gain]
    out_specs += [pl.BlockSpec((1, N_KV_HEADS, t // tm, GROUP * HEAD_DIM), lambda b, i: (b, 0, 0, 0))]
    out_shape += [jax.ShapeDtypeStruct((b_count, N_KV_HEADS, t // tm, GROUP * HEAD_DIM), F32)]
    return pl.pallas_call(
        _moba_prep_kernel,
        grid=(b_count, t // tm),
        in_specs=in_specs, out_specs=out_specs, out_shape=out_shape,
        compiler_params=_cparams("parallel", "arbitrary"),
        name="moba_prep",
    )(z, z, z, cos, sin, cos_t, sin_t, q_g.reshape(HEAD_DIM, 1), k_g.reshape(1, HEAD_DIM))


def _fox_prep(z, z_tail, b_count, t, f_bias, q_g, k_g, tm=256):
    row, in_specs, out_specs, out_shape = _prep_specs(b_count, t, tm, 0, 1, 2, INNER)
    out_specs[1] = pl.BlockSpec((1, N_HEADS, tm, HEAD_DIM), lambda b, i: (b, 0, i, 0))
    out_shape[1] = jax.ShapeDtypeStruct((b_count, N_HEADS, t, HEAD_DIM), BF16)
    gain = pl.BlockSpec((1, HEAD_DIM), lambda b, i: (0, 0))
    fb = jnp.pad(f_bias.reshape(1, N_HEADS), ((0, 0), (0, LANE - N_HEADS)))
    in_specs += [
        pl.BlockSpec((tm, LANE), lambda b, i: (row(b, i), 0)),
        pl.BlockSpec((1, LANE), lambda b, i: (0, 0)),
        pl.BlockSpec((HEAD_DIM, 1), lambda b, i: (0, 0)), gain,
    ]
    out_specs += [
        pl.BlockSpec((1, tm, LANE), lambda b, i: (b, i, 0)),
        pl.BlockSpec((1, LANE, tm), lambda b, i: (b, 0, i)),
    ]
    out_shape += [
        jax.ShapeDtypeStruct((b_count, t, LANE), F32),
        jax.ShapeDtypeStruct((b_count, LANE, t), F32),
    ]
    return pl.pallas_call(
        _fox_prep_kernel,
        grid=(b_count, t // tm),
        in_specs=in_specs, out_specs=out_specs, out_shape=out_shape,
        scratch_shapes=[pltpu.VMEM((1, LANE), F32)],
        compiler_params=_cparams("parallel", "arbitrary"),
        name="fox_prep",
    )(z, z, z, z_tail, fb, q_g.reshape(HEAD_DIM, 1), k_g.reshape(1, HEAD_DIM))


def _dsa_select_kernel(kia_ref, kib_ref, qiT_ref, wT_ref, bias_ref, key_ref, ans_ref):
    qb = pl.program_id(1)
    n_kt = key_ref.shape[0] // TK
    w = wT_ref[0]
    s_iota = lax.broadcasted_iota(I32, (TK, TQ), 0)
    t_iota = lax.broadcasted_iota(I32, (TK, TQ), 1)

    for kt in range(n_kt):
        rows = slice(kt * TK, (kt + 1) * TK)

        @pl.when(kt <= qb)
        def _():
            ka = kia_ref[0, rows, :]
            kb = kib_ref[0, rows, :]
            score = jnp.zeros((TK, TQ), F32)
            for j in range(IDX_HEADS // 2):
                qp = qiT_ref[0, j * LANE:(j + 1) * LANE, :]
                ra = jnp.maximum(jnp.dot(ka, qp, preferred_element_type=F32), 0.0)
                rb = jnp.maximum(jnp.dot(kb, qp, preferred_element_type=F32), 0.0)
                wa = w[IDX_DIM + 2 * j:IDX_DIM + 2 * j + 1, :]
                wb = w[IDX_DIM + 2 * j + 1:IDX_DIM + 2 * j + 2, :]
                score = score + (ra * wa + rb * wb)
            bits = lax.bitcast_convert_type(score + 0.0, I32)
            key = bits ^ ((bits >> 31) & 0x7FFFFFFF)
            causal = (kt * TK + s_iota) <= (qb * TQ + t_iota)
            key_ref[rows, :] = jnp.where(causal, key, INT_MIN)

        @pl.when(kt > qb)
        def _():
            key_ref[rows, :] = jnp.full((TK, TQ), INT_MIN, I32)

    def variant(v):
        n = (v + 1) * TK
        kf = float(DSA_TOPK)

        def count_ge(cand):
            return jnp.sum((key_ref[:n, :] >= cand).astype(F32), axis=0, keepdims=True)

        def bit_step(i, ans):
            cand = ans | (jnp.int32(1) << (30 - i))
            return jnp.where(count_ge(cand) >= kf, cand, ans)

        sign = jnp.where(count_ge(jnp.zeros((1, TQ), I32)) >= kf, 0, INT_MIN).astype(I32)
        ans_ref[...] = lax.fori_loop(0, 31, bit_step, sign)

    _for_each_query_tile(qb, n_kt, variant)
    ans = ans_ref[...]

    for kt in range(n_kt):
        rows = slice(kt * TK, (kt + 1) * TK)
        k = key_ref[rows, :]
        sel = (k >= ans) & (k > INT_MIN)
        bias_ref[0, rows, :] = jnp.where(sel, 0.0, NEG)


def _dsa_select(kia, kib, qiT, wT):
    b_count, t, _ = kia.shape
    return pl.pallas_call(
        _dsa_select_kernel,
        grid=(b_count, t // TQ),
        in_specs=[pl.BlockSpec((1, t, LANE), lambda b, i: (b, 0, 0)),
                  pl.BlockSpec((1, t, LANE), lambda b, i: (b, 0, 0)),
                  pl.BlockSpec((1, IDX_HEADS * IDX_DIM, TQ), lambda b, i: (b, 0, i)),
                  pl.BlockSpec((1, LANE, TQ), lambda b, i: (b, 0, i))],
        out_specs=pl.BlockSpec((1, t, TQ), lambda b, i: (b, 0, i)),
        out_shape=jax.ShapeDtypeStruct((b_count, t, t), F32),
        scratch_shapes=[pltpu.VMEM((t, TQ), I32), pltpu.VMEM((1, TQ), I32)],
        compiler_params=_cparams("parallel", "parallel"),
        name="dsa_select",
    )(kia, kib, qiT, wT)


def _softmax_attend_heads(qs, k_alls, vt_alls, bias_fns, n_tiles, causal_last):
    heads = range(len(qs))
    s_alls = [jnp.dot(k_alls[i], qs[i], preferred_element_type=F32) for i in heads]
    tiles = [[] for _ in heads]
    diag = (n_tiles - 1) * TK
    for i in heads:
        for lo in range(0, n_tiles * TK, KEY_CHUNK):
            sj = bias_fns[i](lo, s_alls[i][lo:lo + KEY_CHUNK])
            if causal_last and lo >= diag:
                s_iota = lax.broadcasted_iota(I32, (KEY_CHUNK, TQ), 0) + (lo - diag)
                sj = jnp.where(s_iota <= lax.broadcasted_iota(I32, (KEY_CHUNK, TQ), 1), sj, NEG)
            tiles[i].append(sj)
    ms = [functools.reduce(jnp.maximum, [jnp.max(sj, axis=0, keepdims=True) for sj in tiles[i]]) for i in heads]
    outs = []
    for i in heads:
        ps = [jnp.exp2(sj - ms[i]) for sj in tiles[i]]
        l = functools.reduce(jnp.add, [jnp.sum(p, axis=0, keepdims=True) for p in ps])
        ps = [p.astype(BF16) for p in ps]
        p_all = jnp.concatenate(ps, axis=0)
        outs.append(jnp.dot(vt_alls[i], p_all, preferred_element_type=F32) * (1.0 / l))
    return outs


def _for_each_query_tile(qb, n_q, body):
    for v in range(n_q):
        pl.when(qb == v)(functools.partial(body, v))


def _pair_rows(i):
    base = pl.multiple_of(i * 2 * HEAD_DIM, 2 * HEAD_DIM)
    return [pl.ds(base, HEAD_DIM), pl.ds(base + HEAD_DIM, HEAD_DIM)]

def _gate_and_store(oT_ref, gate_ref, o_ref):
    for r in range(GROUP):
        sl = slice(r * HEAD_DIM, (r + 1) * HEAD_DIM)
        gate = gate_ref[:, sl]
        o_ref[:, sl] = (_silu(gate) * oT_ref[sl, :].T).astype(BF16)


def _dsa_attn_kernel(qT_ref, k_ref, vT_ref, bias_ref, gate_ref, o_ref, oT_ref):
    def variant(v):
        n = (v + 1) * TK

        def bias_fn(lo, s):
            return s + bias_ref[0, lo:lo + KEY_CHUNK, :]

        def head_pair(i, carry):
            rows = _pair_rows(i)
            outs = _softmax_attend_heads([qT_ref[0, r, :] for r in rows], [k_ref[0, :n, :]] * 2,
                                         [vT_ref[0, :, :n]] * 2, [bias_fn] * 2, v + 1, False)
            for r, o in zip(rows, outs):
                oT_ref[r, :] = o
            return carry

        lax.fori_loop(0, GROUP // 2, head_pair, 0)

    _for_each_query_tile(pl.program_id(2), k_ref.shape[1] // TQ, variant)
    _gate_and_store(oT_ref, gate_ref, o_ref)


def _moba_attn_kernel(qT_ref, k_ref, vT_ref, kmean_ref, gate_ref, o_ref, oT_ref, selb_ref):
    own = pl.program_id(2)
    n_blocks = kmean_ref.shape[2]

    km = kmean_ref[0, 0].astype(BF16)
    gs = jnp.dot(km, qT_ref[0], preferred_element_type=F32)
    blk = lax.broadcasted_iota(I32, (n_blocks, TQ), 0)
    past = blk < own
    gs = jnp.where(past, gs, NEG)
    sel = jnp.zeros((n_blocks, TQ), jnp.bool_)
    for _ in range(min(MOBA_TOPK, n_blocks - 1)):
        mx = jnp.max(gs, axis=0, keepdims=True)
        first = jnp.min(jnp.where(gs == mx, blk, n_blocks), axis=0, keepdims=True)
        pick = blk == first
        sel = sel | pick
        gs = jnp.where(pick, -jnp.inf, gs)
    selb_ref[...] = jnp.where(sel & past, 0.0, NEG)

    def variant(v):
        n = (v + 1) * TK

        def bias_fn(lo, s):
            j = lo // MOBA_BLOCK
            return s if j == v else s + selb_ref[j:j + 1, :]

        def head_pair(i, carry):
            rows = _pair_rows(i)
            outs = _softmax_attend_heads([qT_ref[0, r, :] for r in rows], [k_ref[0, :n, :]] * 2,
                                         [vT_ref[0, :, :n]] * 2, [bias_fn] * 2, v + 1, True)
            for r, o in zip(rows, outs):
                oT_ref[r, :] = o
            return carry

        lax.fori_loop(0, GROUP // 2, head_pair, 0)

    _for_each_query_tile(own, n_blocks, variant)
    _gate_and_store(oT_ref, gate_ref, o_ref)


def _fox_attn_kernel(qT_ref, k_ref, vT_ref, cum_ref, cumT_ref, gate_ref, o_ref, oT_ref):
    g = pl.program_id(1)

    def variant(v):
        n = (v + 1) * TK

        def make_bias_fn(h):
            cq = cumT_ref[0, pl.ds(h, 1), :]
            onehot = (lax.broadcasted_iota(I32, (KEY_CHUNK, LANE), 1) == h).astype(F32)

            def bias_fn(lo, s):
                ck = jnp.sum(cum_ref[0, lo:lo + KEY_CHUNK, :] * onehot, axis=1, keepdims=True)
                return s + (cq - ck)

            return bias_fn

        def head_pair(i, carry):
            rows = _pair_rows(i)
            heads = [2 * i, 2 * i + 1]
            outs = _softmax_attend_heads([qT_ref[0, r, :] for r in rows], [k_ref[0, r, :n, :] for r in heads],
                                         [vT_ref[0, r, :n] for r in rows],
                                         [make_bias_fn(g * GROUP + r) for r in heads], v + 1, True)
            for r, o in zip(rows, outs):
                oT_ref[r, :] = o
            return carry

        lax.fori_loop(0, GROUP // 2, head_pair, 0)

    _for_each_query_tile(pl.program_id(2), k_ref.shape[2] // TQ, variant)
    _gate_and_store(oT_ref, gate_ref, o_ref)


def _attn_scratch():
    return [pltpu.VMEM((GROUP * HEAD_DIM, TQ), F32)]


def _gqa_specs(t, gate_blk):
    nq = t // TQ
    gw = GROUP * HEAD_DIM
    in_specs = [
        pl.BlockSpec((1, gw, TQ), lambda b, g, i: (b, g, i)),
        pl.BlockSpec((1, t, HEAD_DIM), lambda b, g, i: (b, 0, g)),
        pl.BlockSpec((1, HEAD_DIM, t), lambda b, g, i: (b, g, 0)),
    ]
    gate_spec = pl.BlockSpec((TQ, gw), lambda b, g, i: (b * nq + i, gate_blk + g))
    out_spec = pl.BlockSpec((TQ, gw), lambda b, g, i: (b * nq + i, g))
    return nq, in_specs, gate_spec, out_spec


def _dsa_attn(qT, k, vT, bias, z):
    b_count, t, _ = k.shape
    gate_blk = (INNER + 2 * KV_WIDTH) // (GROUP * HEAD_DIM)
    nq, in_specs, gate_spec, out_spec = _gqa_specs(t, gate_blk)
    in_specs += [pl.BlockSpec((1, t, TQ), lambda b, g, i: (b, 0, i)), gate_spec]
    return pl.pallas_call(
        _dsa_attn_kernel,
        grid=(b_count, N_KV_HEADS, nq),
        in_specs=in_specs, out_specs=out_spec,
        out_shape=jax.ShapeDtypeStruct((b_count * t, INNER), BF16),
        scratch_shapes=_attn_scratch(),
        compiler_params=_cparams("parallel", "parallel", "parallel"),
        name="dsa_attn",
    )(qT, k, vT, bias, z)


def _moba_attn(qT, k, vT, kmean, z):
    b_count, t, _ = k.shape
    gate_blk = (INNER + 2 * KV_WIDTH) // (GROUP * HEAD_DIM)
    nq, in_specs, gate_spec, out_spec = _gqa_specs(t, gate_blk)
    n_blocks = t // MOBA_BLOCK
    in_specs += [pl.BlockSpec((1, 1, n_blocks, GROUP * HEAD_DIM), lambda b, g, i: (b, g, 0, 0)), gate_spec]
    return pl.pallas_call(
        _moba_attn_kernel,
        grid=(b_count, N_KV_HEADS, nq),
        in_specs=in_specs, out_specs=out_spec,
        out_shape=jax.ShapeDtypeStruct((b_count * t, INNER), BF16),
        scratch_shapes=_attn_scratch() + [pltpu.VMEM((n_blocks, TQ), F32)],
        compiler_params=_cparams("parallel", "parallel", "parallel"),
        name="moba_attn",
    )(qT, k, vT, kmean, z)


def _fox_attn(qT, k, vT, cum, cumT, z):
    b_count, _, t, _ = k.shape
    nq = t // TQ
    gw = GROUP * HEAD_DIM
    gate_blk = 3 * INNER // gw
    return pl.pallas_call(
        _fox_attn_kernel,
        grid=(b_count, N_HEADS // GROUP, nq),
        in_specs=[pl.BlockSpec((1, gw, TQ), lambda b, g, i: (b, g, i)),
                  pl.BlockSpec((1, GROUP, t, HEAD_DIM), lambda b, g, i: (b, g, 0, 0)),
                  pl.BlockSpec((1, gw, t), lambda b, g, i: (b, g, 0)),
                  pl.BlockSpec((1, t, LANE), lambda b, g, i: (b, 0, 0)),
                  pl.BlockSpec((1, LANE, TQ), lambda b, g, i: (b, 0, i)),
                  pl.BlockSpec((TQ, gw), lambda b, g, i: (b * nq + i, gate_blk + g))],
        out_specs=pl.BlockSpec((TQ, gw), lambda b, g, i: (b * nq + i, g)),
        out_shape=jax.ShapeDtypeStruct((b_count * t, INNER), BF16),
        scratch_shapes=_attn_scratch(),
        compiler_params=_cparams("parallel", "parallel", "parallel"),
        name="fox_attn",
    )(qT, k, vT, cum, cumT, z)


def _retention_kernel(q_ref, k_ref, v_ref, gate_ref, cos_ref, sin_ref, lg_ref, gn_ref, o_ref,
                      state_ref, dmask_ref):
    c = RET_CHUNK
    heads = range(RET_PAIR)
    lgs = [lg_ref[h][:, :1] for h in heads]

    @pl.when(pl.program_id(2) == 0)
    def _():
        state_ref[...] = jnp.zeros_like(state_ref)
        diff = (lax.broadcasted_iota(I32, (c, c), 0) - lax.broadcasted_iota(I32, (c, c), 1)).astype(F32)
        for h in heads:
            dmask_ref[h] = jnp.where(diff >= 0, jnp.exp(jnp.maximum(diff, 0.0) * lgs[h]), 0.0)

    cos = cos_ref[...]
    sin = sin_ref[...]
    half = RET_QK_DIM // 2

    def rope(ref, h):
        x1 = ref[:, h * RET_QK_DIM:h * RET_QK_DIM + half]
        x2 = ref[:, h * RET_QK_DIM + half:(h + 1) * RET_QK_DIM]
        return jnp.concatenate([x1 * cos - x2 * sin, x1 * sin + x2 * cos], axis=1)

    i_col = lax.broadcasted_iota(I32, (c, 1), 0).astype(F32)
    vsl = [slice(h * RET_V_DIM, (h + 1) * RET_V_DIM) for h in heads]
    qs = [rope(q_ref, h) for h in heads]
    ks = [rope(k_ref, h) * (RET_QK_DIM ** -0.5) for h in heads]
    inners = [lax.dot_general(qs[h].astype(BF16), ks[h].astype(BF16), (((1,), (1,)), ((), ())),
                              preferred_element_type=F32) for h in heads]
    vs = [v_ref[:, vsl[h]].astype(BF16) for h in heads]
    cross = [jnp.dot((qs[h] * jnp.exp((i_col + 1.0) * lgs[h])).astype(BF16), state_ref[h].astype(BF16),
                     preferred_element_type=F32) for h in heads]
    kdTs = [(ks[h] * jnp.exp((c - 1.0 - i_col) * lgs[h])).T.astype(BF16) for h in heads]
    os_ = [jnp.dot((inners[h] * dmask_ref[h]).astype(BF16), vs[h], preferred_element_type=F32) + cross[h]
           for h in heads]
    for h in heads:
        state_ref[h] = state_ref[h] * jnp.exp(c * lgs[h]) + jnp.dot(kdTs[h], vs[h], preferred_element_type=F32)
    for h in heads:
        o = os_[h]
        mu = jnp.mean(o, axis=-1, keepdims=True)
        var = jnp.mean(jnp.square(o - mu), axis=-1, keepdims=True)
        on = (o - mu) * lax.rsqrt(var + EPS) * gn_ref[:, vsl[h]]
        gate = gate_ref[:, vsl[h]]
        o_ref[:, vsl[h]] = (_silu(gate) * on).astype(BF16)


def _retention(z, b_count, t, gn_g):
    c = RET_CHUNK
    nc = t // c
    half = RET_QK_DIM // 2
    pos = jnp.arange(t, dtype=jnp.int32)
    inv = ROPE_THETA ** (-jnp.arange(0, RET_QK_DIM, 2, dtype=F32) / RET_QK_DIM)
    ang = pos.astype(F32)[:, None] * inv[None, :]
    cos, sin = jnp.cos(ang), jnp.sin(ang)
    log_gamma = jnp.log(1.0 - 2.0 ** (-5.0 - jnp.arange(RET_HEADS, dtype=F32)))
    lg = jnp.broadcast_to(log_gamma[:, None, None], (RET_HEADS, 1, LANE))
    qk_w = RET_HEADS * RET_QK_DIM
    qk_blk = RET_PAIR * RET_QK_DIM
    v_blk = RET_PAIR * RET_V_DIM
    row = lambda b, h, i: b * nc + i
    return pl.pallas_call(
        _retention_kernel,
        grid=(b_count, RET_HEADS // RET_PAIR, nc),
        in_specs=[pl.BlockSpec((c, qk_blk), lambda b, h, i: (row(b, h, i), h)),
                  pl.BlockSpec((c, qk_blk), lambda b, h, i: (row(b, h, i), qk_w // qk_blk + h)),
                  pl.BlockSpec((c, v_blk), lambda b, h, i: (row(b, h, i), 2 * qk_w // v_blk + h)),
                  pl.BlockSpec((c, v_blk), lambda b, h, i: (row(b, h, i), (2 * qk_w + RET_INNER) // v_blk + h)),
                  pl.BlockSpec((c, half), lambda b, h, i: (i, 0)),
                  pl.BlockSpec((c, half), lambda b, h, i: (i, 0)),
                  pl.BlockSpec((RET_PAIR, 1, LANE), lambda b, h, i: (h, 0, 0)),
                  pl.BlockSpec((1, v_blk), lambda b, h, i: (0, h))],
        out_specs=pl.BlockSpec((c, v_blk), lambda b, h, i: (row(b, h, i), h)),
        out_shape=jax.ShapeDtypeStruct((b_count * t, RET_INNER), BF16),
        scratch_shapes=[pltpu.VMEM((RET_PAIR, RET_QK_DIM, RET_V_DIM), F32), pltpu.VMEM((RET_PAIR, c, c), F32)],
        compiler_params=_cparams("parallel", "parallel", "arbitrary"),
        name="retention",
    )(z, z, z, z, cos, sin, lg, gn_g.reshape(1, RET_INNER))


def _tail_proj_kernel(a_ref, wt_ref, o_ref):
    o_ref[...] = lax.dot_general(a_ref[...], wt_ref[...], (((1,), (1,)), ((), ())), preferred_element_type=F32)


def _tail_proj(hn, w_in_t, n_main, name, tm=1024):
    m, k = hn.shape
    wt_tail = jnp.pad(w_in_t[n_main:, :], ((0, LANE - (w_in_t.shape[0] - n_main)), (0, 0))).astype(BF16)
    return pl.pallas_call(
        _tail_proj_kernel,
        grid=(m // tm,),
        in_specs=[pl.BlockSpec((tm, k), lambda i: (i, 0)),
                  pl.BlockSpec((LANE, k), lambda i: (0, 0))],
        out_specs=pl.BlockSpec((tm, LANE), lambda i: (i, 0)),
        out_shape=jax.ShapeDtypeStruct((m, LANE), F32),
        compiler_params=_cparams("parallel"),
        name=name,
    )(hn, wt_tail)


def _dsa_layer(h, hn, b_count, t, w_in, q_g, k_g, w_out, next_g):
    n_main = 2 * INNER + 2 * KV_WIDTH + IDX_HEADS * IDX_DIM
    w_in_t = w_in.T
    z = _in_proj_wt(hn, w_in_t, n_main, 1024, 1024, "dsa_in_proj")
    z_tail = _tail_proj(hn, w_in_t, n_main, "dsa_tail_proj")
    qT, k, vT, qiT, kia, kib, wT = _dsa_prep(z, z_tail, b_count, t, q_g, k_g)
    bias = _dsa_select(kia, kib, qiT, wT)
    gated = _dsa_attn(qT, k, vT, bias, z)
    return _out_proj(gated, w_out.astype(BF16), h, next_g, 512, "dsa_out_proj")


def _moba_layer(h, hn, b_count, t, w_in, q_g, k_g, w_out, next_g):
    z = _in_proj(hn, w_in, w_in.shape[1], 1024, 1024, "moba_in_proj")
    qT, k, vT, kmean = _moba_prep(z, b_count, t, q_g, k_g)
    gated = _moba_attn(qT, k, vT, kmean, z)
    return _out_proj(gated, w_out.astype(BF16), h, next_g, 512, "moba_out_proj")


def _ret_layer(h, hn, b_count, t, w_in, gn_g, w_out, next_g):
    z = _in_proj(hn, w_in, w_in.shape[1], 1024, 1024, "ret_in_proj")
    gated = _retention(z, b_count, t, gn_g)
    return _out_proj(gated, w_out.astype(BF16), h, next_g, 256, "ret_out_proj")


def _fox_layer(h, hn, b_count, t, w_in, f_bias, q_g, k_g, w_out, next_g):
    n_main = 4 * INNER
    w_in_t = w_in.T
    z = _in_proj_wt(hn, w_in_t, n_main, 1024, 1024, "fox_in_proj")
    z_tail = _tail_proj(hn, w_in_t, n_main, "fox_tail_proj")
    qT, k, vT, cum, cumT = _fox_prep(z, z_tail, b_count, t, f_bias, q_g, k_g)
    gated = _fox_attn(qT, k, vT, cum, cumT, z)
    return _out_proj(gated, w_out.astype(BF16), h, next_g, 512, "fox_out_proj")


def kernel(x, a_norm, a_w_in, a_q_norm, a_k_norm, a_w_out, b_norm, b_w_in, b_q_norm, b_k_norm, b_w_out,
           c_norm, c_w_in, c_gn, c_w_out, d_norm, d_w_in, d_f_bias, d_q_norm, d_k_norm, d_w_out):
    b_count, t, d = x.shape
    assert d == D_MODEL and t % TQ == 0 and t // 4 >= DSA_TOPK
    depth = 4
    norms = (a_norm, b_norm, c_norm, d_norm)
    h = x.reshape(b_count * t, d)
    hn = _rmsnorm(h, a_norm[0])
    for i in range(depth):
        m, j = i % 4, i // 4
        next_g = norms[(i + 1) % 4][(i + 1) // 4] if i + 1 < depth else None
        if m == 0:
            h, hn = _dsa_layer(h, hn, b_count, t, a_w_in[j], a_q_norm[j], a_k_norm[j], a_w_out[j], next_g)
        elif m == 1:
            h, hn = _moba_layer(h, hn, b_count, t, b_w_in[j], b_q_norm[j], b_k_norm[j], b_w_out[j], next_g)
        elif m == 2:
            h, hn = _ret_layer(h, hn, b_count, t, c_w_in[j], c_gn[j], c_w_out[j], next_g)
        else:
            h, hn = _fox_layer(h, hn, b_count, t, d_w_in[j], d_f_bias[j], d_q_norm[j], d_k_norm[j], d_w_out[j],
                               next_g)
    return h.reshape(b_count, t, d)
```

```python
import functools

import jax
import jax.numpy as jnp
import numpy as np
from jax import lax
from jax.experimental import pallas as pl
from jax.experimental.pallas import tpu as pltpu

F32 = jnp.float32
BF16 = jnp.bfloat16
I32 = jnp.int32

D_MODEL = 2048
HEAD_DIM = 128
N_HEADS = 16
N_KV_HEADS = 4
GROUP = N_HEADS // N_KV_HEADS
INNER = N_HEADS * HEAD_DIM
KV_WIDTH = N_KV_HEADS * HEAD_DIM
IDX_HEADS = 16
IDX_DIM = 64
DSA_TOPK = 256
MOBA_BLOCK = 256
MOBA_TOPK = 3
RET_HEADS = 8
RET_QK_DIM = 256
RET_V_DIM = 512
RET_INNER = RET_HEADS * RET_V_DIM
ROPE_THETA = 10000.0
EPS = 1e-6
NEG = -1e30
INT_MIN = -(2 ** 31)
LOG2E = 1.4426950408889634

LANE = 128
TQ = 256
TK = 256
KEY_CHUNK = 256
RET_CHUNK = 256
RET_PAIR = 2
VMEM_LIMIT = 56 * 1024 * 1024


def _cparams(*sem):
    return pltpu.CompilerParams(dimension_semantics=sem, vmem_limit_bytes=VMEM_LIMIT)


def _rmsnorm_kernel(x_ref, g_ref, o_ref):
    x = x_ref[...]
    ms = jnp.mean(x * x, axis=-1, keepdims=True)
    o_ref[...] = (x * lax.rsqrt(ms + EPS) * g_ref[...]).astype(o_ref.dtype)


def _rmsnorm(x, g, tm=512):
    m, d = x.shape
    return pl.pallas_call(
        _rmsnorm_kernel,
        grid=(m // tm,),
        in_specs=[pl.BlockSpec((tm, d), lambda i: (i, 0)),
                  pl.BlockSpec((1, d), lambda i: (0, 0))],
        out_specs=pl.BlockSpec((tm, d), lambda i: (i, 0)),
        out_shape=jax.ShapeDtypeStruct((m, d), BF16),
        compiler_params=_cparams("parallel"),
        name="rmsnorm",
    )(x, g.reshape(1, d))


def _in_proj_kernel(a_ref, w_ref, o_ref, wb_ref):
    @pl.when(pl.program_id(1) == 0)
    def _():
        wb_ref[...] = w_ref[...].astype(BF16)

    o_ref[...] = jnp.dot(a_ref[...], wb_ref[...], preferred_element_type=F32)


def _in_proj(a, w, n, tm, tn, name):
    m, k = a.shape
    return pl.pallas_call(
        _in_proj_kernel,
        grid=(n // tn, m // tm),
        in_specs=[pl.BlockSpec((tm, k), lambda j, i: (i, 0)),
                  pl.BlockSpec((k, tn), lambda j, i: (0, j))],
        out_specs=pl.BlockSpec((tm, tn), lambda j, i: (i, j)),
        out_shape=jax.ShapeDtypeStruct((m, n), F32),
        scratch_shapes=[pltpu.VMEM((k, tn), BF16)],
        compiler_params=_cparams("parallel", "arbitrary"),
        name=name,
    )(a, w)


def _in_proj_wt_kernel(a_ref, wt_ref, o_ref, wb_ref):
    @pl.when(pl.program_id(1) == 0)
    def _():
        wb_ref[...] = wt_ref[...].T.astype(BF16)

    o_ref[...] = jnp.dot(a_ref[...], wb_ref[...], preferred_element_type=F32)


def _in_proj_wt(a, wt, n, tm, tn, name):
    m, k = a.shape
    return pl.pallas_call(
        _in_proj_wt_kernel,
        grid=(n // tn, m // tm),
        in_specs=[pl.BlockSpec((tm, k), lambda j, i: (i, 0)),
                  pl.BlockSpec((tn, k), lambda j, i: (j, 0))],
        out_specs=pl.BlockSpec((tm, tn), lambda j, i: (i, j)),
        out_shape=jax.ShapeDtypeStruct((m, n), F32),
        scratch_shapes=[pltpu.VMEM((k, tn), BF16)],
        compiler_params=_cparams("parallel", "arbitrary"),
        name=name,
    )(a, wt)


def _out_proj_kernel(a_ref, w_ref, r_ref, o_ref):
    o_ref[...] = r_ref[...] + jnp.dot(a_ref[...], w_ref[...], preferred_element_type=F32)


def _out_proj_norm_kernel(a_ref, w_ref, r_ref, g_ref, o_ref, hn_ref):
    h = r_ref[...] + jnp.dot(a_ref[...], w_ref[...], preferred_element_type=F32)
    o_ref[...] = h
    ms = jnp.mean(h * h, axis=-1, keepdims=True)
    hn_ref[...] = (h * lax.rsqrt(ms + EPS) * g_ref[...]).astype(BF16)


def _out_proj(a, w, res, next_g, tm, name):
    m, k = a.shape
    n = w.shape[1]
    in_specs = [pl.BlockSpec((tm, k), lambda i: (i, 0)),
                pl.BlockSpec((k, n), lambda i: (0, 0)),
                pl.BlockSpec((tm, n), lambda i: (i, 0))]
    out_specs = [pl.BlockSpec((tm, n), lambda i: (i, 0))]
    out_shape = [jax.ShapeDtypeStruct((m, n), F32)]
    args = [a, w, res]
    if next_g is not None:
        in_specs.append(pl.BlockSpec((1, n), lambda i: (0, 0)))
        out_specs.append(pl.BlockSpec((tm, n), lambda i: (i, 0)))
        out_shape.append(jax.ShapeDtypeStruct((m, n), BF16))
        args.append(next_g.reshape(1, n))
    out = pl.pallas_call(
        _out_proj_kernel if next_g is None else _out_proj_norm_kernel,
        grid=(m // tm,),
        in_specs=in_specs, out_specs=out_specs, out_shape=out_shape,
        compiler_params=_cparams("parallel"),
        name=name,
    )(*args)
    return (out[0], out[1]) if next_g is not None else (out[0], None)


def _rope_tables(t, d, reps):
    pos = jnp.arange(t, dtype=jnp.int32)
    inv = ROPE_THETA ** (-jnp.arange(0, d, 2, dtype=F32) / d)
    ang = pos.astype(F32)[:, None] * inv[None, :]
    cos, sin = jnp.cos(ang), jnp.sin(ang)
    cos_t = jnp.tile(jnp.concatenate([cos, cos], axis=-1), (1, reps))
    sin_t = jnp.tile(jnp.concatenate([-sin, sin], axis=-1), (1, reps))
    return cos_t, sin_t


def _rope_tables_t(t, d):
    pos = jnp.arange(t, dtype=jnp.int32)
    inv = ROPE_THETA ** (-jnp.arange(0, d, 2, dtype=F32) / d)
    ang = pos.astype(F32)[:, None] * inv[None, :]
    return jnp.cos(ang).T, jnp.sin(ang).T


def _silu(x):
    hx = 0.5 * x
    return hx + hx * jnp.tanh(hx)


def _head_norm(x, g):
    ms = jnp.mean(x * x, axis=-1, keepdims=True)
    return x * lax.rsqrt(ms + EPS) * g


def _rope128(y, cos, sin):
    return y * cos + pltpu.roll(y, 64, 1) * sin


def _rope_rows(y, cos_t, sin_t):
    half = y.shape[0] // 2
    y1, y2 = y[:half], y[half:]
    return jnp.concatenate([y1 * cos_t - y2 * sin_t, y1 * sin_t + y2 * cos_t], axis=0)


def _qkv_prep_body(q_ref, k_ref, v_ref, cos_ref, sin_ref, cos_t_ref, sin_t_ref, qg_col_ref, kg_ref,
                   qT_ref, ko_ref, vT_ref, *, n_kv, rope):
    tm = q_ref.shape[0]
    kg = kg_ref[...]
    scale = HEAD_DIM ** -0.5 * LOG2E
    if rope:
        cos = cos_ref[...]
        sin = sin_ref[...]
        cos_t = cos_t_ref[...]
        sin_t = sin_t_ref[...]
    qg_t = jnp.broadcast_to(qg_col_ref[...], (HEAD_DIM, tm)) * scale
    for h in range(N_HEADS):
        sl = slice(h * HEAD_DIM, (h + 1) * HEAD_DIM)
        x = q_ref[:, sl].T
        ms = jnp.mean(x * x, axis=0, keepdims=True)
        y = x * lax.rsqrt(ms + EPS) * qg_t
        if rope:
            y = _rope_rows(y, cos_t, sin_t)
        qT_ref[0, sl, :] = y.astype(BF16)
    k_out = []
    for g in range(n_kv):
        sl = slice(g * HEAD_DIM, (g + 1) * HEAD_DIM)
        y = _head_norm(k_ref[:, sl], kg)
        if rope:
            y = _rope128(y, cos, sin)
        if ko_ref.ndim == 4:
            ko_ref[0, g] = y.astype(BF16)
        else:
            ko_ref[0, :, sl] = y.astype(BF16)
        vT_ref[0, sl, :] = v_ref[:, sl].T.astype(BF16)
        k_out.append(y)
    return k_out


def _dsa_prep_kernel(q_ref, k_ref, v_ref, qi_ref, tail_ref, cos_ref, sin_ref, cos_t_ref, sin_t_ref,
                     cos64_ref, sin64_ref, cos32_t_ref, sin32_t_ref,
                     qg_ref, kg_ref, qT_ref, ko_ref, vT_ref, qiT_ref, kia_ref, kib_ref, wT_ref):
    _qkv_prep_body(q_ref, k_ref, v_ref, cos_ref, sin_ref, cos_t_ref, sin_t_ref, qg_ref, kg_ref,
                   qT_ref, ko_ref, vT_ref, n_kv=N_KV_HEADS, rope=True)
    tm = tail_ref.shape[0]
    lane = lax.broadcasted_iota(I32, (tm, LANE), 1)
    first_half = (lane % IDX_DIM) < (IDX_DIM // 2)
    c64 = cos64_ref[...]
    s64 = sin64_ref[...]

    def rope64(x):
        rot = jnp.where(first_half, pltpu.roll(x, LANE - IDX_DIM // 2, 1), pltpu.roll(x, IDX_DIM // 2, 1))
        return x * c64 + rot * s64

    c32_t = cos32_t_ref[...]
    s32_t = sin32_t_ref[...]
    for j in range(IDX_HEADS * IDX_DIM // LANE):
        sl = slice(j * LANE, (j + 1) * LANE)
        x = qi_ref[:, sl].T * (IDX_DIM ** -0.5)
        y = jnp.concatenate([_rope_rows(x[:IDX_DIM], c32_t, s32_t), _rope_rows(x[IDX_DIM:], c32_t, s32_t)], axis=0)
        qiT_ref[0, sl, :] = y.astype(BF16)
    tail = tail_ref[...]
    ka = jnp.where(lane < IDX_DIM, rope64(tail), 0.0)
    kia_ref[0] = ka.astype(BF16)
    kib_ref[0] = pltpu.roll(ka, IDX_DIM, 1).astype(BF16)
    wT_ref[0] = (tail * (IDX_HEADS ** -0.5)).T


def _moba_prep_kernel(q_ref, k_ref, v_ref, cos_ref, sin_ref, cos_t_ref, sin_t_ref, qg_ref, kg_ref,
                      qT_ref, ko_ref, vT_ref, kmean_ref):
    k_out = _qkv_prep_body(q_ref, k_ref, v_ref, cos_ref, sin_ref, cos_t_ref, sin_t_ref, qg_ref, kg_ref,
                           qT_ref, ko_ref, vT_ref, n_kv=N_KV_HEADS, rope=True)
    for g in range(N_KV_HEADS):
        km = jnp.mean(k_out[g], axis=0, keepdims=True)
        kmean_ref[0, g, pl.ds(pl.program_id(1), 1), :] = jnp.concatenate([km] * GROUP, axis=1)


def _fox_prep_kernel(q_ref, k_ref, v_ref, f_ref, fb_ref, qg_ref, kg_ref,
                     qT_ref, ko_ref, vT_ref, cum_ref, cumT_ref, carry_ref):
    _qkv_prep_body(q_ref, k_ref, v_ref, None, None, None, None, qg_ref, kg_ref, qT_ref, ko_ref, vT_ref,
                   n_kv=N_HEADS, rope=False)

    @pl.when(pl.program_id(1) == 0)
    def _():
        carry_ref[...] = jnp.zeros_like(carry_ref)

    tm = f_ref.shape[0]
    x = f_ref[...] + fb_ref[...]
    lf = jnp.minimum(x, 0.0) - jnp.log(1.0 + jnp.exp(-jnp.abs(x)))
    hi = lf.astype(BF16)
    r1 = lf - hi.astype(F32)
    lo = r1.astype(BF16)
    lo2 = (r1 - lo.astype(F32)).astype(BF16)
    row = lax.broadcasted_iota(I32, (tm, tm), 0)
    col = lax.broadcasted_iota(I32, (tm, tm), 1)
    tri = (col <= row).astype(BF16)
    parts = jnp.dot(tri, jnp.concatenate([hi, lo, lo2], axis=1), preferred_element_type=F32)
    cum = parts[:, :LANE] + parts[:, LANE:2 * LANE] + parts[:, 2 * LANE:] + carry_ref[...]
    carry_ref[...] = cum[tm - 1:tm, :]
    cum2 = cum * LOG2E
    cum_ref[0] = cum2
    cumT_ref[0] = cum2.T


def _prep_specs(b_count, t, tm, z_q_blk, z_k_blk, z_v_blk, kv_width):
    nt = t // tm
    row = lambda b, i: b * nt + i
    in_specs = [
        pl.BlockSpec((tm, INNER), lambda b, i: (row(b, i), z_q_blk)),
        pl.BlockSpec((tm, kv_width), lambda b, i: (row(b, i), z_k_blk)),
        pl.BlockSpec((tm, kv_width), lambda b, i: (row(b, i), z_v_blk)),
    ]
    out_specs = [
        pl.BlockSpec((1, INNER, tm), lambda b, i: (b, 0, i)),
        pl.BlockSpec((1, tm, kv_width), lambda b, i: (b, i, 0)),
        pl.BlockSpec((1, kv_width, tm), lambda b, i: (b, 0, i)),
    ]
    out_shape = [
        jax.ShapeDtypeStruct((b_count, INNER, t), BF16),
        jax.ShapeDtypeStruct((b_count, t, kv_width), BF16),
        jax.ShapeDtypeStruct((b_count, kv_width, t), BF16),
    ]
    return row, in_specs, out_specs, out_shape


def _dsa_prep(z, z_tail, b_count, t, q_g, k_g, tm=256):
    row, in_specs, out_specs, out_shape = _prep_specs(b_count, t, tm, 0, INNER // KV_WIDTH,
                                                      INNER // KV_WIDTH + 1, KV_WIDTH)
    qi_w = IDX_HEADS * IDX_DIM
    qi_off = 2 * INNER + 2 * KV_WIDTH
    cos, sin = _rope_tables(t, HEAD_DIM, 1)
    cos64, sin64 = _rope_tables(t, IDX_DIM, LANE // IDX_DIM)
    cos_t, sin_t = _rope_tables_t(t, HEAD_DIM)
    cos32_t, sin32_t = _rope_tables_t(t, IDX_DIM)
    tab = pl.BlockSpec((tm, LANE), lambda b, i: (i, 0))
    tab_t = pl.BlockSpec((HEAD_DIM // 2, tm), lambda b, i: (0, i))
    tab32_t = pl.BlockSpec((IDX_DIM // 2, tm), lambda b, i: (0, i))
    gain = pl.BlockSpec((1, HEAD_DIM), lambda b, i: (0, 0))
    gain_col = pl.BlockSpec((HEAD_DIM, 1), lambda b, i: (0, 0))
    in_specs += [
        pl.BlockSpec((tm, qi_w), lambda b, i: (row(b, i), qi_off // qi_w)),
        pl.BlockSpec((tm, LANE), lambda b, i: (row(b, i), 0)),
        tab, tab, tab_t, tab_t, tab, tab, tab32_t, tab32_t, gain_col, gain,
    ]
    out_specs += [
        pl.BlockSpec((1, qi_w, tm), lambda b, i: (b, 0, i)),
        pl.BlockSpec((1, tm, LANE), lambda b, i: (b, i, 0)),
        pl.BlockSpec((1, tm, LANE), lambda b, i: (b, i, 0)),
        pl.BlockSpec((1, LANE, tm), lambda b, i: (b, 0, i)),
    ]
    out_shape += [
        jax.ShapeDtypeStruct((b_count, qi_w, t), BF16),
        jax.ShapeDtypeStruct((b_count, t, LANE), BF16),
        jax.ShapeDtypeStruct((b_count, t, LANE), BF16),
        jax.ShapeDtypeStruct((b_count, LANE, t), F32),
    ]
    return pl.pallas_call(
        _dsa_prep_kernel,
        grid=(b_count, t // tm),
        in_specs=in_specs, out_specs=out_specs, out_shape=out_shape,
        compiler_params=_cparams("parallel", "parallel"),
        name="dsa_prep",
    )(z, z, z, z, z_tail, cos, sin, cos_t, sin_t, cos64, sin64, cos32_t, sin32_t,
      q_g.reshape(HEAD_DIM, 1), k_g.reshape(1, HEAD_DIM))


def _moba_prep(z, b_count, t, q_g, k_g):
    tm = MOBA_BLOCK
    row, in_specs, out_specs, out_shape = _prep_specs(b_count, t, tm, 0, INNER // KV_WIDTH,
                                                      INNER // KV_WIDTH + 1, KV_WIDTH)
    cos, sin = _rope_tables(t, HEAD_DIM, 1)
    cos_t, sin_t = _rope_tables_t(t, HEAD_DIM)
    tab = pl.BlockSpec((tm, LANE), lambda b, i: (i, 0))
    tab_t = pl.BlockSpec((HEAD_DIM // 2, tm), lambda b, i: (0, i))
    gain = pl.BlockSpec((1, HEAD_DIM), lambda b, i: (0, 0))
    gain_col = pl.BlockSpec((HEAD_DIM, 1), lambda b, i: (0, 0))
    in_specs += [tab, tab, tab_t, tab_t, gain_col, gain]
    out_specs += [pl.BlockSpec((1, N_KV_HEADS, t // tm, GROUP * HEAD_DIM), lambda b, i: (b, 0, 0, 0))]
    out_shape += [jax.ShapeDtypeStruct((b_count, N_KV_HEADS, t // tm, GROUP * HEAD_DIM), F32)]
    return pl.pallas_call(
        _moba_prep_kernel,
        grid=(b_count, t // tm),
        in_specs=in_specs, out_specs=out_specs, out_shape=out_shape,
        compiler_params=_cparams("parallel", "arbitrary"),
        name="moba_prep",
    )(z, z, z, cos, sin, cos_t, sin_t, q_g.reshape(HEAD_DIM, 1), k_g.reshape(1, HEAD_DIM))


def _fox_prep(z, z_tail, b_count, t, f_bias, q_g, k_g, tm=256):
    row, in_specs, out_specs, out_shape = _prep_specs(b_count, t, tm, 0, 1, 2, INNER)
    out_specs[1] = pl.BlockSpec((1, N_HEADS, tm, HEAD_DIM), lambda b, i: (b, 0, i, 0))
    out_shape[1] = jax.ShapeDtypeStruct((b_count, N_HEADS, t, HEAD_DIM), BF16)
    gain = pl.BlockSpec((1, HEAD_DIM), lambda b, i: (0, 0))
    fb = jnp.pad(f_bias.reshape(1, N_HEADS), ((0, 0), (0, LANE - N_HEADS)))
    in_specs += [
        pl.BlockSpec((tm, LANE), lambda b, i: (row(b, i), 0)),
        pl.BlockSpec((1, LANE), lambda b, i: (0, 0)),
        pl.BlockSpec((HEAD_DIM, 1), lambda b, i: (0, 0)), gain,
    ]
    out_specs += [
        pl.BlockSpec((1, tm, LANE), lambda b, i: (b, i, 0)),
        pl.BlockSpec((1, LANE, tm), lambda b, i: (b, 0, i)),
    ]
    out_shape += [
        jax.ShapeDtypeStruct((b_count, t, LANE), F32),
        jax.ShapeDtypeStruct((b_count, LANE, t), F32),
    ]
    return pl.pallas_call(
        _fox_prep_kernel,
        grid=(b_count, t // tm),
        in_specs=in_specs, out_specs=out_specs, out_shape=out_shape,
        scratch_shapes=[pltpu.VMEM((1, LANE), F32)],
        compiler_params=_cparams("parallel", "arbitrary"),
        name="fox_prep",
    )(z, z, z, z_tail, fb, q_g.reshape(HEAD_DIM, 1), k_g.reshape(1, HEAD_DIM))


def _dsa_select_kernel(kia_ref, kib_ref, qiT_ref, wT_ref, bias_ref, key_ref, ans_ref, n_ge_ref):
    qb = pl.program_id(1)
    n_kt = key_ref.shape[0] // TK
    w = wT_ref[0]
    s_iota = lax.broadcasted_iota(I32, (TK, TQ), 0)
    t_iota = lax.broadcasted_iota(I32, (TK, TQ), 1)

    for kt in range(n_kt):
        rows = slice(kt * TK, (kt + 1) * TK)

        @pl.when(kt <= qb)
        def _():
            ka = kia_ref[0, rows, :]
            kb = kib_ref[0, rows, :]
            score = jnp.zeros((TK, TQ), F32)
            for j in range(IDX_HEADS // 2):
                qp = qiT_ref[0, j * LANE:(j + 1) * LANE, :]
                ra = jnp.maximum(jnp.dot(ka, qp, preferred_element_type=F32), 0.0)
                rb = jnp.maximum(jnp.dot(kb, qp, preferred_element_type=F32), 0.0)
                wa = w[IDX_DIM + 2 * j:IDX_DIM + 2 * j + 1, :]
                wb = w[IDX_DIM + 2 * j + 1:IDX_DIM + 2 * j + 2, :]
                score = score + (ra * wa + rb * wb)
            bits = lax.bitcast_convert_type(score + 0.0, I32)
            key = bits ^ ((bits >> 31) & 0x7FFFFFFF)
            causal = (kt * TK + s_iota) <= (qb * TQ + t_iota)
            key_ref[rows, :] = jnp.where(causal, key, INT_MIN)

        @pl.when(kt > qb)
        def _():
            key_ref[rows, :] = jnp.full((TK, TQ), INT_MIN, I32)

    def variant(v):
        n = (v + 1) * TK
        kf = float(DSA_TOPK)

        def count_ge(cand):
            return jnp.sum((key_ref[:n, :] >= cand).astype(F32), axis=0, keepdims=True)

        def bit_step(i, ans):
            cand = ans | (jnp.int32(1) << (30 - i))
            return jnp.where(count_ge(cand) >= kf, cand, ans)

        sign = jnp.where(count_ge(jnp.zeros((1, TQ), I32)) >= kf, 0, INT_MIN).astype(I32)
        ans = lax.fori_loop(0, 31, bit_step, sign)
        ans_ref[...] = ans
        n_ge_ref[...] = count_ge(ans)

    _for_each_query_tile(qb, n_kt, variant)
    ans = ans_ref[...]

    for kt in range(n_kt):
        rows = slice(kt * TK, (kt + 1) * TK)
        k = key_ref[rows, :]
        sel = (k >= ans) & (k > INT_MIN)
        bias_ref[0, rows, :] = jnp.where(sel, 0.0, NEG)

    @pl.when(jnp.max(n_ge_ref[...]) > float(DSA_TOPK))
    def _():
        n_gt = jnp.zeros((1, TQ), F32)
        for kt in range(n_kt):
            n_gt = n_gt + jnp.sum((key_ref[kt * TK:(kt + 1) * TK, :] > ans).astype(F32), axis=0, keepdims=True)
        n_tied_kept = float(DSA_TOPK) - n_gt
        tri = (lax.broadcasted_iota(I32, (TK, TK), 1) <= lax.broadcasted_iota(I32, (TK, TK), 0)).astype(BF16)
        seen = jnp.zeros((1, TQ), F32)
        for kt in range(n_kt):
            rows = slice(kt * TK, (kt + 1) * TK)
            k = key_ref[rows, :]
            tied = (k == ans) & (k > INT_MIN)
            rank = jnp.dot(tri, tied.astype(F32).astype(BF16), preferred_element_type=F32) + seen
            keep = (k > ans) | (tied & (rank <= n_tied_kept))
            bias_ref[0, rows, :] = jnp.where(keep, 0.0, NEG)
            seen = rank[TK - 1:TK, :]


def _dsa_select(kia, kib, qiT, wT):
    b_count, t, _ = kia.shape
    return pl.pallas_call(
        _dsa_select_kernel,
        grid=(b_count, t // TQ),
        in_specs=[pl.BlockSpec((1, t, LANE), lambda b, i: (b, 0, 0)),
                  pl.BlockSpec((1, t, LANE), lambda b, i: (b, 0, 0)),
                  pl.BlockSpec((1, IDX_HEADS * IDX_DIM, TQ), lambda b, i: (b, 0, i)),
                  pl.BlockSpec((1, LANE, TQ), lambda b, i: (b, 0, i))],
        out_specs=pl.BlockSpec((1, t, TQ), lambda b, i: (b, 0, i)),
        out_shape=jax.ShapeDtypeStruct((b_count, t, t), F32),
        scratch_shapes=[pltpu.VMEM((t, TQ), I32), pltpu.VMEM((1, TQ), I32), pltpu.VMEM((1, TQ), F32)],
        compiler_params=_cparams("parallel", "parallel"),
        name="dsa_select",
    )(kia, kib, qiT, wT)


def _softmax_attend_heads(qs, k_alls, vt_alls, bias_fns, n_tiles, causal_last):
    heads = range(len(qs))
    s_alls = [jnp.dot(k_alls[i], qs[i], preferred_element_type=F32) for i in heads]
    tiles = [[] for _ in heads]
    diag = (n_tiles - 1) * TK
    for i in heads:
        for lo in range(0, n_tiles * TK, KEY_CHUNK):
            sj = bias_fns[i](lo, s_alls[i][lo:lo + KEY_CHUNK])
            if causal_last and lo >= diag:
                s_iota = lax.broadcasted_iota(I32, (KEY_CHUNK, TQ), 0) + (lo - diag)
                sj = jnp.where(s_iota <= lax.broadcasted_iota(I32, (KEY_CHUNK, TQ), 1), sj, NEG)
            tiles[i].append(sj)
    ms = [functools.reduce(jnp.maximum, [jnp.max(sj, axis=0, keepdims=True) for sj in tiles[i]]) for i in heads]
    outs = []
    for i in heads:
        ps = [jnp.exp2(sj - ms[i]) for sj in tiles[i]]
        l = functools.reduce(jnp.add, [jnp.sum(p, axis=0, keepdims=True) for p in ps])
        ps = [p.astype(BF16) for p in ps]
        p_all = jnp.concatenate(ps, axis=0)
        outs.append(jnp.dot(vt_alls[i], p_all, preferred_element_type=F32) * (1.0 / l))
    return outs


def _for_each_query_tile(qb, n_q, body):
    for v in range(n_q):
        pl.when(qb == v)(functools.partial(body, v))


def _pair_rows(i):
    base = pl.multiple_of(i * 2 * HEAD_DIM, 2 * HEAD_DIM)
    return [pl.ds(base, HEAD_DIM), pl.ds(base + HEAD_DIM, HEAD_DIM)]

def _gate_and_store(oT_ref, gate_ref, o_ref):
    for r in range(GROUP):
        sl = slice(r * HEAD_DIM, (r + 1) * HEAD_DIM)
        gate = gate_ref[:, sl]
        o_ref[:, sl] = (_silu(gate) * oT_ref[sl, :].T).astype(BF16)


def _dsa_attn_kernel(qT_ref, k_ref, vT_ref, bias_ref, gate_ref, o_ref, oT_ref):
    def variant(v):
        n = (v + 1) * TK

        def bias_fn(lo, s):
            return s + bias_ref[0, lo:lo + KEY_CHUNK, :]

        def head_pair(i, carry):
            rows = _pair_rows(i)
            outs = _softmax_attend_heads([qT_ref[0, r, :] for r in rows], [k_ref[0, :n, :]] * 2,
                                         [vT_ref[0, :, :n]] * 2, [bias_fn] * 2, v + 1, False)
            for r, o in zip(rows, outs):
                oT_ref[r, :] = o
            return carry

        lax.fori_loop(0, GROUP // 2, head_pair, 0)

    _for_each_query_tile(pl.program_id(2), k_ref.shape[1] // TQ, variant)
    _gate_and_store(oT_ref, gate_ref, o_ref)


def _moba_attn_kernel(qT_ref, k_ref, vT_ref, kmean_ref, gate_ref, o_ref, oT_ref, selb_ref):
    own = pl.program_id(2)
    n_blocks = kmean_ref.shape[2]

    km = kmean_ref[0, 0].astype(BF16)
    gs = jnp.dot(km, qT_ref[0], preferred_element_type=F32)
    blk = lax.broadcasted_iota(I32, (n_blocks, TQ), 0)
    past = blk < own
    gs = jnp.where(past, gs, NEG)
    sel = jnp.zeros((n_blocks, TQ), jnp.bool_)
    for _ in range(min(MOBA_TOPK, n_blocks - 1)):
        mx = jnp.max(gs, axis=0, keepdims=True)
        first = jnp.min(jnp.where(gs == mx, blk, n_blocks), axis=0, keepdims=True)
        pick = blk == first
        sel = sel | pick
        gs = jnp.where(pick, -jnp.inf, gs)
    selb_ref[...] = jnp.where(sel & past, 0.0, NEG)

    def variant(v):
        n = (v + 1) * TK

        def bias_fn(lo, s):
            j = lo // MOBA_BLOCK
            return s if j == v else s + selb_ref[j:j + 1, :]

        def head_pair(i, carry):
            rows = _pair_rows(i)
            outs = _softmax_attend_heads([qT_ref[0, r, :] for r in rows], [k_ref[0, :n, :]] * 2,
                                         [vT_ref[0, :, :n]] * 2, [bias_fn] * 2, v + 1, True)
            for r, o in zip(rows, outs):
                oT_ref[r, :] = o
            return carry

        lax.fori_loop(0, GROUP // 2, head_pair, 0)

    _for_each_query_tile(own, n_blocks, variant)
    _gate_and_store(oT_ref, gate_ref, o_ref)


def _fox_attn_kernel(qT_ref, k_ref, vT_ref, cum_ref, cumT_ref, gate_ref, o_ref, oT_ref):
    g = pl.program_id(1)

    def variant(v):
        n = (v + 1) * TK

        def make_bias_fn(h):
            cq = cumT_ref[0, pl.ds(h, 1), :]
            onehot = (lax.broadcasted_iota(I32, (KEY_CHUNK, LANE), 1) == h).astype(F32)

            def bias_fn(lo, s):
                ck = jnp.sum(cum_ref[0, lo:lo + KEY_CHUNK, :] * onehot, axis=1, keepdims=True)
                return s + (cq - ck)

            return bias_fn

        def head_pair(i, carry):
            rows = _pair_rows(i)
            heads = [2 * i, 2 * i + 1]
            outs = _softmax_attend_heads([qT_ref[0, r, :] for r in rows], [k_ref[0, r, :n, :] for r in heads],
                                         [vT_ref[0, r, :n] for r in rows],
                                         [make_bias_fn(g * GROUP + r) for r in heads], v + 1, True)
            for r, o in zip(rows, outs):
                oT_ref[r, :] = o
            return carry

        lax.fori_loop(0, GROUP // 2, head_pair, 0)

    _for_each_query_tile(pl.program_id(2), k_ref.shape[2] // TQ, variant)
    _gate_and_store(oT_ref, gate_ref, o_ref)


def _attn_scratch():
    return [pltpu.VMEM((GROUP * HEAD_DIM, TQ), F32)]


def _gqa_specs(t, gate_blk):
    nq = t // TQ
    gw = GROUP * HEAD_DIM
    in_specs = [
        pl.BlockSpec((1, gw, TQ), lambda b, g, i: (b, g, i)),
        pl.BlockSpec((1, t, HEAD_DIM), lambda b, g, i: (b, 0, g)),
        pl.BlockSpec((1, HEAD_DIM, t), lambda b, g, i: (b, g, 0)),
    ]
    gate_spec = pl.BlockSpec((TQ, gw), lambda b, g, i: (b * nq + i, gate_blk + g))
    out_spec = pl.BlockSpec((TQ, gw), lambda b, g, i: (b * nq + i, g))
    return nq, in_specs, gate_spec, out_spec


def _dsa_attn(qT, k, vT, bias, z):
    b_count, t, _ = k.shape
    gate_blk = (INNER + 2 * KV_WIDTH) // (GROUP * HEAD_DIM)
    nq, in_specs, gate_spec, out_spec = _gqa_specs(t, gate_blk)
    in_specs += [pl.BlockSpec((1, t, TQ), lambda b, g, i: (b, 0, i)), gate_spec]
    return pl.pallas_call(
        _dsa_attn_kernel,
        grid=(b_count, N_KV_HEADS, nq),
        in_specs=in_specs, out_specs=out_spec,
        out_shape=jax.ShapeDtypeStruct((b_count * t, INNER), BF16),
        scratch_shapes=_attn_scratch(),
        compiler_params=_cparams("parallel", "parallel", "parallel"),
        name="dsa_attn",
    )(qT, k, vT, bias, z)


def _moba_attn(qT, k, vT, kmean, z):
    b_count, t, _ = k.shape
    gate_blk = (INNER + 2 * KV_WIDTH) // (GROUP * HEAD_DIM)
    nq, in_specs, gate_spec, out_spec = _gqa_specs(t, gate_blk)
    n_blocks = t // MOBA_BLOCK
    in_specs += [pl.BlockSpec((1, 1, n_blocks, GROUP * HEAD_DIM), lambda b, g, i: (b, g, 0, 0)), gate_spec]
    return pl.pallas_call(
        _moba_attn_kernel,
        grid=(b_count, N_KV_HEADS, nq),
        in_specs=in_specs, out_specs=out_spec,
        out_shape=jax.ShapeDtypeStruct((b_count * t, INNER), BF16),
        scratch_shapes=_attn_scratch() + [pltpu.VMEM((n_blocks, TQ), F32)],
        compiler_params=_cparams("parallel", "parallel", "parallel"),
        name="moba_attn",
    )(qT, k, vT, kmean, z)


def _fox_attn(qT, k, vT, cum, cumT, z):
    b_count, _, t, _ = k.shape
    nq = t // TQ
    gw = GROUP * HEAD_DIM
    gate_blk = 3 * INNER // gw
    return pl.pallas_call(
        _fox_attn_kernel,
        grid=(b_count, N_HEADS // GROUP, nq),
        in_specs=[pl.BlockSpec((1, gw, TQ), lambda b, g, i: (b, g, i)),
                  pl.BlockSpec((1, GROUP, t, HEAD_DIM), lambda b, g, i: (b, g, 0, 0)),
                  pl.BlockSpec((1, gw, t), lambda b, g, i: (b, g, 0)),
                  pl.BlockSpec((1, t, LANE), lambda b, g, i: (b, 0, 0)),
                  pl.BlockSpec((1, LANE, TQ), lambda b, g, i: (b, 0, i)),
                  pl.BlockSpec((TQ, gw), lambda b, g, i: (b * nq + i, gate_blk + g))],
        out_specs=pl.BlockSpec((TQ, gw), lambda b, g, i: (b * nq + i, g)),
        out_shape=jax.ShapeDtypeStruct((b_count * t, INNER), BF16),
        scratch_shapes=_attn_scratch(),
        compiler_params=_cparams("parallel", "parallel", "parallel"),
        name="fox_attn",
    )(qT, k, vT, cum, cumT, z)


def _retention_kernel(q_ref, k_ref, v_ref, gate_ref, cos_ref, sin_ref, lg_ref, gn_ref, o_ref,
                      state_ref, dmask_ref):
    c = RET_CHUNK
    heads = range(RET_PAIR)
    lgs = [lg_ref[h][:, :1] for h in heads]

    @pl.when(pl.program_id(2) == 0)
    def _():
        state_ref[...] = jnp.zeros_like(state_ref)
        diff = (lax.broadcasted_iota(I32, (c, c), 0) - lax.broadcasted_iota(I32, (c, c), 1)).astype(F32)
        for h in heads:
            dmask_ref[h] = jnp.where(diff >= 0, jnp.exp(jnp.maximum(diff, 0.0) * lgs[h]), 0.0)

    cos = cos_ref[...]
    sin = sin_ref[...]
    half = RET_QK_DIM // 2

    def rope(ref, h):
        x1 = ref[:, h * RET_QK_DIM:h * RET_QK_DIM + half]
        x2 = ref[:, h * RET_QK_DIM + half:(h + 1) * RET_QK_DIM]
        return jnp.concatenate([x1 * cos - x2 * sin, x1 * sin + x2 * cos], axis=1)

    i_col = lax.broadcasted_iota(I32, (c, 1), 0).astype(F32)
    vsl = [slice(h * RET_V_DIM, (h + 1) * RET_V_DIM) for h in heads]
    qs = [rope(q_ref, h) for h in heads]
    ks = [rope(k_ref, h) * (RET_QK_DIM ** -0.5) for h in heads]
    inners = [lax.dot_general(qs[h].astype(BF16), ks[h].astype(BF16), (((1,), (1,)), ((), ())),
                              preferred_element_type=F32) for h in heads]
    vs = [v_ref[:, vsl[h]].astype(BF16) for h in heads]
    cross = [jnp.dot((qs[h] * jnp.exp((i_col + 1.0) * lgs[h])).astype(BF16), state_ref[h].astype(BF16),
                     preferred_element_type=F32) for h in heads]
    kdTs = [(ks[h] * jnp.exp((c - 1.0 - i_col) * lgs[h])).T.astype(BF16) for h in heads]
    os_ = [jnp.dot((inners[h] * dmask_ref[h]).astype(BF16), vs[h], preferred_element_type=F32) + cross[h]
           for h in heads]
    for h in heads:
        state_ref[h] = state_ref[h] * jnp.exp(c * lgs[h]) + jnp.dot(kdTs[h], vs[h], preferred_element_type=F32)
    for h in heads:
        o = os_[h]
        mu = jnp.mean(o, axis=-1, keepdims=True)
        var = jnp.mean(jnp.square(o - mu), axis=-1, keepdims=True)
        on = (o - mu) * lax.rsqrt(var + EPS) * gn_ref[:, vsl[h]]
        gate = gate_ref[:, vsl[h]]
        o_ref[:, vsl[h]] = (_silu(gate) * on).astype(BF16)


def _retention(z, b_count, t, gn_g):
    c = RET_CHUNK
    nc = t // c
    half = RET_QK_DIM // 2
    pos = jnp.arange(t, dtype=jnp.int32)
    inv = ROPE_THETA ** (-jnp.arange(0, RET_QK_DIM, 2, dtype=F32) / RET_QK_DIM)
    ang = pos.astype(F32)[:, None] * inv[None, :]
    cos, sin = jnp.cos(ang), jnp.sin(ang)
    log_gamma = jnp.log(1.0 - 2.0 ** (-5.0 - jnp.arange(RET_HEADS, dtype=F32)))
    lg = jnp.broadcast_to(log_gamma[:, None, None], (RET_HEADS, 1, LANE))
    qk_w = RET_HEADS * RET_QK_DIM
    qk_blk = RET_PAIR * RET_QK_DIM
    v_blk = RET_PAIR * RET_V_DIM
    row = lambda b, h, i: b * nc + i
    return pl.pallas_call(
        _retention_kernel,
        grid=(b_count, RET_HEADS // RET_PAIR, nc),
        in_specs=[pl.BlockSpec((c, qk_blk), lambda b, h, i: (row(b, h, i), h)),
                  pl.BlockSpec((c, qk_blk), lambda b, h, i: (row(b, h, i), qk_w // qk_blk + h)),
                  pl.BlockSpec((c, v_blk), lambda b, h, i: (row(b, h, i), 2 * qk_w // v_blk + h)),
                  pl.BlockSpec((c, v_blk), lambda b, h, i: (row(b, h, i), (2 * qk_w + RET_INNER) // v_blk + h)),
                  pl.BlockSpec((c, half), lambda b, h, i: (i, 0)),
                  pl.BlockSpec((c, half), lambda b, h, i: (i, 0)),
                  pl.BlockSpec((RET_PAIR, 1, LANE), lambda b, h, i: (h, 0, 0)),
                  pl.BlockSpec((1, v_blk), lambda b, h, i: (0, h))],
        out_specs=pl.BlockSpec((c, v_blk), lambda b, h, i: (row(b, h, i), h)),
        out_shape=jax.ShapeDtypeStruct((b_count * t, RET_INNER), BF16),
        scratch_shapes=[pltpu.VMEM((RET_PAIR, RET_QK_DIM, RET_V_DIM), F32), pltpu.VMEM((RET_PAIR, c, c), F32)],
        compiler_params=_cparams("parallel", "parallel", "arbitrary"),
        name="retention",
    )(z, z, z, z, cos, sin, lg, gn_g.reshape(1, RET_INNER))


def _tail_proj_kernel(a_ref, wt_ref, o_ref):
    o_ref[...] = lax.dot_general(a_ref[...], wt_ref[...], (((1,), (1,)), ((), ())), preferred_element_type=F32)


def _tail_proj(hn, w_in_t, n_main, name, tm=1024):
    m, k = hn.shape
    wt_tail = jnp.pad(w_in_t[n_main:, :], ((0, LANE - (w_in_t.shape[0] - n_main)), (0, 0))).astype(BF16)
    return pl.pallas_call(
        _tail_proj_kernel,
        grid=(m // tm,),
        in_specs=[pl.BlockSpec((tm, k), lambda i: (i, 0)),
                  pl.BlockSpec((LANE, k), lambda i: (0, 0))],
        out_specs=pl.BlockSpec((tm, LANE), lambda i: (i, 0)),
        out_shape=jax.ShapeDtypeStruct((m, LANE), F32),
        compiler_params=_cparams("parallel"),
        name=name,
    )(hn, wt_tail)


def _dsa_layer(h, hn, b_count, t, w_in, q_g, k_g, w_out, next_g):
    n_main = 2 * INNER + 2 * KV_WIDTH + IDX_HEADS * IDX_DIM
    w_in_t = w_in.T
    z = _in_proj_wt(hn, w_in_t, n_main, 1024, 1024, "dsa_in_proj")
    z_tail = _tail_proj(hn, w_in_t, n_main, "dsa_tail_proj")
    qT, k, vT, qiT, kia, kib, wT = _dsa_prep(z, z_tail, b_count, t, q_g, k_g)
    bias = _dsa_select(kia, kib, qiT, wT)
    gated = _dsa_attn(qT, k, vT, bias, z)
    return _out_proj(gated, w_out.astype(BF16), h, next_g, 512, "dsa_out_proj")


def _moba_layer(h, hn, b_count, t, w_in, q_g, k_g, w_out, next_g):
    z = _in_proj(hn, w_in, w_in.shape[1], 1024, 1024, "moba_in_proj")
    qT, k, vT, kmean = _moba_prep(z, b_count, t, q_g, k_g)
    gated = _moba_attn(qT, k, vT, kmean, z)
    return _out_proj(gated, w_out.astype(BF16), h, next_g, 512, "moba_out_proj")


def _ret_layer(h, hn, b_count, t, w_in, gn_g, w_out, next_g):
    z = _in_proj(hn, w_in, w_in.shape[1], 1024, 1024, "ret_in_proj")
    gated = _retention(z, b_count, t, gn_g)
    return _out_proj(gated, w_out.astype(BF16), h, next_g, 256, "ret_out_proj")


def _fox_layer(h, hn, b_count, t, w_in, f_bias, q_g, k_g, w_out, next_g):
    n_main = 4 * INNER
    w_in_t = w_in.T
    z = _in_proj_wt(hn, w_in_t, n_main, 1024, 1024, "fox_in_proj")
    z_tail = _tail_proj(hn, w_in_t, n_main, "fox_tail_proj")
    qT, k, vT, cum, cumT = _fox_prep(z, z_tail, b_count, t, f_bias, q_g, k_g)
    gated = _fox_attn(qT, k, vT, cum, cumT, z)
    return _out_proj(gated, w_out.astype(BF16), h, next_g, 512, "fox_out_proj")


def kernel(x, a_norm, a_w_in, a_q_norm, a_k_norm, a_w_out, b_norm, b_w_in, b_q_norm, b_k_norm, b_w_out,
           c_norm, c_w_in, c_gn, c_w_out, d_norm, d_w_in, d_f_bias, d_q_norm, d_k_norm, d_w_out):
    b_count, t, d = x.shape
    assert d == D_MODEL and t % TQ == 0 and t // 4 >= DSA_TOPK
    depth = 4
    norms = (a_norm, b_norm, c_norm, d_norm)
    h = x.reshape(b_count * t, d)
    hn = _rmsnorm(h, a_norm[0])
    for i in range(depth):
        m, j = i % 4, i // 4
        next_g = norms[(i + 1) % 4][(i + 1) // 4] if i + 1 < depth else None
        if m == 0:
            h, hn = _dsa_layer(h, hn, b_count, t, a_w_in[j], a_q_norm[j], a_k_norm[j], a_w_out[j], next_g)
        elif m == 1:
            h, hn = _moba_layer(h, hn, b_count, t, b_w_in[j], b_q_norm[j], b_k_norm[j], b_w_out[j], next_g)
        elif m == 2:
            h, hn = _ret_layer(h, hn, b_count, t, c_w_in[j], c_gn[j], c_w_out[j], next_g)
        else:
            h, hn = _fox_layer(h, hn, b_count, t, d_w_in[j], d_f_bias[j], d_q_norm[j], d_k_norm[j], d_w_out[j],
                               next_g)
    return h.reshape(b_count, t, d)
```

```python
import functools

import jax
import jax.numpy as jnp
from jax import lax
from jax.experimental import pallas as pl
from jax.experimental.pallas import tpu as pltpu

F32 = jnp.float32
BF16 = jnp.bfloat16
I32 = jnp.int32

D_MODEL = 2048
HEAD_DIM = 128
N_HEADS = 16
N_KV_HEADS = 4
GROUP = N_HEADS // N_KV_HEADS
INNER = N_HEADS * HEAD_DIM
KV_WIDTH = N_KV_HEADS * HEAD_DIM
IDX_HEADS = 16
IDX_DIM = 64
DSA_TOPK = 256
MOBA_BLOCK = 256
MOBA_TOPK = 3
RET_HEADS = 8
RET_QK_DIM = 256
RET_V_DIM = 512
RET_INNER = RET_HEADS * RET_V_DIM
ROPE_THETA = 10000.0
EPS = 1e-6
NEG = -1e30
INT_MIN = -(2 ** 31)
LOG2E = 1.4426950408889634

LANE = 128
TQ = 256
TK = 256
KEY_CHUNK = 256
RET_CHUNK = 256
RET_PAIR = 2
VMEM_LIMIT = 56 * 1024 * 1024


def _cparams(*sem):
    return pltpu.CompilerParams(dimension_semantics=sem, vmem_limit_bytes=VMEM_LIMIT)


def _rmsnorm_kernel(x_ref, g_ref, o_ref):
    x = x_ref[...]
    ms = jnp.mean(x * x, axis=-1, keepdims=True)
    o_ref[...] = (x * lax.rsqrt(ms + EPS) * g_ref[...]).astype(o_ref.dtype)


def _rmsnorm(x, g, tm=512):
    m, d = x.shape
    return pl.pallas_call(
        _rmsnorm_kernel,
        grid=(m // tm,),
        in_specs=[pl.BlockSpec((tm, d), lambda i: (i, 0)),
                  pl.BlockSpec((1, d), lambda i: (0, 0))],
        out_specs=pl.BlockSpec((tm, d), lambda i: (i, 0)),
        out_shape=jax.ShapeDtypeStruct((m, d), BF16),
        compiler_params=_cparams("parallel"),
        name="rmsnorm",
    )(x, g.reshape(1, d))


def _in_proj_kernel(a_ref, w_ref, o_ref, wb_ref):
    @pl.when(pl.program_id(1) == 0)
    def _():
        wb_ref[...] = w_ref[...].astype(BF16)

    o_ref[...] = jnp.dot(a_ref[...], wb_ref[...], preferred_element_type=F32)


def _in_proj(a, w, n, tm, tn, name):
    m, k = a.shape
    return pl.pallas_call(
        _in_proj_kernel,
        grid=(n // tn, m // tm),
        in_specs=[pl.BlockSpec((tm, k), lambda j, i: (i, 0)),
                  pl.BlockSpec((k, tn), lambda j, i: (0, j))],
        out_specs=pl.BlockSpec((tm, tn), lambda j, i: (i, j)),
        out_shape=jax.ShapeDtypeStruct((m, n), F32),
        scratch_shapes=[pltpu.VMEM((k, tn), BF16)],
        compiler_params=_cparams("parallel", "arbitrary"),
        name=name,
    )(a, w)


def _in_proj_wt_kernel(a_ref, wt_ref, o_ref, wb_ref):
    @pl.when(pl.program_id(1) == 0)
    def _():
        wb_ref[...] = wt_ref[...].T.astype(BF16)

    o_ref[...] = jnp.dot(a_ref[...], wb_ref[...], preferred_element_type=F32)


def _in_proj_wt(a, wt, n, tm, tn, name):
    m, k = a.shape
    return pl.pallas_call(
        _in_proj_wt_kernel,
        grid=(n // tn, m // tm),
        in_specs=[pl.BlockSpec((tm, k), lambda j, i: (i, 0)),
                  pl.BlockSpec((tn, k), lambda j, i: (j, 0))],
        out_specs=pl.BlockSpec((tm, tn), lambda j, i: (i, j)),
        out_shape=jax.ShapeDtypeStruct((m, n), F32),
        scratch_shapes=[pltpu.VMEM((k, tn), BF16)],
        compiler_params=_cparams("parallel", "arbitrary"),
        name=name,
    )(a, wt)


def _out_proj_kernel(a_ref, w_ref, r_ref, o_ref):
    o_ref[...] = r_ref[...] + jnp.dot(a_ref[...], w_ref[...], preferred_element_type=F32)


def _out_proj_norm_kernel(a_ref, w_ref, r_ref, g_ref, o_ref, hn_ref):
    h = r_ref[...] + jnp.dot(a_ref[...], w_ref[...], preferred_element_type=F32)
    o_ref[...] = h
    ms = jnp.mean(h * h, axis=-1, keepdims=True)
    hn_ref[...] = (h * lax.rsqrt(ms + EPS) * g_ref[...]).astype(BF16)


def _out_proj(a, w, res, next_g, tm, name):
    m, k = a.shape
    n = w.shape[1]
    in_specs = [pl.BlockSpec((tm, k), lambda i: (i, 0)),
                pl.BlockSpec((k, n), lambda i: (0, 0)),
                pl.BlockSpec((tm, n), lambda i: (i, 0))]
    out_specs = [pl.BlockSpec((tm, n), lambda i: (i, 0))]
    out_shape = [jax.ShapeDtypeStruct((m, n), F32)]
    args = [a, w, res]
    if next_g is not None:
        in_specs.append(pl.BlockSpec((1, n), lambda i: (0, 0)))
        out_specs.append(pl.BlockSpec((tm, n), lambda i: (i, 0)))
        out_shape.append(jax.ShapeDtypeStruct((m, n), BF16))
        args.append(next_g.reshape(1, n))
    out = pl.pallas_call(
        _out_proj_kernel if next_g is None else _out_proj_norm_kernel,
        grid=(m // tm,),
        in_specs=in_specs, out_specs=out_specs, out_shape=out_shape,
        compiler_params=_cparams("parallel"),
        name=name,
    )(*args)
    return (out[0], out[1]) if next_g is not None else (out[0], None)


def _rope_tables(t, d, reps):
    pos = jnp.arange(t, dtype=jnp.int32)
    inv = ROPE_THETA ** (-jnp.arange(0, d, 2, dtype=F32) / d)
    ang = pos.astype(F32)[:, None] * inv[None, :]
    cos, sin = jnp.cos(ang), jnp.sin(ang)
    cos_t = jnp.tile(jnp.concatenate([cos, cos], axis=-1), (1, reps))
    sin_t = jnp.tile(jnp.concatenate([-sin, sin], axis=-1), (1, reps))
    return cos_t, sin_t


def _rope_tables_t(t, d):
    pos = jnp.arange(t, dtype=jnp.int32)
    inv = ROPE_THETA ** (-jnp.arange(0, d, 2, dtype=F32) / d)
    ang = pos.astype(F32)[:, None] * inv[None, :]
    return jnp.cos(ang).T, jnp.sin(ang).T


def _silu(x):
    hx = 0.5 * x
    return hx + hx * jnp.tanh(hx)


def _head_norm(x, g):
    ms = jnp.mean(x * x, axis=-1, keepdims=True)
    return x * lax.rsqrt(ms + EPS) * g


def _rope128(y, cos, sin):
    return y * cos + pltpu.roll(y, 64, 1) * sin


def _rope_rows(y, cos_t, sin_t):
    half = y.shape[0] // 2
    y1, y2 = y[:half], y[half:]
    return jnp.concatenate([y1 * cos_t - y2 * sin_t, y1 * sin_t + y2 * cos_t], axis=0)


def _qkv_prep_body(q_ref, k_ref, v_ref, cos_ref, sin_ref, cos_t_ref, sin_t_ref, qg_col_ref, kg_ref,
                   qT_ref, ko_ref, vT_ref, *, n_kv, rope):
    tm = q_ref.shape[0]
    kg = kg_ref[...]
    scale = HEAD_DIM ** -0.5 * LOG2E
    if rope:
        cos = cos_ref[...]
        sin = sin_ref[...]
        cos_t = cos_t_ref[...]
        sin_t = sin_t_ref[...]
    qg_t = jnp.broadcast_to(qg_col_ref[...], (HEAD_DIM, tm)) * scale
    for h in range(N_HEADS):
        sl = slice(h * HEAD_DIM, (h + 1) * HEAD_DIM)
        x = q_ref[:, sl].T
        ms = jnp.mean(x * x, axis=0, keepdims=True)
        y = x * lax.rsqrt(ms + EPS) * qg_t
        if rope:
            y = _rope_rows(y, cos_t, sin_t)
        qT_ref[0, sl, :] = y.astype(BF16)
    k_out = []
    for g in range(n_kv):
        sl = slice(g * HEAD_DIM, (g + 1) * HEAD_DIM)
        y = _head_norm(k_ref[:, sl], kg)
        if rope:
            y = _rope128(y, cos, sin)
        if ko_ref.ndim == 4:
            ko_ref[0, g] = y.astype(BF16)
        else:
            ko_ref[0, :, sl] = y.astype(BF16)
        vT_ref[0, sl, :] = v_ref[:, sl].T.astype(BF16)
        k_out.append(y)
    return k_out


def _dsa_prep_kernel(q_ref, k_ref, v_ref, qi_ref, tail_ref, cos_ref, sin_ref, cos_t_ref, sin_t_ref,
                     cos64_ref, sin64_ref, cos32_t_ref, sin32_t_ref,
                     qg_ref, kg_ref, qT_ref, ko_ref, vT_ref, qiT_ref, kia_ref, kib_ref, wT_ref):
    _qkv_prep_body(q_ref, k_ref, v_ref, cos_ref, sin_ref, cos_t_ref, sin_t_ref, qg_ref, kg_ref,
                   qT_ref, ko_ref, vT_ref, n_kv=N_KV_HEADS, rope=True)
    tm = tail_ref.shape[0]
    lane = lax.broadcasted_iota(I32, (tm, LANE), 1)
    first_half = (lane % IDX_DIM) < (IDX_DIM // 2)
    c64 = cos64_ref[...]
    s64 = sin64_ref[...]

    def rope64(x):
        rot = jnp.where(first_half, pltpu.roll(x, LANE - IDX_DIM // 2, 1), pltpu.roll(x, IDX_DIM // 2, 1))
        return x * c64 + rot * s64

    c32_t = cos32_t_ref[...]
    s32_t = sin32_t_ref[...]
    for j in range(IDX_HEADS * IDX_DIM // LANE):
        sl = slice(j * LANE, (j + 1) * LANE)
        x = qi_ref[:, sl].T * (IDX_DIM ** -0.5)
        y = jnp.concatenate([_rope_rows(x[:IDX_DIM], c32_t, s32_t), _rope_rows(x[IDX_DIM:], c32_t, s32_t)], axis=0)
        qiT_ref[0, sl, :] = y.astype(BF16)
    tail = tail_ref[...]
    ka = jnp.where(lane < IDX_DIM, rope64(tail), 0.0)
    kia_ref[0] = ka.astype(BF16)
    kib_ref[0] = pltpu.roll(ka, IDX_DIM, 1).astype(BF16)
    wT_ref[0] = (tail * (IDX_HEADS ** -0.5)).T


def _moba_prep_kernel(q_ref, k_ref, v_ref, cos_ref, sin_ref, cos_t_ref, sin_t_ref, qg_ref, kg_ref,
                      qT_ref, ko_ref, vT_ref, kmean_ref):
    k_out = _qkv_prep_body(q_ref, k_ref, v_ref, cos_ref, sin_ref, cos_t_ref, sin_t_ref, qg_ref, kg_ref,
                           qT_ref, ko_ref, vT_ref, n_kv=N_KV_HEADS, rope=True)
    for g in range(N_KV_HEADS):
        km = jnp.mean(k_out[g], axis=0, keepdims=True)
        kmean_ref[0, g, pl.ds(pl.program_id(1), 1), :] = jnp.concatenate([km] * GROUP, axis=1)


def _fox_prep_kernel(q_ref, k_ref, v_ref, f_ref, fb_ref, qg_ref, kg_ref,
                     qT_ref, ko_ref, vT_ref, cum_ref, cumT_ref, carry_ref):
    _qkv_prep_body(q_ref, k_ref, v_ref, None, None, None, None, qg_ref, kg_ref, qT_ref, ko_ref, vT_ref,
                   n_kv=N_HEADS, rope=False)

    @pl.when(pl.program_id(1) == 0)
    def _():
        carry_ref[...] = jnp.zeros_like(carry_ref)

    tm = f_ref.shape[0]
    x = f_ref[...] + fb_ref[...]
    lf = jnp.minimum(x, 0.0) - jnp.log(1.0 + jnp.exp(-jnp.abs(x)))
    hi = lf.astype(BF16)
    r1 = lf - hi.astype(F32)
    lo = r1.astype(BF16)
    lo2 = (r1 - lo.astype(F32)).astype(BF16)
    row = lax.broadcasted_iota(I32, (tm, tm), 0)
    col = lax.broadcasted_iota(I32, (tm, tm), 1)
    tri = (col <= row).astype(BF16)
    parts = jnp.dot(tri, jnp.concatenate([hi, lo, lo2], axis=1), preferred_element_type=F32)
    cum = parts[:, :LANE] + parts[:, LANE:2 * LANE] + parts[:, 2 * LANE:] + carry_ref[...]
    carry_ref[...] = cum[tm - 1:tm, :]
    cum2 = cum * LOG2E
    cum_ref[0] = cum2
    cumT_ref[0] = cum2.T


def _prep_specs(b_count, t, tm, z_q_blk, z_k_blk, z_v_blk, kv_width):
    nt = t // tm
    row = lambda b, i: b * nt + i
    in_specs = [
        pl.BlockSpec((tm, INNER), lambda b, i: (row(b, i), z_q_blk)),
        pl.BlockSpec((tm, kv_width), lambda b, i: (row(b, i), z_k_blk)),
        pl.BlockSpec((tm, kv_width), lambda b, i: (row(b, i), z_v_blk)),
    ]
    out_specs = [
        pl.BlockSpec((1, INNER, tm), lambda b, i: (b, 0, i)),
        pl.BlockSpec((1, tm, kv_width), lambda b, i: (b, i, 0)),
        pl.BlockSpec((1, kv_width, tm), lambda b, i: (b, 0, i)),
    ]
    out_shape = [
        jax.ShapeDtypeStruct((b_count, INNER, t), BF16),
        jax.ShapeDtypeStruct((b_count, t, kv_width), BF16),
        jax.ShapeDtypeStruct((b_count, kv_width, t), BF16),
    ]
    return row, in_specs, out_specs, out_shape


def _dsa_prep(z, z_tail, b_count, t, q_g, k_g, tm=256):
    row, in_specs, out_specs, out_shape = _prep_specs(b_count, t, tm, 0, INNER // KV_WIDTH,
                                                      INNER // KV_WIDTH + 1, KV_WIDTH)
    qi_w = IDX_HEADS * IDX_DIM
    qi_off = 2 * INNER + 2 * KV_WIDTH
    cos, sin = _rope_tables(t, HEAD_DIM, 1)
    cos64, sin64 = _rope_tables(t, IDX_DIM, LANE // IDX_DIM)
    cos_t, sin_t = _rope_tables_t(t, HEAD_DIM)
    cos32_t, sin32_t = _rope_tables_t(t, IDX_DIM)
    tab = pl.BlockSpec((tm, LANE), lambda b, i: (i, 0))
    tab_t = pl.BlockSpec((HEAD_DIM // 2, tm), lambda b, i: (0, i))
    tab32_t = pl.BlockSpec((IDX_DIM // 2, tm), lambda b, i: (0, i))
    gain = pl.BlockSpec((1, HEAD_DIM), lambda b, i: (0, 0))
    gain_col = pl.BlockSpec((HEAD_DIM, 1), lambda b, i: (0, 0))
    in_specs += [
        pl.BlockSpec((tm, qi_w), lambda b, i: (row(b, i), qi_off // qi_w)),
        pl.BlockSpec((tm, LANE), lambda b, i: (row(b, i), 0)),
        tab, tab, tab_t, tab_t, tab, tab, tab32_t, tab32_t, gain_col, gain,
    ]
    out_specs += [
        pl.BlockSpec((1, qi_w, tm), lambda b, i: (b, 0, i)),
        pl.BlockSpec((1, tm, LANE), lambda b, i: (b, i, 0)),
        pl.BlockSpec((1, tm, LANE), lambda b, i: (b, i, 0)),
        pl.BlockSpec((1, LANE, tm), lambda b, i: (b, 0, i)),
    ]
    out_shape += [
        jax.ShapeDtypeStruct((b_count, qi_w, t), BF16),
        jax.ShapeDtypeStruct((b_count, t, LANE), BF16),
        jax.ShapeDtypeStruct((b_count, t, LANE), BF16),
        jax.ShapeDtypeStruct((b_count, LANE, t), F32),
    ]
    return pl.pallas_call(
        _dsa_prep_kernel,
        grid=(b_count, t // tm),
        in_specs=in_specs, out_specs=out_specs, out_shape=out_shape,
        compiler_params=_cparams("parallel", "parallel"),
        name="dsa_prep",
    )(z, z, z, z, z_tail, cos, sin, cos_t, sin_t, cos64, sin64, cos32_t, sin32_t,
      q_g.reshape(HEAD_DIM, 1), k_g.reshape(1, HEAD_DIM))


def _moba_prep(z, b_count, t, q_g, k_g):
    tm = MOBA_BLOCK
    row, in_specs, out_specs, out_shape = _prep_specs(b_count, t, tm, 0, INNER // KV_WIDTH,
                                                      INNER // KV_WIDTH + 1, KV_WIDTH)
    cos, sin = _rope_tables(t, HEAD_DIM, 1)
    cos_t, sin_t = _rope_tables_t(t, HEAD_DIM)
    tab = pl.BlockSpec((tm, LANE), lambda b, i: (i, 0))
    tab_t = pl.BlockSpec((HEAD_DIM // 2, tm), lambda b, i: (0, i))
    gain = pl.BlockSpec((1, HEAD_DIM), lambda b, i: (0, 0))
    gain_col = pl.BlockSpec((HEAD_DIM, 1), lambda b, i: (0, 0))
    in_specs += [tab, tab, tab_t, tab_t, gain_col, gain]
    out_specs += [pl.BlockSpec((1, N_KV_HEADS, t // tm, GROUP * HEAD_DIM), lambda b, i: (b, 0, 0, 0))]
    out_shape += [jax.ShapeDtypeStruct((b_count, N_KV_HEADS, t // tm, GROUP * HEAD_DIM), F32)]
    return pl.pallas_call(
        _moba_prep_kernel,
        grid=(b_count, t // tm),
        in_specs=in_specs, out_specs=out_specs, out_shape=out_shape,
        compiler_params=_cparams("parallel", "arbitrary"),
        name="moba_prep",
    )(z, z, z, cos, sin, cos_t, sin_t, q_g.reshape(HEAD_DIM, 1), k_g.reshape(1, HEAD_DIM))


def _fox_prep(z, z_tail, b_count, t, f_bias, q_g, k_g, tm=256):
    row, in_specs, out_specs, out_shape = _prep_specs(b_count, t, tm, 0, 1, 2, INNER)
    out_specs[1] = pl.BlockSpec((1, N_HEADS, tm, HEAD_DIM), lambda b, i: (b, 0, i, 0))
    out_shape[1] = jax.ShapeDtypeStruct((b_count, N_HEADS, t, HEAD_DIM), BF16)
    gain = pl.BlockSpec((1, HEAD_DIM), lambda b, i: (0, 0))
    fb = jnp.pad(f_bias.reshape(1, N_HEADS), ((0, 0), (0, LANE - N_HEADS)))
    in_specs += [
        pl.BlockSpec((tm, LANE), lambda b, i: (row(b, i), 0)),
        pl.BlockSpec((1, LANE), lambda b, i: (0, 0)),
        pl.BlockSpec((HEAD_DIM, 1), lambda b, i: (0, 0)), gain,
    ]
    out_specs += [
        pl.BlockSpec((1, tm, LANE), lambda b, i: (b, i, 0)),
        pl.BlockSpec((1, LANE, tm), lambda b, i: (b, 0, i)),
    ]
    out_shape += [
        jax.ShapeDtypeStruct((b_count, t, LANE), F32),
        jax.ShapeDtypeStruct((b_count, LANE, t), F32),
    ]
    return pl.pallas_call(
        _fox_prep_kernel,
        grid=(b_count, t // tm),
        in_specs=in_specs, out_specs=out_specs, out_shape=out_shape,
        scratch_shapes=[pltpu.VMEM((1, LANE), F32)],
        compiler_params=_cparams("parallel", "arbitrary"),
        name="fox_prep",
    )(z, z, z, z_tail, fb, q_g.reshape(HEAD_DIM, 1), k_g.reshape(1, HEAD_DIM))


def _dsa_select_kernel(kia_ref, kib_ref, qiT_ref, wT_ref, bias_ref, key_ref, ans_ref, n_ge_ref):
    qb = pl.program_id(1)
    n_kt = key_ref.shape[0] // TK
    w = wT_ref[0]
    s_iota = lax.broadcasted_iota(I32, (TK, TQ), 0)
    t_iota = lax.broadcasted_iota(I32, (TK, TQ), 1)

    for kt in range(n_kt):
        rows = slice(kt * TK, (kt + 1) * TK)

        @pl.when(kt <= qb)
        def _():
            ka = kia_ref[0, rows, :]
            kb = kib_ref[0, rows, :]
            score = jnp.zeros((TK, TQ), F32)
            for j in range(IDX_HEADS // 2):
                qp = qiT_ref[0, j * LANE:(j + 1) * LANE, :]
                ra = jnp.maximum(jnp.dot(ka, qp, preferred_element_type=F32), 0.0)
                rb = jnp.maximum(jnp.dot(kb, qp, preferred_element_type=F32), 0.0)
                wa = w[IDX_DIM + 2 * j:IDX_DIM + 2 * j + 1, :]
                wb = w[IDX_DIM + 2 * j + 1:IDX_DIM + 2 * j + 2, :]
                score = score + (ra * wa + rb * wb)
            bits = lax.bitcast_convert_type(score + 0.0, I32)
            key = bits ^ ((bits >> 31) & 0x7FFFFFFF)
            causal = (kt * TK + s_iota) <= (qb * TQ + t_iota)
            key_ref[rows, :] = jnp.where(causal, key, INT_MIN)

        @pl.when(kt > qb)
        def _():
            key_ref[rows, :] = jnp.full((TK, TQ), INT_MIN, I32)

    def variant(v):
        n = (v + 1) * TK
        kf = float(DSA_TOPK)

        def count_ge(cand):
            return jnp.sum((key_ref[:n, :] >= cand).astype(F32), axis=0, keepdims=True)

        def bit_step(i, ans):
            cand = ans | (jnp.int32(1) << (30 - i))
            return jnp.where(count_ge(cand) >= kf, cand, ans)

        sign = jnp.where(count_ge(jnp.zeros((1, TQ), I32)) >= kf, 0, INT_MIN).astype(I32)
        ans = lax.fori_loop(0, 31, bit_step, sign)
        ans_ref[...] = ans
        n_ge_ref[...] = count_ge(ans)

    _for_each_query_tile(qb, n_kt, variant)
    ans = ans_ref[...]

    for kt in range(n_kt):
        rows = slice(kt * TK, (kt + 1) * TK)
        k = key_ref[rows, :]
        sel = (k >= ans) & (k > INT_MIN)
        bias_ref[0, rows, :] = jnp.where(sel, 0.0, NEG)

    @pl.when(jnp.max(n_ge_ref[...]) > float(DSA_TOPK))
    def _():
        n_gt = jnp.zeros((1, TQ), F32)
        for kt in range(n_kt):
            n_gt = n_gt + jnp.sum((key_ref[kt * TK:(kt + 1) * TK, :] > ans).astype(F32), axis=0, keepdims=True)
        n_tied_kept = float(DSA_TOPK) - n_gt
        tri = (lax.broadcasted_iota(I32, (TK, TK), 1) <= lax.broadcasted_iota(I32, (TK, TK), 0)).astype(BF16)
        seen = jnp.zeros((1, TQ), F32)
        for kt in range(n_kt):
            rows = slice(kt * TK, (kt + 1) * TK)
            k = key_ref[rows, :]
            tied = (k == ans) & (k > INT_MIN)
            rank = jnp.dot(tri, tied.astype(F32).astype(BF16), preferred_element_type=F32) + seen
            keep = (k > ans) | (tied & (rank <= n_tied_kept))
            bias_ref[0, rows, :] = jnp.where(keep, 0.0, NEG)
            seen = rank[TK - 1:TK, :]


def _dsa_select(kia, kib, qiT, wT):
    b_count, t, _ = kia.shape
    return pl.pallas_call(
        _dsa_select_kernel,
        grid=(b_count, t // TQ),
        in_specs=[pl.BlockSpec((1, t, LANE), lambda b, i: (b, 0, 0)),
                  pl.BlockSpec((1, t, LANE), lambda b, i: (b, 0, 0)),
                  pl.BlockSpec((1, IDX_HEADS * IDX_DIM, TQ), lambda b, i: (b, 0, i)),
                  pl.BlockSpec((1, LANE, TQ), lambda b, i: (b, 0, i))],
        out_specs=pl.BlockSpec((1, t, TQ), lambda b, i: (b, 0, i)),
        out_shape=jax.ShapeDtypeStruct((b_count, t, t), F32),
        scratch_shapes=[pltpu.VMEM((t, TQ), I32), pltpu.VMEM((1, TQ), I32), pltpu.VMEM((1, TQ), F32)],
        compiler_params=_cparams("parallel", "parallel"),
        name="dsa_select",
    )(kia, kib, qiT, wT)


def _softmax_attend_heads(qs, k_alls, vt_alls, bias_fns, n_tiles, causal_last):
    heads = range(len(qs))
    s_alls = [jnp.dot(k_alls[i], qs[i], preferred_element_type=F32) for i in heads]
    tiles = [[] for _ in heads]
    diag = (n_tiles - 1) * TK
    for i in heads:
        for lo in range(0, n_tiles * TK, KEY_CHUNK):
            sj = bias_fns[i](lo, s_alls[i][lo:lo + KEY_CHUNK])
            if causal_last and lo >= diag:
                s_iota = lax.broadcasted_iota(I32, (KEY_CHUNK, TQ), 0) + (lo - diag)
                sj = jnp.where(s_iota <= lax.broadcasted_iota(I32, (KEY_CHUNK, TQ), 1), sj, NEG)
            tiles[i].append(sj)
    ms = [functools.reduce(jnp.maximum, [jnp.max(sj, axis=0, keepdims=True) for sj in tiles[i]]) for i in heads]
    outs = []
    for i in heads:
        ps = [jnp.exp2(sj - ms[i]) for sj in tiles[i]]
        l = functools.reduce(jnp.add, [jnp.sum(p, axis=0, keepdims=True) for p in ps])
        ps = [p.astype(BF16) for p in ps]
        p_all = jnp.concatenate(ps, axis=0)
        outs.append(jnp.dot(vt_alls[i], p_all, preferred_element_type=F32) * (1.0 / l))
    return outs


def _for_each_query_tile(qb, n_q, body):
    for v in range(n_q):
        pl.when(qb == v)(functools.partial(body, v))


def _pair_rows(i):
    base = pl.multiple_of(i * 2 * HEAD_DIM, 2 * HEAD_DIM)
    return [pl.ds(base, HEAD_DIM), pl.ds(base + HEAD_DIM, HEAD_DIM)]

def _gate_and_store(oT_ref, gate_ref, o_ref):
    for r in range(GROUP):
        sl = slice(r * HEAD_DIM, (r + 1) * HEAD_DIM)
        gate = gate_ref[:, sl]
        o_ref[:, sl] = (_silu(gate) * oT_ref[sl, :].T).astype(BF16)


def _dsa_attn_kernel(qT_ref, k_ref, vT_ref, bias_ref, gate_ref, o_ref, oT_ref):
    def variant(v):
        n = (v + 1) * TK

        def bias_fn(lo, s):
            return s + bias_ref[0, lo:lo + KEY_CHUNK, :]

        def head_pair(i, carry):
            rows = _pair_rows(i)
            outs = _softmax_attend_heads([qT_ref[0, r, :] for r in rows], [k_ref[0, :n, :]] * 2,
                                         [vT_ref[0, :, :n]] * 2, [bias_fn] * 2, v + 1, False)
            for r, o in zip(rows, outs):
                oT_ref[r, :] = o
            return carry

        lax.fori_loop(0, GROUP // 2, head_pair, 0)

    _for_each_query_tile(pl.program_id(2), k_ref.shape[1] // TQ, variant)
    _gate_and_store(oT_ref, gate_ref, o_ref)


def _moba_attn_kernel(qT_ref, k_ref, vT_ref, kmean_ref, gate_ref, o_ref, oT_ref, selb_ref):
    own = pl.program_id(2)
    n_blocks = kmean_ref.shape[2]

    km = kmean_ref[0, 0].astype(BF16)
    gs = jnp.dot(km, qT_ref[0], preferred_element_type=F32)
    blk = lax.broadcasted_iota(I32, (n_blocks, TQ), 0)
    past = blk < own
    gs = jnp.where(past, gs, NEG)
    rank = jnp.zeros((n_blocks, TQ), F32)
    for m in range(n_blocks):
        row = gs[m:m + 1, :]
        rank = rank + ((row > gs) | ((row == gs) & (blk > m))).astype(F32)
    sel = rank < float(min(MOBA_TOPK, n_blocks - 1))
    selb_ref[...] = jnp.where(sel & past, 0.0, NEG)

    def variant(v):
        n = (v + 1) * TK

        def bias_fn(lo, s):
            j = lo // MOBA_BLOCK
            return s if j == v else s + selb_ref[j:j + 1, :]

        def head_pair(i, carry):
            rows = _pair_rows(i)
            outs = _softmax_attend_heads([qT_ref[0, r, :] for r in rows], [k_ref[0, :n, :]] * 2,
                                         [vT_ref[0, :, :n]] * 2, [bias_fn] * 2, v + 1, True)
            for r, o in zip(rows, outs):
                oT_ref[r, :] = o
            return carry

        lax.fori_loop(0, GROUP // 2, head_pair, 0)

    _for_each_query_tile(own, n_blocks, variant)
    _gate_and_store(oT_ref, gate_ref, o_ref)


def _fox_attn_kernel(qT_ref, k_ref, vT_ref, cum_ref, cumT_ref, gate_ref, o_ref, oT_ref):
    g = pl.program_id(1)

    def variant(v):
        n = (v + 1) * TK

        def make_bias_fn(h):
            cq = cumT_ref[0, pl.ds(h, 1), :]
            onehot = (lax.broadcasted_iota(I32, (KEY_CHUNK, LANE), 1) == h).astype(F32)

            def bias_fn(lo, s):
                ck = jnp.sum(cum_ref[0, lo:lo + KEY_CHUNK, :] * onehot, axis=1, keepdims=True)
                return s + (cq - ck)

            return bias_fn

        def head_pair(i, carry):
            rows = _pair_rows(i)
            heads = [2 * i, 2 * i + 1]
            outs = _softmax_attend_heads([qT_ref[0, r, :] for r in rows], [k_ref[0, r, :n, :] for r in heads],
                                         [vT_ref[0, r, :n] for r in rows],
                                         [make_bias_fn(g * GROUP + r) for r in heads], v + 1, True)
            for r, o in zip(rows, outs):
                oT_ref[r, :] = o
            return carry

        lax.fori_loop(0, GROUP // 2, head_pair, 0)

    _for_each_query_tile(pl.program_id(2), k_ref.shape[2] // TQ, variant)
    _gate_and_store(oT_ref, gate_ref, o_ref)


def _attn_scratch():
    return [pltpu.VMEM((GROUP * HEAD_DIM, TQ), F32)]


def _gqa_specs(t, gate_blk):
    nq = t // TQ
    gw = GROUP * HEAD_DIM
    in_specs = [
        pl.BlockSpec((1, gw, TQ), lambda b, g, i: (b, g, i)),
        pl.BlockSpec((1, t, HEAD_DIM), lambda b, g, i: (b, 0, g)),
        pl.BlockSpec((1, HEAD_DIM, t), lambda b, g, i: (b, g, 0)),
    ]
    gate_spec = pl.BlockSpec((TQ, gw), lambda b, g, i: (b * nq + i, gate_blk + g))
    out_spec = pl.BlockSpec((TQ, gw), lambda b, g, i: (b * nq + i, g))
    return nq, in_specs, gate_spec, out_spec


def _dsa_attn(qT, k, vT, bias, z):
    b_count, t, _ = k.shape
    gate_blk = (INNER + 2 * KV_WIDTH) // (GROUP * HEAD_DIM)
    nq, in_specs, gate_spec, out_spec = _gqa_specs(t, gate_blk)
    in_specs += [pl.BlockSpec((1, t, TQ), lambda b, g, i: (b, 0, i)), gate_spec]
    return pl.pallas_call(
        _dsa_attn_kernel,
        grid=(b_count, N_KV_HEADS, nq),
        in_specs=in_specs, out_specs=out_spec,
        out_shape=jax.ShapeDtypeStruct((b_count * t, INNER), BF16),
        scratch_shapes=_attn_scratch(),
        compiler_params=_cparams("parallel", "parallel", "parallel"),
        name="dsa_attn",
    )(qT, k, vT, bias, z)


def _moba_attn(qT, k, vT, kmean, z):
    b_count, t, _ = k.shape
    gate_blk = (INNER + 2 * KV_WIDTH) // (GROUP * HEAD_DIM)
    nq, in_specs, gate_spec, out_spec = _gqa_specs(t, gate_blk)
    n_blocks = t // MOBA_BLOCK
    in_specs += [pl.BlockSpec((1, 1, n_blocks, GROUP * HEAD_DIM), lambda b, g, i: (b, g, 0, 0)), gate_spec]
    return pl.pallas_call(
        _moba_attn_kernel,
        grid=(b_count, N_KV_HEADS, nq),
        in_specs=in_specs, out_specs=out_spec,
        out_shape=jax.ShapeDtypeStruct((b_count * t, INNER), BF16),
        scratch_shapes=_attn_scratch() + [pltpu.VMEM((n_blocks, TQ), F32)],
        compiler_params=_cparams("parallel", "parallel", "parallel"),
        name="moba_attn",
    )(qT, k, vT, kmean, z)


def _fox_attn(qT, k, vT, cum, cumT, z):
    b_count, _, t, _ = k.shape
    nq = t // TQ
    gw = GROUP * HEAD_DIM
    gate_blk = 3 * INNER // gw
    return pl.pallas_call(
        _fox_attn_kernel,
        grid=(b_count, N_HEADS // GROUP, nq),
        in_specs=[pl.BlockSpec((1, gw, TQ), lambda b, g, i: (b, g, i)),
                  pl.BlockSpec((1, GROUP, t, HEAD_DIM), lambda b, g, i: (b, g, 0, 0)),
                  pl.BlockSpec((1, gw, t), lambda b, g, i: (b, g, 0)),
                  pl.BlockSpec((1, t, LANE), lambda b, g, i: (b, 0, 0)),
                  pl.BlockSpec((1, LANE, TQ), lambda b, g, i: (b, 0, i)),
                  pl.BlockSpec((TQ, gw), lambda b, g, i: (b * nq + i, gate_blk + g))],
        out_specs=pl.BlockSpec((TQ, gw), lambda b, g, i: (b * nq + i, g)),
        out_shape=jax.ShapeDtypeStruct((b_count * t, INNER), BF16),
        scratch_shapes=_attn_scratch(),
        compiler_params=_cparams("parallel", "parallel", "parallel"),
        name="fox_attn",
    )(qT, k, vT, cum, cumT, z)


def _retention_kernel(q_ref, k_ref, v_ref, gate_ref, cos_ref, sin_ref, lg_ref, gn_ref, o_ref,
                      state_ref, dmask_ref):
    c = RET_CHUNK
    heads = range(RET_PAIR)
    lgs = [lg_ref[h][:, :1] for h in heads]

    @pl.when(pl.program_id(2) == 0)
    def _():
        state_ref[...] = jnp.zeros_like(state_ref)
        diff = (lax.broadcasted_iota(I32, (c, c), 0) - lax.broadcasted_iota(I32, (c, c), 1)).astype(F32)
        for h in heads:
            dmask_ref[h] = jnp.where(diff >= 0, jnp.exp(jnp.maximum(diff, 0.0) * lgs[h]), 0.0)

    cos = cos_ref[...]
    sin = sin_ref[...]
    half = RET_QK_DIM // 2

    def rope(ref, h):
        x1 = ref[:, h * RET_QK_DIM:h * RET_QK_DIM + half]
        x2 = ref[:, h * RET_QK_DIM + half:(h + 1) * RET_QK_DIM]
        return jnp.concatenate([x1 * cos - x2 * sin, x1 * sin + x2 * cos], axis=1)

    i_col = lax.broadcasted_iota(I32, (c, 1), 0).astype(F32)
    vsl = [slice(h * RET_V_DIM, (h + 1) * RET_V_DIM) for h in heads]
    qs = [rope(q_ref, h) for h in heads]
    ks = [rope(k_ref, h) * (RET_QK_DIM ** -0.5) for h in heads]
    inners = [lax.dot_general(qs[h].astype(BF16), ks[h].astype(BF16), (((1,), (1,)), ((), ())),
                              preferred_element_type=F32) for h in heads]
    vs = [v_ref[:, vsl[h]].astype(BF16) for h in heads]
    cross = [jnp.dot((qs[h] * jnp.exp((i_col + 1.0) * lgs[h])).astype(BF16), state_ref[h].astype(BF16),
                     preferred_element_type=F32) for h in heads]
    kdTs = [(ks[h] * jnp.exp((c - 1.0 - i_col) * lgs[h])).T.astype(BF16) for h in heads]
    os_ = [jnp.dot((inners[h] * dmask_ref[h]).astype(BF16), vs[h], preferred_element_type=F32) + cross[h]
           for h in heads]
    for h in heads:
        state_ref[h] = state_ref[h] * jnp.exp(c * lgs[h]) + jnp.dot(kdTs[h], vs[h], preferred_element_type=F32)
    for h in heads:
        o = os_[h]
        mu = jnp.mean(o, axis=-1, keepdims=True)
        var = jnp.mean(jnp.square(o - mu), axis=-1, keepdims=True)
        on = (o - mu) * lax.rsqrt(var + EPS) * gn_ref[:, vsl[h]]
        gate = gate_ref[:, vsl[h]]
        o_ref[:, vsl[h]] = (_silu(gate) * on).astype(BF16)


def _retention(z, b_count, t, gn_g):
    c = RET_CHUNK
    nc = t // c
    half = RET_QK_DIM // 2
    pos = jnp.arange(t, dtype=jnp.int32)
    inv = ROPE_THETA ** (-jnp.arange(0, RET_QK_DIM, 2, dtype=F32) / RET_QK_DIM)
    ang = pos.astype(F32)[:, None] * inv[None, :]
    cos, sin = jnp.cos(ang), jnp.sin(ang)
    log_gamma = jnp.log(1.0 - 2.0 ** (-5.0 - jnp.arange(RET_HEADS, dtype=F32)))
    lg = jnp.broadcast_to(log_gamma[:, None, None], (RET_HEADS, 1, LANE))
    qk_w = RET_HEADS * RET_QK_DIM
    qk_blk = RET_PAIR * RET_QK_DIM
    v_blk = RET_PAIR * RET_V_DIM
    row = lambda b, h, i: b * nc + i
    return pl.pallas_call(
        _retention_kernel,
        grid=(b_count, RET_HEADS // RET_PAIR, nc),
        in_specs=[pl.BlockSpec((c, qk_blk), lambda b, h, i: (row(b, h, i), h)),
                  pl.BlockSpec((c, qk_blk), lambda b, h, i: (row(b, h, i), qk_w // qk_blk + h)),
                  pl.BlockSpec((c, v_blk), lambda b, h, i: (row(b, h, i), 2 * qk_w // v_blk + h)),
                  pl.BlockSpec((c, v_blk), lambda b, h, i: (row(b, h, i), (2 * qk_w + RET_INNER) // v_blk + h)),
                  pl.BlockSpec((c, half), lambda b, h, i: (i, 0)),
                  pl.BlockSpec((c, half), lambda b, h, i: (i, 0)),
                  pl.BlockSpec((RET_PAIR, 1, LANE), lambda b, h, i: (h, 0, 0)),
                  pl.BlockSpec((1, v_blk), lambda b, h, i: (0, h))],
        out_specs=pl.BlockSpec((c, v_blk), lambda b, h, i: (row(b, h, i), h)),
        out_shape=jax.ShapeDtypeStruct((b_count * t, RET_INNER), BF16),
        scratch_shapes=[pltpu.VMEM((RET_PAIR, RET_QK_DIM, RET_V_DIM), F32), pltpu.VMEM((RET_PAIR, c, c), F32)],
        compiler_params=_cparams("parallel", "parallel", "arbitrary"),
        name="retention",
    )(z, z, z, z, cos, sin, lg, gn_g.reshape(1, RET_INNER))


def _tail_proj_kernel(a_ref, wt_ref, o_ref):
    o_ref[...] = lax.dot_general(a_ref[...], wt_ref[...], (((1,), (1,)), ((), ())), preferred_element_type=F32)


def _tail_proj(hn, w_in_t, n_main, name, tm=1024):
    m, k = hn.shape
    wt_tail = jnp.pad(w_in_t[n_main:, :], ((0, LANE - (w_in_t.shape[0] - n_main)), (0, 0))).astype(BF16)
    return pl.pallas_call(
        _tail_proj_kernel,
        grid=(m // tm,),
        in_specs=[pl.BlockSpec((tm, k), lambda i: (i, 0)),
                  pl.BlockSpec((LANE, k), lambda i: (0, 0))],
        out_specs=pl.BlockSpec((tm, LANE), lambda i: (i, 0)),
        out_shape=jax.ShapeDtypeStruct((m, LANE), F32),
        compiler_params=_cparams("parallel"),
        name=name,
    )(hn, wt_tail)


def _dsa_layer(h, hn, b_count, t, w_in, q_g, k_g, w_out, next_g):
    n_main = 2 * INNER + 2 * KV_WIDTH + IDX_HEADS * IDX_DIM
    w_in_t = w_in.T
    z = _in_proj_wt(hn, w_in_t, n_main, 1024, 1024, "dsa_in_proj")
    z_tail = _tail_proj(hn, w_in_t, n_main, "dsa_tail_proj")
    qT, k, vT, qiT, kia, kib, wT = _dsa_prep(z, z_tail, b_count, t, q_g, k_g)
    bias = _dsa_select(kia, kib, qiT, wT)
    gated = _dsa_attn(qT, k, vT, bias, z)
    return _out_proj(gated, w_out.astype(BF16), h, next_g, 512, "dsa_out_proj")


def _moba_layer(h, hn, b_count, t, w_in, q_g, k_g, w_out, next_g):
    z = _in_proj(hn, w_in, w_in.shape[1], 1024, 1024, "moba_in_proj")
    qT, k, vT, kmean = _moba_prep(z, b_count, t, q_g, k_g)
    gated = _moba_attn(qT, k, vT, kmean, z)
    return _out_proj(gated, w_out.astype(BF16), h, next_g, 512, "moba_out_proj")


def _ret_layer(h, hn, b_count, t, w_in, gn_g, w_out, next_g):
    z = _in_proj(hn, w_in, w_in.shape[1], 1024, 1024, "ret_in_proj")
    gated = _retention(z, b_count, t, gn_g)
    return _out_proj(gated, w_out.astype(BF16), h, next_g, 256, "ret_out_proj")


def _fox_layer(h, hn, b_count, t, w_in, f_bias, q_g, k_g, w_out, next_g):
    n_main = 4 * INNER
    w_in_t = w_in.T
    z = _in_proj_wt(hn, w_in_t, n_main, 1024, 1024, "fox_in_proj")
    z_tail = _tail_proj(hn, w_in_t, n_main, "fox_tail_proj")
    qT, k, vT, cum, cumT = _fox_prep(z, z_tail, b_count, t, f_bias, q_g, k_g)
    gated = _fox_attn(qT, k, vT, cum, cumT, z)
    return _out_proj(gated, w_out.astype(BF16), h, next_g, 512, "fox_out_proj")


def kernel(x, a_norm, a_w_in, a_q_norm, a_k_norm, a_w_out, b_norm, b_w_in, b_q_norm, b_k_norm, b_w_out,
           c_norm, c_w_in, c_gn, c_w_out, d_norm, d_w_in, d_f_bias, d_q_norm, d_k_norm, d_w_out):
    b_count, t, d = x.shape
    assert d == D_MODEL and t % TQ == 0 and t // 4 >= DSA_TOPK
    depth = 4
    norms = (a_norm, b_norm, c_norm, d_norm)
    h = x.reshape(b_count * t, d)
    hn = _rmsnorm(h, a_norm[0])
    for i in range(depth):
        m, j = i % 4, i // 4
        next_g = norms[(i + 1) % 4][(i + 1) // 4] if i + 1 < depth else None
        if m == 0:
            h, hn = _dsa_layer(h, hn, b_count, t, a_w_in[j], a_q_norm[j], a_k_norm[j], a_w_out[j], next_g)
        elif m == 1:
            h, hn = _moba_layer(h, hn, b_count, t, b_w_in[j], b_q_norm[j], b_k_norm[j], b_w_out[j], next_g)
        elif m == 2:
            h, hn = _ret_layer(h, hn, b_count, t, c_w_in[j], c_gn[j], c_w_out[j], next_g)
        else:
            h, hn = _fox_layer(h, hn, b_count, t, d_w_in[j], d_f_bias[j], d_q_norm[j], d_k_norm[j], d_w_out[j],
                               next_g)
    return h.reshape(b_count, t, d)
```

```python
import functools

import jax
import jax.numpy as jnp
from jax import lax
from jax.experimental import pallas as pl
from jax.experimental.pallas import tpu as pltpu

F32 = jnp.float32
BF16 = jnp.bfloat16
I32 = jnp.int32

D_MODEL = 2048
HEAD_DIM = 128
N_HEADS = 16
N_KV_HEADS = 4
GROUP = N_HEADS // N_KV_HEADS
INNER = N_HEADS * HEAD_DIM
KV_WIDTH = N_KV_HEADS * HEAD_DIM
IDX_HEADS = 16
IDX_DIM = 64
DSA_TOPK = 256
MOBA_BLOCK = 256
MOBA_TOPK = 3
RET_HEADS = 8
RET_QK_DIM = 256
RET_V_DIM = 512
RET_INNER = RET_HEADS * RET_V_DIM
ROPE_THETA = 10000.0
EPS = 1e-6
NEG = -1e30
INT_MIN = -(2 ** 31)
LOG2E = 1.4426950408889634

LANE = 128
TQ = 256
TK = 256
KEY_CHUNK = 256
RET_CHUNK = 256
RET_PAIR = 2
VMEM_LIMIT = 56 * 1024 * 1024


def _cparams(*sem):
    return pltpu.CompilerParams(dimension_semantics=sem, vmem_limit_bytes=VMEM_LIMIT)


def _rmsnorm_kernel(x_ref, g_ref, o_ref):
    x = x_ref[...]
    ms = jnp.mean(x * x, axis=-1, keepdims=True)
    o_ref[...] = (x * lax.rsqrt(ms + EPS) * g_ref[...]).astype(o_ref.dtype)


def _rmsnorm(x, g, tm=512):
    m, d = x.shape
    return pl.pallas_call(
        _rmsnorm_kernel,
        grid=(m // tm,),
        in_specs=[pl.BlockSpec((tm, d), lambda i: (i, 0)),
                  pl.BlockSpec((1, d), lambda i: (0, 0))],
        out_specs=pl.BlockSpec((tm, d), lambda i: (i, 0)),
        out_shape=jax.ShapeDtypeStruct((m, d), BF16),
        compiler_params=_cparams("parallel"),
        name="rmsnorm",
    )(x, g.reshape(1, d))


def _in_proj_kernel(a_ref, w_ref, o_ref, wb_ref):
    @pl.when(pl.program_id(1) == 0)
    def _():
        wb_ref[...] = w_ref[...].astype(BF16)

    o_ref[...] = jnp.dot(a_ref[...], wb_ref[...], preferred_element_type=F32)


def _serpentine(n_rows):
    return lambda j, i: i + (j % 2) * (n_rows - 1 - 2 * i)


def _in_proj(a, w, n, tm, tn, name):
    m, k = a.shape
    row = _serpentine(m // tm)
    return pl.pallas_call(
        _in_proj_kernel,
        grid=(n // tn, m // tm),
        in_specs=[pl.BlockSpec((tm, k), lambda j, i: (row(j, i), 0)),
                  pl.BlockSpec((k, tn), lambda j, i: (0, j))],
        out_specs=pl.BlockSpec((tm, tn), lambda j, i: (row(j, i), j)),
        out_shape=jax.ShapeDtypeStruct((m, n), F32),
        scratch_shapes=[pltpu.VMEM((k, tn), BF16)],
        compiler_params=_cparams("parallel", "arbitrary"),
        name=name,
    )(a, w)


def _in_proj_wt_kernel(a_ref, wt_ref, o_ref, wb_ref):
    @pl.when(pl.program_id(1) == 0)
    def _():
        wb_ref[...] = wt_ref[...].T.astype(BF16)

    o_ref[...] = jnp.dot(a_ref[...], wb_ref[...], preferred_element_type=F32)


def _in_proj_wt(a, wt, n, tm, tn, name):
    m, k = a.shape
    row = _serpentine(m // tm)
    return pl.pallas_call(
        _in_proj_wt_kernel,
        grid=(n // tn, m // tm),
        in_specs=[pl.BlockSpec((tm, k), lambda j, i: (row(j, i), 0)),
                  pl.BlockSpec((tn, k), lambda j, i: (j, 0))],
        out_specs=pl.BlockSpec((tm, tn), lambda j, i: (row(j, i), j)),
        out_shape=jax.ShapeDtypeStruct((m, n), F32),
        scratch_shapes=[pltpu.VMEM((k, tn), BF16)],
        compiler_params=_cparams("parallel", "arbitrary"),
        name=name,
    )(a, wt)


def _out_proj_kernel(a_ref, w_ref, r_ref, o_ref):
    o_ref[...] = r_ref[...] + jnp.dot(a_ref[...], w_ref[...], preferred_element_type=F32)


def _out_proj_norm_kernel(a_ref, w_ref, r_ref, g_ref, o_ref, hn_ref):
    h = r_ref[...] + jnp.dot(a_ref[...], w_ref[...], preferred_element_type=F32)
    o_ref[...] = h
    ms = jnp.mean(h * h, axis=-1, keepdims=True)
    hn_ref[...] = (h * lax.rsqrt(ms + EPS) * g_ref[...]).astype(BF16)


def _out_proj(a, w, res, next_g, tm, name):
    m, k = a.shape
    n = w.shape[1]
    in_specs = [pl.BlockSpec((tm, k), lambda i: (i, 0)),
                pl.BlockSpec((k, n), lambda i: (0, 0)),
                pl.BlockSpec((tm, n), lambda i: (i, 0))]
    out_specs = [pl.BlockSpec((tm, n), lambda i: (i, 0))]
    out_shape = [jax.ShapeDtypeStruct((m, n), F32)]
    args = [a, w, res]
    if next_g is not None:
        in_specs.append(pl.BlockSpec((1, n), lambda i: (0, 0)))
        out_specs.append(pl.BlockSpec((tm, n), lambda i: (i, 0)))
        out_shape.append(jax.ShapeDtypeStruct((m, n), BF16))
        args.append(next_g.reshape(1, n))
    out = pl.pallas_call(
        _out_proj_kernel if next_g is None else _out_proj_norm_kernel,
        grid=(m // tm,),
        in_specs=in_specs, out_specs=out_specs, out_shape=out_shape,
        compiler_params=_cparams("parallel"),
        name=name,
    )(*args)
    return (out[0], out[1]) if next_g is not None else (out[0], None)


def _rope_tables(t, d, reps):
    pos = jnp.arange(t, dtype=jnp.int32)
    inv = ROPE_THETA ** (-jnp.arange(0, d, 2, dtype=F32) / d)
    ang = pos.astype(F32)[:, None] * inv[None, :]
    cos, sin = jnp.cos(ang), jnp.sin(ang)
    cos_t = jnp.tile(jnp.concatenate([cos, cos], axis=-1), (1, reps))
    sin_t = jnp.tile(jnp.concatenate([-sin, sin], axis=-1), (1, reps))
    return cos_t, sin_t


def _rope_tables_t(t, d):
    pos = jnp.arange(t, dtype=jnp.int32)
    inv = ROPE_THETA ** (-jnp.arange(0, d, 2, dtype=F32) / d)
    ang = pos.astype(F32)[:, None] * inv[None, :]
    return jnp.cos(ang).T, jnp.sin(ang).T


def _silu(x):
    hx = 0.5 * x
    return hx + hx * jnp.tanh(hx)


def _head_norm(x, g):
    ms = jnp.mean(x * x, axis=-1, keepdims=True)
    return x * lax.rsqrt(ms + EPS) * g


def _rope128(y, cos, sin):
    return y * cos + pltpu.roll(y, 64, 1) * sin


def _rope_rows(y, cos_t, sin_t):
    half = y.shape[0] // 2
    y1, y2 = y[:half], y[half:]
    return jnp.concatenate([y1 * cos_t - y2 * sin_t, y1 * sin_t + y2 * cos_t], axis=0)


def _qkv_prep_body(q_ref, k_ref, v_ref, cos_ref, sin_ref, cos_t_ref, sin_t_ref, qg_col_ref, kg_ref,
                   qT_ref, ko_ref, vT_ref, *, n_kv, rope):
    tm = q_ref.shape[0]
    kg = kg_ref[...]
    scale = HEAD_DIM ** -0.5 * LOG2E
    if rope:
        cos = cos_ref[...]
        sin = sin_ref[...]
        cos_t = cos_t_ref[...]
        sin_t = sin_t_ref[...]
    qg_t = jnp.broadcast_to(qg_col_ref[...], (HEAD_DIM, tm)) * scale
    for h in range(N_HEADS):
        sl = slice(h * HEAD_DIM, (h + 1) * HEAD_DIM)
        x = q_ref[:, sl].T
        ms = jnp.mean(x * x, axis=0, keepdims=True)
        y = x * lax.rsqrt(ms + EPS) * qg_t
        if rope:
            y = _rope_rows(y, cos_t, sin_t)
        qT_ref[0, sl, :] = y.astype(BF16)
    k_out = []
    for g in range(n_kv):
        sl = slice(g * HEAD_DIM, (g + 1) * HEAD_DIM)
        y = _head_norm(k_ref[:, sl], kg)
        if rope:
            y = _rope128(y, cos, sin)
        if ko_ref.ndim == 4:
            ko_ref[0, g] = y.astype(BF16)
        else:
            ko_ref[0, :, sl] = y.astype(BF16)
        vT_ref[0, sl, :] = v_ref[:, sl].T.astype(BF16)
        k_out.append(y)
    return k_out


def _dsa_prep_kernel(q_ref, k_ref, v_ref, qi_ref, tail_ref, cos_ref, sin_ref, cos_t_ref, sin_t_ref,
                     cos64_ref, sin64_ref, cos32_t_ref, sin32_t_ref,
                     qg_ref, kg_ref, qT_ref, ko_ref, vT_ref, qiT_ref, kia_ref, kib_ref, wT_ref):
    _qkv_prep_body(q_ref, k_ref, v_ref, cos_ref, sin_ref, cos_t_ref, sin_t_ref, qg_ref, kg_ref,
                   qT_ref, ko_ref, vT_ref, n_kv=N_KV_HEADS, rope=True)
    tm = tail_ref.shape[0]
    lane = lax.broadcasted_iota(I32, (tm, LANE), 1)
    first_half = (lane % IDX_DIM) < (IDX_DIM // 2)
    c64 = cos64_ref[...]
    s64 = sin64_ref[...]

    def rope64(x):
        rot = jnp.where(first_half, pltpu.roll(x, LANE - IDX_DIM // 2, 1), pltpu.roll(x, IDX_DIM // 2, 1))
        return x * c64 + rot * s64

    c32_t = cos32_t_ref[...]
    s32_t = sin32_t_ref[...]
    for j in range(IDX_HEADS * IDX_DIM // LANE):
        sl = slice(j * LANE, (j + 1) * LANE)
        x = qi_ref[:, sl].T * (IDX_DIM ** -0.5)
        y = jnp.concatenate([_rope_rows(x[:IDX_DIM], c32_t, s32_t), _rope_rows(x[IDX_DIM:], c32_t, s32_t)], axis=0)
        qiT_ref[0, sl, :] = y.astype(BF16)
    tail = tail_ref[...]
    ka = jnp.where(lane < IDX_DIM, rope64(tail), 0.0)
    kia_ref[0] = ka.astype(BF16)
    kib_ref[0] = pltpu.roll(ka, IDX_DIM, 1).astype(BF16)
    wT_ref[0] = (tail * (IDX_HEADS ** -0.5)).T


def _moba_prep_kernel(q_ref, k_ref, v_ref, cos_ref, sin_ref, cos_t_ref, sin_t_ref, qg_ref, kg_ref,
                      qT_ref, ko_ref, vT_ref, kmean_ref):
    k_out = _qkv_prep_body(q_ref, k_ref, v_ref, cos_ref, sin_ref, cos_t_ref, sin_t_ref, qg_ref, kg_ref,
                           qT_ref, ko_ref, vT_ref, n_kv=N_KV_HEADS, rope=True)
    for g in range(N_KV_HEADS):
        km = jnp.mean(k_out[g], axis=0, keepdims=True)
        kmean_ref[0, g, pl.ds(pl.program_id(1), 1), :] = jnp.concatenate([km] * GROUP, axis=1)


def _fox_prep_kernel(q_ref, k_ref, v_ref, f_ref, fb_ref, qg_ref, kg_ref,
                     qT_ref, ko_ref, vT_ref, cum_ref, cumT_ref, carry_ref):
    _qkv_prep_body(q_ref, k_ref, v_ref, None, None, None, None, qg_ref, kg_ref, qT_ref, ko_ref, vT_ref,
                   n_kv=N_HEADS, rope=False)

    @pl.when(pl.program_id(1) == 0)
    def _():
        carry_ref[...] = jnp.zeros_like(carry_ref)

    tm = f_ref.shape[0]
    x = f_ref[...] + fb_ref[...]
    lf = jnp.minimum(x, 0.0) - jnp.log(1.0 + jnp.exp(-jnp.abs(x)))
    hi = lf.astype(BF16)
    r1 = lf - hi.astype(F32)
    lo = r1.astype(BF16)
    lo2 = (r1 - lo.astype(F32)).astype(BF16)
    row = lax.broadcasted_iota(I32, (tm, tm), 0)
    col = lax.broadcasted_iota(I32, (tm, tm), 1)
    tri = (col <= row).astype(BF16)
    parts = jnp.dot(tri, jnp.concatenate([hi, lo, lo2], axis=1), preferred_element_type=F32)
    cum = parts[:, :LANE] + parts[:, LANE:2 * LANE] + parts[:, 2 * LANE:] + carry_ref[...]
    carry_ref[...] = cum[tm - 1:tm, :]
    cum2 = cum * LOG2E
    cum_ref[0] = cum2
    cumT_ref[0] = cum2.T


def _prep_specs(b_count, t, tm, z_q_blk, z_k_blk, z_v_blk, kv_width):
    nt = t // tm
    row = lambda b, i: b * nt + i
    in_specs = [
        pl.BlockSpec((tm, INNER), lambda b, i: (row(b, i), z_q_blk)),
        pl.BlockSpec((tm, kv_width), lambda b, i: (row(b, i), z_k_blk)),
        pl.BlockSpec((tm, kv_width), lambda b, i: (row(b, i), z_v_blk)),
    ]
    out_specs = [
        pl.BlockSpec((1, INNER, tm), lambda b, i: (b, 0, i)),
        pl.BlockSpec((1, tm, kv_width), lambda b, i: (b, i, 0)),
        pl.BlockSpec((1, kv_width, tm), lambda b, i: (b, 0, i)),
    ]
    out_shape = [
        jax.ShapeDtypeStruct((b_count, INNER, t), BF16),
        jax.ShapeDtypeStruct((b_count, t, kv_width), BF16),
        jax.ShapeDtypeStruct((b_count, kv_width, t), BF16),
    ]
    return row, in_specs, out_specs, out_shape


def _dsa_prep(z, z_tail, b_count, t, q_g, k_g, tm=256):
    row, in_specs, out_specs, out_shape = _prep_specs(b_count, t, tm, 0, INNER // KV_WIDTH,
                                                      INNER // KV_WIDTH + 1, KV_WIDTH)
    qi_w = IDX_HEADS * IDX_DIM
    qi_off = 2 * INNER + 2 * KV_WIDTH
    cos, sin = _rope_tables(t, HEAD_DIM, 1)
    cos64, sin64 = _rope_tables(t, IDX_DIM, LANE // IDX_DIM)
    cos_t, sin_t = _rope_tables_t(t, HEAD_DIM)
    cos32_t, sin32_t = _rope_tables_t(t, IDX_DIM)
    tab = pl.BlockSpec((tm, LANE), lambda b, i: (i, 0))
    tab_t = pl.BlockSpec((HEAD_DIM // 2, tm), lambda b, i: (0, i))
    tab32_t = pl.BlockSpec((IDX_DIM // 2, tm), lambda b, i: (0, i))
    gain = pl.BlockSpec((1, HEAD_DIM), lambda b, i: (0, 0))
    gain_col = pl.BlockSpec((HEAD_DIM, 1), lambda b, i: (0, 0))
    in_specs += [
        pl.BlockSpec((tm, qi_w), lambda b, i: (row(b, i), qi_off // qi_w)),
        pl.BlockSpec((tm, LANE), lambda b, i: (row(b, i), 0)),
        tab, tab, tab_t, tab_t, tab, tab, tab32_t, tab32_t, gain_col, gain,
    ]
    out_specs += [
        pl.BlockSpec((1, qi_w, tm), lambda b, i: (b, 0, i)),
        pl.BlockSpec((1, tm, LANE), lambda b, i: (b, i, 0)),
        pl.BlockSpec((1, tm, LANE), lambda b, i: (b, i, 0)),
        pl.BlockSpec((1, LANE, tm), lambda b, i: (b, 0, i)),
    ]
    out_shape += [
        jax.ShapeDtypeStruct((b_count, qi_w, t), BF16),
        jax.ShapeDtypeStruct((b_count, t, LANE), BF16),
        jax.ShapeDtypeStruct((b_count, t, LANE), BF16),
        jax.ShapeDtypeStruct((b_count, LANE, t), F32),
    ]
    return pl.pallas_call(
        _dsa_prep_kernel,
        grid=(b_count, t // tm),
        in_specs=in_specs, out_specs=out_specs, out_shape=out_shape,
        compiler_params=_cparams("parallel", "parallel"),
        name="dsa_prep",
    )(z, z, z, z, z_tail, cos, sin, cos_t, sin_t, cos64, sin64, cos32_t, sin32_t,
      q_g.reshape(HEAD_DIM, 1), k_g.reshape(1, HEAD_DIM))


def _moba_prep(z, b_count, t, q_g, k_g):
    tm = MOBA_BLOCK
    row, in_specs, out_specs, out_shape = _prep_specs(b_count, t, tm, 0, INNER // KV_WIDTH,
                                                      INNER // KV_WIDTH + 1, KV_WIDTH)
    cos, sin = _rope_tables(t, HEAD_DIM, 1)
    cos_t, sin_t = _rope_tables_t(t, HEAD_DIM)
    tab = pl.BlockSpec((tm, LANE), lambda b, i: (i, 0))
    tab_t = pl.BlockSpec((HEAD_DIM // 2, tm), lambda b, i: (0, i))
    gain = pl.BlockSpec((1, HEAD_DIM), lambda b, i: (0, 0))
    gain_col = pl.BlockSpec((HEAD_DIM, 1), lambda b, i: (0, 0))
    in_specs += [tab, tab, tab_t, tab_t, gain_col, gain]
    out_specs += [pl.BlockSpec((1, N_KV_HEADS, t // tm, GROUP * HEAD_DIM), lambda b, i: (b, 0, 0, 0))]
    out_shape += [jax.ShapeDtypeStruct((b_count, N_KV_HEADS, t // tm, GROUP * HEAD_DIM), F32)]
    return pl.pallas_call(
        _moba_prep_kernel,
        grid=(b_count, t // tm),
        in_specs=in_specs, out_specs=out_specs, out_shape=out_shape,
        compiler_params=_cparams("parallel", "arbitrary"),
        name="moba_prep",
    )(z, z, z, cos, sin, cos_t, sin_t, q_g.reshape(HEAD_DIM, 1), k_g.reshape(1, HEAD_DIM))


def _fox_prep(z, z_tail, b_count, t, f_bias, q_g, k_g, tm=256):
    row, in_specs, out_specs, out_shape = _prep_specs(b_count, t, tm, 0, 1, 2, INNER)
    out_specs[1] = pl.BlockSpec((1, N_HEADS, tm, HEAD_DIM), lambda b, i: (b, 0, i, 0))
    out_shape[1] = jax.ShapeDtypeStruct((b_count, N_HEADS, t, HEAD_DIM), BF16)
    gain = pl.BlockSpec((1, HEAD_DIM), lambda b, i: (0, 0))
    fb = jnp.pad(f_bias.reshape(1, N_HEADS), ((0, 0), (0, LANE - N_HEADS)))
    in_specs += [
        pl.BlockSpec((tm, LANE), lambda b, i: (row(b, i), 0)),
        pl.BlockSpec((1, LANE), lambda b, i: (0, 0)),
        pl.BlockSpec((HEAD_DIM, 1), lambda b, i: (0, 0)), gain,
    ]
    out_specs += [
        pl.BlockSpec((1, tm, LANE), lambda b, i: (b, i, 0)),
        pl.BlockSpec((1, LANE, tm), lambda b, i: (b, 0, i)),
    ]
    out_shape += [
        jax.ShapeDtypeStruct((b_count, t, LANE), F32),
        jax.ShapeDtypeStruct((b_count, LANE, t), F32),
    ]
    return pl.pallas_call(
        _fox_prep_kernel,
        grid=(b_count, t // tm),
        in_specs=in_specs, out_specs=out_specs, out_shape=out_shape,
        scratch_shapes=[pltpu.VMEM((1, LANE), F32)],
        compiler_params=_cparams("parallel", "arbitrary"),
        name="fox_prep",
    )(z, z, z, z_tail, fb, q_g.reshape(HEAD_DIM, 1), k_g.reshape(1, HEAD_DIM))


def _dsa_select_kernel(kia_ref, kib_ref, qiT_ref, wT_ref, bias_ref, key_ref, ans_ref, n_ge_ref):
    qb = pl.program_id(1)
    n_kt = key_ref.shape[0] // TK
    w = wT_ref[0]
    s_iota = lax.broadcasted_iota(I32, (TK, TQ), 0)
    t_iota = lax.broadcasted_iota(I32, (TK, TQ), 1)

    for kt in range(n_kt):
        rows = slice(kt * TK, (kt + 1) * TK)

        @pl.when(kt <= qb)
        def _():
            ka = kia_ref[0, rows, :]
            kb = kib_ref[0, rows, :]
            score = jnp.zeros((TK, TQ), F32)
            for j in range(IDX_HEADS // 2):
                qp = qiT_ref[0, j * LANE:(j + 1) * LANE, :]
                ra = jnp.maximum(jnp.dot(ka, qp, preferred_element_type=F32), 0.0)
                rb = jnp.maximum(jnp.dot(kb, qp, preferred_element_type=F32), 0.0)
                wa = w[IDX_DIM + 2 * j:IDX_DIM + 2 * j + 1, :]
                wb = w[IDX_DIM + 2 * j + 1:IDX_DIM + 2 * j + 2, :]
                score = score + (ra * wa + rb * wb)
            bits = lax.bitcast_convert_type(score + 0.0, I32)
            key = bits ^ ((bits >> 31) & 0x7FFFFFFF)
            causal = (kt * TK + s_iota) <= (qb * TQ + t_iota)
            key_ref[rows, :] = jnp.where(causal, key, INT_MIN)

        @pl.when(kt > qb)
        def _():
            key_ref[rows, :] = jnp.full((TK, TQ), INT_MIN, I32)

    def variant(v):
        n = (v + 1) * TK
        kf = float(DSA_TOPK)

        def count_ge(cand):
            return jnp.sum((key_ref[:n, :] >= cand).astype(F32), axis=0, keepdims=True)

        def bit_step(i, ans):
            cand = ans | (jnp.int32(1) << (30 - i))
            return jnp.where(count_ge(cand) >= kf, cand, ans)

        sign = jnp.where(count_ge(jnp.zeros((1, TQ), I32)) >= kf, 0, INT_MIN).astype(I32)
        ans = lax.fori_loop(0, 31, bit_step, sign)
        ans_ref[...] = ans
        n_ge_ref[...] = count_ge(ans)

    _for_each_query_tile(qb, n_kt, variant)
    ans = ans_ref[...]

    for kt in range(n_kt):
        rows = slice(kt * TK, (kt + 1) * TK)
        k = key_ref[rows, :]
        sel = (k >= ans) & (k > INT_MIN)
        bias_ref[0, rows, :] = jnp.where(sel, 0.0, NEG)

    @pl.when(jnp.max(n_ge_ref[...]) > float(DSA_TOPK))
    def _():
        n_gt = jnp.zeros((1, TQ), F32)
        for kt in range(n_kt):
            n_gt = n_gt + jnp.sum((key_ref[kt * TK:(kt + 1) * TK, :] > ans).astype(F32), axis=0, keepdims=True)
        n_tied_kept = float(DSA_TOPK) - n_gt
        tri = (lax.broadcasted_iota(I32, (TK, TK), 1) <= lax.broadcasted_iota(I32, (TK, TK), 0)).astype(BF16)
        seen = jnp.zeros((1, TQ), F32)
        for kt in range(n_kt):
            rows = slice(kt * TK, (kt + 1) * TK)
            k = key_ref[rows, :]
            tied = (k == ans) & (k > INT_MIN)
            rank = jnp.dot(tri, tied.astype(F32).astype(BF16), preferred_element_type=F32) + seen
            keep = (k > ans) | (tied & (rank <= n_tied_kept))
            bias_ref[0, rows, :] = jnp.where(keep, 0.0, NEG)
            seen = rank[TK - 1:TK, :]


def _dsa_select(kia, kib, qiT, wT):
    b_count, t, _ = kia.shape
    return pl.pallas_call(
        _dsa_select_kernel,
        grid=(b_count, t // TQ),
        in_specs=[pl.BlockSpec((1, t, LANE), lambda b, i: (b, 0, 0)),
                  pl.BlockSpec((1, t, LANE), lambda b, i: (b, 0, 0)),
                  pl.BlockSpec((1, IDX_HEADS * IDX_DIM, TQ), lambda b, i: (b, 0, i)),
                  pl.BlockSpec((1, LANE, TQ), lambda b, i: (b, 0, i))],
        out_specs=pl.BlockSpec((1, t, TQ), lambda b, i: (b, 0, i)),
        out_shape=jax.ShapeDtypeStruct((b_count, t, t), F32),
        scratch_shapes=[pltpu.VMEM((t, TQ), I32), pltpu.VMEM((1, TQ), I32), pltpu.VMEM((1, TQ), F32)],
        compiler_params=_cparams("parallel", "parallel"),
        name="dsa_select",
    )(kia, kib, qiT, wT)


def _softmax_attend_heads(qs, k_alls, vt_alls, bias_fns, n_tiles, causal_last):
    heads = range(len(qs))
    s_alls = [jnp.dot(k_alls[i], qs[i], preferred_element_type=F32) for i in heads]
    tiles = [[] for _ in heads]
    diag = (n_tiles - 1) * TK
    for i in heads:
        for lo in range(0, n_tiles * TK, KEY_CHUNK):
            sj = bias_fns[i](lo, s_alls[i][lo:lo + KEY_CHUNK])
            if causal_last and lo >= diag:
                s_iota = lax.broadcasted_iota(I32, (KEY_CHUNK, TQ), 0) + (lo - diag)
                sj = jnp.where(s_iota <= lax.broadcasted_iota(I32, (KEY_CHUNK, TQ), 1), sj, NEG)
            tiles[i].append(sj)
    ms = [functools.reduce(jnp.maximum, [jnp.max(sj, axis=0, keepdims=True) for sj in tiles[i]]) for i in heads]
    outs = []
    for i in heads:
        ps = [jnp.exp2(sj - ms[i]) for sj in tiles[i]]
        l = functools.reduce(jnp.add, [jnp.sum(p, axis=0, keepdims=True) for p in ps])
        ps = [p.astype(BF16) for p in ps]
        p_all = jnp.concatenate(ps, axis=0)
        outs.append(jnp.dot(vt_alls[i], p_all, preferred_element_type=F32) * (1.0 / l))
    return outs


def _for_each_query_tile(qb, n_q, body):
    for v in range(n_q):
        pl.when(qb == v)(functools.partial(body, v))


def _pair_rows(i):
    base = pl.multiple_of(i * 2 * HEAD_DIM, 2 * HEAD_DIM)
    return [pl.ds(base, HEAD_DIM), pl.ds(base + HEAD_DIM, HEAD_DIM)]

def _gate_and_store(oT_ref, gate_ref, o_ref):
    for r in range(GROUP):
        sl = slice(r * HEAD_DIM, (r + 1) * HEAD_DIM)
        gate = gate_ref[:, sl]
        o_ref[:, sl] = (_silu(gate) * oT_ref[sl, :].T).astype(BF16)


def _dsa_attn_kernel(qT_ref, k_ref, vT_ref, bias_ref, gate_ref, o_ref, oT_ref):
    def variant(v):
        n = (v + 1) * TK

        def bias_fn(lo, s):
            return s + bias_ref[0, lo:lo + KEY_CHUNK, :]

        def head_pair(i, carry):
            rows = _pair_rows(i)
            outs = _softmax_attend_heads([qT_ref[0, r, :] for r in rows], [k_ref[0, :n, :]] * 2,
                                         [vT_ref[0, :, :n]] * 2, [bias_fn] * 2, v + 1, False)
            for r, o in zip(rows, outs):
                oT_ref[r, :] = o
            return carry

        lax.fori_loop(0, GROUP // 2, head_pair, 0)

    _for_each_query_tile(pl.program_id(2), k_ref.shape[1] // TQ, variant)
    _gate_and_store(oT_ref, gate_ref, o_ref)


def _moba_attn_kernel(qT_ref, k_ref, vT_ref, kmean_ref, gate_ref, o_ref, oT_ref, selb_ref):
    own = pl.program_id(2)
    n_blocks = kmean_ref.shape[2]

    km = kmean_ref[0, 0].astype(BF16)
    gs = jnp.dot(km, qT_ref[0], preferred_element_type=F32)
    blk = lax.broadcasted_iota(I32, (n_blocks, TQ), 0)
    past = blk < own
    gs = jnp.where(past, gs, NEG)
    rank = jnp.zeros((n_blocks, TQ), F32)
    for m in range(n_blocks):
        row = gs[m:m + 1, :]
        rank = rank + ((row > gs) | ((row == gs) & (blk > m))).astype(F32)
    sel = rank < float(min(MOBA_TOPK, n_blocks - 1))
    selb_ref[...] = jnp.where(sel & past, 0.0, NEG)

    def variant(v):
        n = (v + 1) * TK

        def bias_fn(lo, s):
            j = lo // MOBA_BLOCK
            return s if j == v else s + selb_ref[j:j + 1, :]

        def head_pair(i, carry):
            rows = _pair_rows(i)
            outs = _softmax_attend_heads([qT_ref[0, r, :] for r in rows], [k_ref[0, :n, :]] * 2,
                                         [vT_ref[0, :, :n]] * 2, [bias_fn] * 2, v + 1, True)
            for r, o in zip(rows, outs):
                oT_ref[r, :] = o
            return carry

        lax.fori_loop(0, GROUP // 2, head_pair, 0)

    _for_each_query_tile(own, n_blocks, variant)
    _gate_and_store(oT_ref, gate_ref, o_ref)


def _fox_attn_kernel(qT_ref, k_ref, vT_ref, cum_ref, cumT_ref, gate_ref, o_ref, oT_ref):
    g = pl.program_id(1)

    def variant(v):
        n = (v + 1) * TK

        def make_bias_fn(h):
            cq = cumT_ref[0, pl.ds(h, 1), :]
            onehot = (lax.broadcasted_iota(I32, (KEY_CHUNK, LANE), 1) == h).astype(F32)

            def bias_fn(lo, s):
                ck = jnp.sum(cum_ref[0, lo:lo + KEY_CHUNK, :] * onehot, axis=1, keepdims=True)
                return s + (cq - ck)

            return bias_fn

        def head_pair(i, carry):
            rows = _pair_rows(i)
            heads = [2 * i, 2 * i + 1]
            outs = _softmax_attend_heads([qT_ref[0, r, :] for r in rows], [k_ref[0, r, :n, :] for r in heads],
                                         [vT_ref[0, r, :n] for r in rows],
                                         [make_bias_fn(g * GROUP + r) for r in heads], v + 1, True)
            for r, o in zip(rows, outs):
                oT_ref[r, :] = o
            return carry

        lax.fori_loop(0, GROUP // 2, head_pair, 0)

    _for_each_query_tile(pl.program_id(2), k_ref.shape[2] // TQ, variant)
    _gate_and_store(oT_ref, gate_ref, o_ref)


def _attn_scratch():
    return [pltpu.VMEM((GROUP * HEAD_DIM, TQ), F32)]


def _gqa_specs(t, gate_blk):
    nq = t // TQ
    gw = GROUP * HEAD_DIM
    in_specs = [
        pl.BlockSpec((1, gw, TQ), lambda b, g, i: (b, g, i)),
        pl.BlockSpec((1, t, HEAD_DIM), lambda b, g, i: (b, 0, g)),
        pl.BlockSpec((1, HEAD_DIM, t), lambda b, g, i: (b, g, 0)),
    ]
    gate_spec = pl.BlockSpec((TQ, gw), lambda b, g, i: (b * nq + i, gate_blk + g))
    out_spec = pl.BlockSpec((TQ, gw), lambda b, g, i: (b * nq + i, g))
    return nq, in_specs, gate_spec, out_spec


def _dsa_attn(qT, k, vT, bias, z):
    b_count, t, _ = k.shape
    gate_blk = (INNER + 2 * KV_WIDTH) // (GROUP * HEAD_DIM)
    nq, in_specs, gate_spec, out_spec = _gqa_specs(t, gate_blk)
    in_specs += [pl.BlockSpec((1, t, TQ), lambda b, g, i: (b, 0, i)), gate_spec]
    return pl.pallas_call(
        _dsa_attn_kernel,
        grid=(b_count, N_KV_HEADS, nq),
        in_specs=in_specs, out_specs=out_spec,
        out_shape=jax.ShapeDtypeStruct((b_count * t, INNER), BF16),
        scratch_shapes=_attn_scratch(),
        compiler_params=_cparams("parallel", "parallel", "parallel"),
        name="dsa_attn",
    )(qT, k, vT, bias, z)


def _moba_attn(qT, k, vT, kmean, z):
    b_count, t, _ = k.shape
    gate_blk = (INNER + 2 * KV_WIDTH) // (GROUP * HEAD_DIM)
    nq, in_specs, gate_spec, out_spec = _gqa_specs(t, gate_blk)
    n_blocks = t // MOBA_BLOCK
    in_specs += [pl.BlockSpec((1, 1, n_blocks, GROUP * HEAD_DIM), lambda b, g, i: (b, g, 0, 0)), gate_spec]
    return pl.pallas_call(
        _moba_attn_kernel,
        grid=(b_count, N_KV_HEADS, nq),
        in_specs=in_specs, out_specs=out_spec,
        out_shape=jax.ShapeDtypeStruct((b_count * t, INNER), BF16),
        scratch_shapes=_attn_scratch() + [pltpu.VMEM((n_blocks, TQ), F32)],
        compiler_params=_cparams("parallel", "parallel", "parallel"),
        name="moba_attn",
    )(qT, k, vT, kmean, z)


def _fox_attn(qT, k, vT, cum, cumT, z):
    b_count, _, t, _ = k.shape
    nq = t // TQ
    gw = GROUP * HEAD_DIM
    gate_blk = 3 * INNER // gw
    return pl.pallas_call(
        _fox_attn_kernel,
        grid=(b_count, N_HEADS // GROUP, nq),
        in_specs=[pl.BlockSpec((1, gw, TQ), lambda b, g, i: (b, g, i)),
                  pl.BlockSpec((1, GROUP, t, HEAD_DIM), lambda b, g, i: (b, g, 0, 0)),
                  pl.BlockSpec((1, gw, t), lambda b, g, i: (b, g, 0)),
                  pl.BlockSpec((1, t, LANE), lambda b, g, i: (b, 0, 0)),
                  pl.BlockSpec((1, LANE, TQ), lambda b, g, i: (b, 0, i)),
                  pl.BlockSpec((TQ, gw), lambda b, g, i: (b * nq + i, gate_blk + g))],
        out_specs=pl.BlockSpec((TQ, gw), lambda b, g, i: (b * nq + i, g)),
        out_shape=jax.ShapeDtypeStruct((b_count * t, INNER), BF16),
        scratch_shapes=_attn_scratch(),
        compiler_params=_cparams("parallel", "parallel", "parallel"),
        name="fox_attn",
    )(qT, k, vT, cum, cumT, z)


def _retention_kernel(q_ref, k_ref, v_ref, gate_ref, cos_ref, sin_ref, lg_ref, gn_ref, o_ref,
                      state_ref, dmask_ref):
    c = RET_CHUNK
    heads = range(RET_PAIR)
    lgs = [lg_ref[h][:, :1] for h in heads]

    @pl.when(pl.program_id(2) == 0)
    def _():
        state_ref[...] = jnp.zeros_like(state_ref)
        diff = (lax.broadcasted_iota(I32, (c, c), 0) - lax.broadcasted_iota(I32, (c, c), 1)).astype(F32)
        for h in heads:
            dmask_ref[h] = jnp.where(diff >= 0, jnp.exp(jnp.maximum(diff, 0.0) * lgs[h]), 0.0)

    cos = cos_ref[...]
    sin = sin_ref[...]
    half = RET_QK_DIM // 2

    def rope(ref, h):
        x1 = ref[:, h * RET_QK_DIM:h * RET_QK_DIM + half]
        x2 = ref[:, h * RET_QK_DIM + half:(h + 1) * RET_QK_DIM]
        return jnp.concatenate([x1 * cos - x2 * sin, x1 * sin + x2 * cos], axis=1)

    i_col = lax.broadcasted_iota(I32, (c, 1), 0).astype(F32)
    vsl = [slice(h * RET_V_DIM, (h + 1) * RET_V_DIM) for h in heads]
    qs = [rope(q_ref, h) for h in heads]
    ks = [rope(k_ref, h) * (RET_QK_DIM ** -0.5) for h in heads]
    inners = [lax.dot_general(qs[h].astype(BF16), ks[h].astype(BF16), (((1,), (1,)), ((), ())),
                              preferred_element_type=F32) for h in heads]
    vs = [v_ref[:, vsl[h]].astype(BF16) for h in heads]
    cross = [jnp.dot((qs[h] * jnp.exp((i_col + 1.0) * lgs[h])).astype(BF16), state_ref[h].astype(BF16),
                     preferred_element_type=F32) for h in heads]
    kdTs = [(ks[h] * jnp.exp((c - 1.0 - i_col) * lgs[h])).T.astype(BF16) for h in heads]
    os_ = [jnp.dot((inners[h] * dmask_ref[h]).astype(BF16), vs[h], preferred_element_type=F32) + cross[h]
           for h in heads]
    for h in heads:
        state_ref[h] = state_ref[h] * jnp.exp(c * lgs[h]) + jnp.dot(kdTs[h], vs[h], preferred_element_type=F32)
    for h in heads:
        o = os_[h]
        mu = jnp.mean(o, axis=-1, keepdims=True)
        var = jnp.mean(jnp.square(o - mu), axis=-1, keepdims=True)
        on = (o - mu) * lax.rsqrt(var + EPS) * gn_ref[:, vsl[h]]
        gate = gate_ref[:, vsl[h]]
        o_ref[:, vsl[h]] = (_silu(gate) * on).astype(BF16)


def _retention(z, b_count, t, gn_g):
    c = RET_CHUNK
    nc = t // c
    half = RET_QK_DIM // 2
    pos = jnp.arange(t, dtype=jnp.int32)
    inv = ROPE_THETA ** (-jnp.arange(0, RET_QK_DIM, 2, dtype=F32) / RET_QK_DIM)
    ang = pos.astype(F32)[:, None] * inv[None, :]
    cos, sin = jnp.cos(ang), jnp.sin(ang)
    log_gamma = jnp.log(1.0 - 2.0 ** (-5.0 - jnp.arange(RET_HEADS, dtype=F32)))
    lg = jnp.broadcast_to(log_gamma[:, None, None], (RET_HEADS, 1, LANE))
    qk_w = RET_HEADS * RET_QK_DIM
    qk_blk = RET_PAIR * RET_QK_DIM
    v_blk = RET_PAIR * RET_V_DIM
    row = lambda b, h, i: b * nc + i
    return pl.pallas_call(
        _retention_kernel,
        grid=(b_count, RET_HEADS // RET_PAIR, nc),
        in_specs=[pl.BlockSpec((c, qk_blk), lambda b, h, i: (row(b, h, i), h)),
                  pl.BlockSpec((c, qk_blk), lambda b, h, i: (row(b, h, i), qk_w // qk_blk + h)),
                  pl.BlockSpec((c, v_blk), lambda b, h, i: (row(b, h, i), 2 * qk_w // v_blk + h)),
                  pl.BlockSpec((c, v_blk), lambda b, h, i: (row(b, h, i), (2 * qk_w + RET_INNER) // v_blk + h)),
                  pl.BlockSpec((c, half), lambda b, h, i: (i, 0)),
                  pl.BlockSpec((c, half), lambda b, h, i: (i, 0)),
                  pl.BlockSpec((RET_PAIR, 1, LANE), lambda b, h, i: (h, 0, 0)),
                  pl.BlockSpec((1, v_blk), lambda b, h, i: (0, h))],
        out_specs=pl.BlockSpec((c, v_blk), lambda b, h, i: (row(b, h, i), h)),
        out_shape=jax.ShapeDtypeStruct((b_count * t, RET_INNER), BF16),
        scratch_shapes=[pltpu.VMEM((RET_PAIR, RET_QK_DIM, RET_V_DIM), F32), pltpu.VMEM((RET_PAIR, c, c), F32)],
        compiler_params=_cparams("parallel", "parallel", "arbitrary"),
        name="retention",
    )(z, z, z, z, cos, sin, lg, gn_g.reshape(1, RET_INNER))


def _tail_proj_kernel(a_ref, wt_ref, o_ref):
    o_ref[...] = lax.dot_general(a_ref[...], wt_ref[...], (((1,), (1,)), ((), ())), preferred_element_type=F32)


def _tail_proj(hn, w_in_t, n_main, name, tm=1024):
    m, k = hn.shape
    wt_tail = jnp.pad(w_in_t[n_main:, :], ((0, LANE - (w_in_t.shape[0] - n_main)), (0, 0))).astype(BF16)
    return pl.pallas_call(
        _tail_proj_kernel,
        grid=(m // tm,),
        in_specs=[pl.BlockSpec((tm, k), lambda i: (i, 0)),
                  pl.BlockSpec((LANE, k), lambda i: (0, 0))],
        out_specs=pl.BlockSpec((tm, LANE), lambda i: (i, 0)),
        out_shape=jax.ShapeDtypeStruct((m, LANE), F32),
        compiler_params=_cparams("parallel"),
        name=name,
    )(hn, wt_tail)


def _dsa_layer(h, hn, b_count, t, w_in, q_g, k_g, w_out, next_g):
    n_main = 2 * INNER + 2 * KV_WIDTH + IDX_HEADS * IDX_DIM
    w_in_t = w_in.T
    z = _in_proj_wt(hn, w_in_t, n_main, 1024, 1024, "dsa_in_proj")
    z_tail = _tail_proj(hn, w_in_t, n_main, "dsa_tail_proj")
    qT, k, vT, qiT, kia, kib, wT = _dsa_prep(z, z_tail, b_count, t, q_g, k_g)
    bias = _dsa_select(kia, kib, qiT, wT)
    gated = _dsa_attn(qT, k, vT, bias, z)
    return _out_proj(gated, w_out.astype(BF16), h, next_g, 512, "dsa_out_proj")


def _moba_layer(h, hn, b_count, t, w_in, q_g, k_g, w_out, next_g):
    z = _in_proj(hn, w_in, w_in.shape[1], 1024, 1024, "moba_in_proj")
    qT, k, vT, kmean = _moba_prep(z, b_count, t, q_g, k_g)
    gated = _moba_attn(qT, k, vT, kmean, z)
    return _out_proj(gated, w_out.astype(BF16), h, next_g, 512, "moba_out_proj")


def _ret_layer(h, hn, b_count, t, w_in, gn_g, w_out, next_g):
    z = _in_proj(hn, w_in, w_in.shape[1], 1024, 1024, "ret_in_proj")
    gated = _retention(z, b_count, t, gn_g)
    return _out_proj(gated, w_out.astype(BF16), h, next_g, 256, "ret_out_proj")


def _fox_layer(h, hn, b_count, t, w_in, f_bias, q_g, k_g, w_out, next_g):
    n_main = 4 * INNER
    w_in_t = w_in.T
    z = _in_proj_wt(hn, w_in_t, n_main, 1024, 1024, "fox_in_proj")
    z_tail = _tail_proj(hn, w_in_t, n_main, "fox_tail_proj")
    qT, k, vT, cum, cumT = _fox_prep(z, z_tail, b_count, t, f_bias, q_g, k_g)
    gated = _fox_attn(qT, k, vT, cum, cumT, z)
    return _out_proj(gated, w_out.astype(BF16), h, next_g, 512, "fox_out_proj")


def kernel(x, a_norm, a_w_in, a_q_norm, a_k_norm, a_w_out, b_norm, b_w_in, b_q_norm, b_k_norm, b_w_out,
           c_norm, c_w_in, c_gn, c_w_out, d_norm, d_w_in, d_f_bias, d_q_norm, d_k_norm, d_w_out):
    b_count, t, d = x.shape
    assert d == D_MODEL and t % TQ == 0 and t // 4 >= DSA_TOPK
    depth = 4
    norms = (a_norm, b_norm, c_norm, d_norm)
    h = x.reshape(b_count * t, d)
    hn = _rmsnorm(h, a_norm[0])
    for i in range(depth):
        m, j = i % 4, i // 4
        next_g = norms[(i + 1) % 4][(i + 1) // 4] if i + 1 < depth else None
        if m == 0:
            h, hn = _dsa_layer(h, hn, b_count, t, a_w_in[j], a_q_norm[j], a_k_norm[j], a_w_out[j], next_g)
        elif m == 1:
            h, hn = _moba_layer(h, hn, b_count, t, b_w_in[j], b_q_norm[j], b_k_norm[j], b_w_out[j], next_g)
        elif m == 2:
            h, hn = _ret_layer(h, hn, b_count, t, c_w_in[j], c_gn[j], c_w_out[j], next_g)
        else:
            h, hn = _fox_layer(h, hn, b_count, t, d_w_in[j], d_f_bias[j], d_q_norm[j], d_k_norm[j], d_w_out[j],
                               next_g)
    return h.reshape(b_count, t, d)
```

```python
import functools

import jax
import jax.numpy as jnp
from jax import lax
from jax.experimental import pallas as pl
from jax.experimental.pallas import tpu as pltpu

F32 = jnp.float32
BF16 = jnp.bfloat16
I32 = jnp.int32

D_MODEL = 2048
HEAD_DIM = 128
N_HEADS = 16
N_KV_HEADS = 4
GROUP = N_HEADS // N_KV_HEADS
INNER = N_HEADS * HEAD_DIM
KV_WIDTH = N_KV_HEADS * HEAD_DIM
IDX_HEADS = 16
IDX_DIM = 64
DSA_TOPK = 256
MOBA_BLOCK = 256
MOBA_TOPK = 3
RET_HEADS = 8
RET_QK_DIM = 256
RET_V_DIM = 512
RET_INNER = RET_HEADS * RET_V_DIM
ROPE_THETA = 10000.0
EPS = 1e-6
NEG = -1e30
INT_MIN = -(2 ** 31)
LOG2E = 1.4426950408889634

LANE = 128
TQ = 256
TK = 256
KEY_CHUNK = 256
RET_CHUNK = 256
RET_PAIR = 2
VMEM_LIMIT = 56 * 1024 * 1024


def _cparams(*sem):
    return pltpu.CompilerParams(dimension_semantics=sem, vmem_limit_bytes=VMEM_LIMIT)


def _rmsnorm_kernel(x_ref, g_ref, o_ref):
    x = x_ref[...]
    ms = jnp.mean(x * x, axis=-1, keepdims=True)
    o_ref[...] = (x * lax.rsqrt(ms + EPS) * g_ref[...]).astype(o_ref.dtype)


def _rmsnorm(x, g, tm=512):
    m, d = x.shape
    return pl.pallas_call(
        _rmsnorm_kernel,
        grid=(m // tm,),
        in_specs=[pl.BlockSpec((tm, d), lambda i: (i, 0)),
                  pl.BlockSpec((1, d), lambda i: (0, 0))],
        out_specs=pl.BlockSpec((tm, d), lambda i: (i, 0)),
        out_shape=jax.ShapeDtypeStruct((m, d), BF16),
        compiler_params=_cparams("parallel"),
        name="rmsnorm",
    )(x, g.reshape(1, d))


def _in_proj_kernel(a_ref, w_ref, o_ref, wb_ref):
    @pl.when(pl.program_id(1) == 0)
    def _():
        wb_ref[...] = w_ref[...].astype(BF16)

    o_ref[...] = jnp.dot(a_ref[...], wb_ref[...], preferred_element_type=F32)


def _serpentine(n_rows):
    return lambda j, i: i + (j % 2) * (n_rows - 1 - 2 * i)


def _in_proj(a, w, n, tm, tn, name):
    m, k = a.shape
    row = _serpentine(m // tm)
    return pl.pallas_call(
        _in_proj_kernel,
        grid=(n // tn, m // tm),
        in_specs=[pl.BlockSpec((tm, k), lambda j, i: (row(j, i), 0)),
                  pl.BlockSpec((k, tn), lambda j, i: (0, j))],
        out_specs=pl.BlockSpec((tm, tn), lambda j, i: (row(j, i), j)),
        out_shape=jax.ShapeDtypeStruct((m, n), F32),
        scratch_shapes=[pltpu.VMEM((k, tn), BF16)],
        compiler_params=_cparams("parallel", "arbitrary"),
        name=name,
    )(a, w)


def _in_proj_wt_kernel(a_ref, wt_ref, o_ref, wb_ref):
    @pl.when(pl.program_id(1) == 0)
    def _():
        wb_ref[...] = wt_ref[...].T.astype(BF16)

    o_ref[...] = jnp.dot(a_ref[...], wb_ref[...], preferred_element_type=F32)


def _in_proj_wt(a, wt, n, tm, tn, name):
    m, k = a.shape
    row = _serpentine(m // tm)
    return pl.pallas_call(
        _in_proj_wt_kernel,
        grid=(n // tn, m // tm),
        in_specs=[pl.BlockSpec((tm, k), lambda j, i: (row(j, i), 0)),
                  pl.BlockSpec((tn, k), lambda j, i: (j, 0))],
        out_specs=pl.BlockSpec((tm, tn), lambda j, i: (row(j, i), j)),
        out_shape=jax.ShapeDtypeStruct((m, n), F32),
        scratch_shapes=[pltpu.VMEM((k, tn), BF16)],
        compiler_params=_cparams("parallel", "arbitrary"),
        name=name,
    )(a, wt)


def _out_proj_kernel(a_ref, w_ref, r_ref, o_ref):
    o_ref[...] = r_ref[...] + jnp.dot(a_ref[...], w_ref[...], preferred_element_type=F32)


def _out_proj_norm_kernel(a_ref, w_ref, r_ref, g_ref, o_ref, hn_ref):
    h = r_ref[...] + jnp.dot(a_ref[...], w_ref[...], preferred_element_type=F32)
    o_ref[...] = h
    ms = jnp.mean(h * h, axis=-1, keepdims=True)
    hn_ref[...] = (h * lax.rsqrt(ms + EPS) * g_ref[...]).astype(BF16)


def _out_proj(a, w, res, next_g, tm, name):
    m, k = a.shape
    n = w.shape[1]
    in_specs = [pl.BlockSpec((tm, k), lambda i: (i, 0)),
                pl.BlockSpec((k, n), lambda i: (0, 0)),
                pl.BlockSpec((tm, n), lambda i: (i, 0))]
    out_specs = [pl.BlockSpec((tm, n), lambda i: (i, 0))]
    out_shape = [jax.ShapeDtypeStruct((m, n), F32)]
    args = [a, w, res]
    if next_g is not None:
        in_specs.append(pl.BlockSpec((1, n), lambda i: (0, 0)))
        out_specs.append(pl.BlockSpec((tm, n), lambda i: (i, 0)))
        out_shape.append(jax.ShapeDtypeStruct((m, n), BF16))
        args.append(next_g.reshape(1, n))
    out = pl.pallas_call(
        _out_proj_kernel if next_g is None else _out_proj_norm_kernel,
        grid=(m // tm,),
        in_specs=in_specs, out_specs=out_specs, out_shape=out_shape,
        compiler_params=_cparams("parallel"),
        name=name,
    )(*args)
    return (out[0], out[1]) if next_g is not None else (out[0], None)


def _rope_tables(t, d, reps):
    pos = jnp.arange(t, dtype=jnp.int32)
    inv = ROPE_THETA ** (-jnp.arange(0, d, 2, dtype=F32) / d)
    ang = pos.astype(F32)[:, None] * inv[None, :]
    cos, sin = jnp.cos(ang), jnp.sin(ang)
    cos_t = jnp.tile(jnp.concatenate([cos, cos], axis=-1), (1, reps))
    sin_t = jnp.tile(jnp.concatenate([-sin, sin], axis=-1), (1, reps))
    return cos_t, sin_t


def _rope_tables_t(t, d):
    pos = jnp.arange(t, dtype=jnp.int32)
    inv = ROPE_THETA ** (-jnp.arange(0, d, 2, dtype=F32) / d)
    ang = pos.astype(F32)[:, None] * inv[None, :]
    return jnp.cos(ang).T, jnp.sin(ang).T


def _silu(x):
    hx = 0.5 * x
    return hx + hx * jnp.tanh(hx)


def _head_norm(x, g):
    ms = jnp.mean(x * x, axis=-1, keepdims=True)
    return x * lax.rsqrt(ms + EPS) * g


def _rope128(y, cos, sin):
    return y * cos + pltpu.roll(y, 64, 1) * sin


def _rope_rows(y, cos_t, sin_t):
    half = y.shape[0] // 2
    y1, y2 = y[:half], y[half:]
    return jnp.concatenate([y1 * cos_t - y2 * sin_t, y1 * sin_t + y2 * cos_t], axis=0)


def _qkv_prep_body(q_ref, k_ref, v_ref, cos_ref, sin_ref, cos_t_ref, sin_t_ref, qg_col_ref, kg_ref,
                   qT_ref, ko_ref, vT_ref, *, n_kv, rope):
    tm = q_ref.shape[0]
    kg = kg_ref[...]
    scale = HEAD_DIM ** -0.5 * LOG2E
    if rope:
        cos = cos_ref[...]
        sin = sin_ref[...]
        cos_t = cos_t_ref[...]
        sin_t = sin_t_ref[...]
    qg_t = jnp.broadcast_to(qg_col_ref[...], (HEAD_DIM, tm)) * scale
    for h in range(N_HEADS):
        sl = slice(h * HEAD_DIM, (h + 1) * HEAD_DIM)
        x = q_ref[:, sl].T
        ms = jnp.mean(x * x, axis=0, keepdims=True)
        y = x * lax.rsqrt(ms + EPS) * qg_t
        if rope:
            y = _rope_rows(y, cos_t, sin_t)
        qT_ref[0, sl, :] = y.astype(BF16)
    k_out = []
    for g in range(n_kv):
        sl = slice(g * HEAD_DIM, (g + 1) * HEAD_DIM)
        y = _head_norm(k_ref[:, sl], kg)
        if rope:
            y = _rope128(y, cos, sin)
        if ko_ref.ndim == 4:
            ko_ref[0, g] = y.astype(BF16)
        else:
            ko_ref[0, :, sl] = y.astype(BF16)
        vT_ref[0, sl, :] = v_ref[:, sl].T.astype(BF16)
        k_out.append(y)
    return k_out


def _dsa_prep_kernel(q_ref, k_ref, v_ref, qi_ref, tail_ref, cos_ref, sin_ref, cos_t_ref, sin_t_ref,
                     cos64_ref, sin64_ref, cos32_t_ref, sin32_t_ref,
                     qg_ref, kg_ref, qT_ref, ko_ref, vT_ref, qiT_ref, kia_ref, kib_ref, wT_ref):
    _qkv_prep_body(q_ref, k_ref, v_ref, cos_ref, sin_ref, cos_t_ref, sin_t_ref, qg_ref, kg_ref,
                   qT_ref, ko_ref, vT_ref, n_kv=N_KV_HEADS, rope=True)
    tm = tail_ref.shape[0]
    lane = lax.broadcasted_iota(I32, (tm, LANE), 1)
    first_half = (lane % IDX_DIM) < (IDX_DIM // 2)
    c64 = cos64_ref[...]
    s64 = sin64_ref[...]

    def rope64(x):
        rot = jnp.where(first_half, pltpu.roll(x, LANE - IDX_DIM // 2, 1), pltpu.roll(x, IDX_DIM // 2, 1))
        return x * c64 + rot * s64

    c32_t = cos32_t_ref[...]
    s32_t = sin32_t_ref[...]
    for j in range(IDX_HEADS * IDX_DIM // LANE):
        sl = slice(j * LANE, (j + 1) * LANE)
        x = qi_ref[:, sl].T * (IDX_DIM ** -0.5)
        y = jnp.concatenate([_rope_rows(x[:IDX_DIM], c32_t, s32_t), _rope_rows(x[IDX_DIM:], c32_t, s32_t)], axis=0)
        qiT_ref[0, sl, :] = y.astype(BF16)
    tail = tail_ref[...]
    ka = jnp.where(lane < IDX_DIM, rope64(tail), 0.0)
    kia_ref[0] = ka.astype(BF16)
    kib_ref[0] = pltpu.roll(ka, IDX_DIM, 1).astype(BF16)
    wT_ref[0] = (tail * (IDX_HEADS ** -0.5)).T


def _moba_prep_kernel(q_ref, k_ref, v_ref, cos_ref, sin_ref, cos_t_ref, sin_t_ref, qg_ref, kg_ref,
                      qT_ref, ko_ref, vT_ref, kmean_ref):
    k_out = _qkv_prep_body(q_ref, k_ref, v_ref, cos_ref, sin_ref, cos_t_ref, sin_t_ref, qg_ref, kg_ref,
                           qT_ref, ko_ref, vT_ref, n_kv=N_KV_HEADS, rope=True)
    for g in range(N_KV_HEADS):
        km = jnp.mean(k_out[g], axis=0, keepdims=True)
        kmean_ref[0, g, pl.ds(pl.program_id(1), 1), :] = jnp.concatenate([km] * GROUP, axis=1)


def _fox_prep_kernel(q_ref, k_ref, v_ref, f_ref, fb_ref, qg_ref, kg_ref,
                     qT_ref, ko_ref, vT_ref, cum_ref, cumT_ref, carry_ref):
    _qkv_prep_body(q_ref, k_ref, v_ref, None, None, None, None, qg_ref, kg_ref, qT_ref, ko_ref, vT_ref,
                   n_kv=N_HEADS, rope=False)

    @pl.when(pl.program_id(1) == 0)
    def _():
        carry_ref[...] = jnp.zeros_like(carry_ref)

    tm = f_ref.shape[0]
    x = f_ref[...] + fb_ref[...]
    lf = jnp.minimum(x, 0.0) - jnp.log(1.0 + jnp.exp(-jnp.abs(x)))
    hi = lf.astype(BF16)
    r1 = lf - hi.astype(F32)
    lo = r1.astype(BF16)
    lo2 = (r1 - lo.astype(F32)).astype(BF16)
    row = lax.broadcasted_iota(I32, (tm, tm), 0)
    col = lax.broadcasted_iota(I32, (tm, tm), 1)
    tri = (col <= row).astype(BF16)
    parts = jnp.dot(tri, jnp.concatenate([hi, lo, lo2], axis=1), preferred_element_type=F32)
    cum = parts[:, :LANE] + parts[:, LANE:2 * LANE] + parts[:, 2 * LANE:] + carry_ref[...]
    carry_ref[...] = cum[tm - 1:tm, :]
    cum2 = cum * LOG2E
    cum_ref[0] = cum2
    cumT_ref[0] = cum2.T


def _prep_specs(b_count, t, tm, z_q_blk, z_k_blk, z_v_blk, kv_width):
    nt = t // tm
    row = lambda b, i: b * nt + i
    in_specs = [
        pl.BlockSpec((tm, INNER), lambda b, i: (row(b, i), z_q_blk)),
        pl.BlockSpec((tm, kv_width), lambda b, i: (row(b, i), z_k_blk)),
        pl.BlockSpec((tm, kv_width), lambda b, i: (row(b, i), z_v_blk)),
    ]
    out_specs = [
        pl.BlockSpec((1, INNER, tm), lambda b, i: (b, 0, i)),
        pl.BlockSpec((1, tm, kv_width), lambda b, i: (b, i, 0)),
        pl.BlockSpec((1, kv_width, tm), lambda b, i: (b, 0, i)),
    ]
    out_shape = [
        jax.ShapeDtypeStruct((b_count, INNER, t), BF16),
        jax.ShapeDtypeStruct((b_count, t, kv_width), BF16),
        jax.ShapeDtypeStruct((b_count, kv_width, t), BF16),
    ]
    return row, in_specs, out_specs, out_shape


def _dsa_prep(z, z_tail, b_count, t, q_g, k_g, tm=256):
    row, in_specs, out_specs, out_shape = _prep_specs(b_count, t, tm, 0, INNER // KV_WIDTH,
                                                      INNER // KV_WIDTH + 1, KV_WIDTH)
    qi_w = IDX_HEADS * IDX_DIM
    qi_off = 2 * INNER + 2 * KV_WIDTH
    cos, sin = _rope_tables(t, HEAD_DIM, 1)
    cos64, sin64 = _rope_tables(t, IDX_DIM, LANE // IDX_DIM)
    cos_t, sin_t = _rope_tables_t(t, HEAD_DIM)
    cos32_t, sin32_t = _rope_tables_t(t, IDX_DIM)
    tab = pl.BlockSpec((tm, LANE), lambda b, i: (i, 0))
    tab_t = pl.BlockSpec((HEAD_DIM // 2, tm), lambda b, i: (0, i))
    tab32_t = pl.BlockSpec((IDX_DIM // 2, tm), lambda b, i: (0, i))
    gain = pl.BlockSpec((1, HEAD_DIM), lambda b, i: (0, 0))
    gain_col = pl.BlockSpec((HEAD_DIM, 1), lambda b, i: (0, 0))
    in_specs += [
        pl.BlockSpec((tm, qi_w), lambda b, i: (row(b, i), qi_off // qi_w)),
        pl.BlockSpec((tm, LANE), lambda b, i: (row(b, i), 0)),
        tab, tab, tab_t, tab_t, tab, tab, tab32_t, tab32_t, gain_col, gain,
    ]
    out_specs += [
        pl.BlockSpec((1, qi_w, tm), lambda b, i: (b, 0, i)),
        pl.BlockSpec((1, tm, LANE), lambda b, i: (b, i, 0)),
        pl.BlockSpec((1, tm, LANE), lambda b, i: (b, i, 0)),
        pl.BlockSpec((1, LANE, tm), lambda b, i: (b, 0, i)),
    ]
    out_shape += [
        jax.ShapeDtypeStruct((b_count, qi_w, t), BF16),
        jax.ShapeDtypeStruct((b_count, t, LANE), BF16),
        jax.ShapeDtypeStruct((b_count, t, LANE), BF16),
        jax.ShapeDtypeStruct((b_count, LANE, t), F32),
    ]
    return pl.pallas_call(
        _dsa_prep_kernel,
        grid=(b_count, t // tm),
        in_specs=in_specs, out_specs=out_specs, out_shape=out_shape,
        compiler_params=_cparams("parallel", "parallel"),
        name="dsa_prep",
    )(z, z, z, z, z_tail, cos, sin, cos_t, sin_t, cos64, sin64, cos32_t, sin32_t,
      q_g.reshape(HEAD_DIM, 1), k_g.reshape(1, HEAD_DIM))


def _moba_prep(z, b_count, t, q_g, k_g):
    tm = MOBA_BLOCK
    row, in_specs, out_specs, out_shape = _prep_specs(b_count, t, tm, 0, INNER // KV_WIDTH,
                                                      INNER // KV_WIDTH + 1, KV_WIDTH)
    cos, sin = _rope_tables(t, HEAD_DIM, 1)
    cos_t, sin_t = _rope_tables_t(t, HEAD_DIM)
    tab = pl.BlockSpec((tm, LANE), lambda b, i: (i, 0))
    tab_t = pl.BlockSpec((HEAD_DIM // 2, tm), lambda b, i: (0, i))
    gain = pl.BlockSpec((1, HEAD_DIM), lambda b, i: (0, 0))
    gain_col = pl.BlockSpec((HEAD_DIM, 1), lambda b, i: (0, 0))
    in_specs += [tab, tab, tab_t, tab_t, gain_col, gain]
    out_specs += [pl.BlockSpec((1, N_KV_HEADS, t // tm, GROUP * HEAD_DIM), lambda b, i: (b, 0, 0, 0))]
    out_shape += [jax.ShapeDtypeStruct((b_count, N_KV_HEADS, t // tm, GROUP * HEAD_DIM), F32)]
    return pl.pallas_call(
        _moba_prep_kernel,
        grid=(b_count, t // tm),
        in_specs=in_specs, out_specs=out_specs, out_shape=out_shape,
        compiler_params=_cparams("parallel", "arbitrary"),
        name="moba_prep",
    )(z, z, z, cos, sin, cos_t, sin_t, q_g.reshape(HEAD_DIM, 1), k_g.reshape(1, HEAD_DIM))


def _fox_prep(z, z_tail, b_count, t, f_bias, q_g, k_g, tm=256):
    row, in_specs, out_specs, out_shape = _prep_specs(b_count, t, tm, 0, 1, 2, INNER)
    out_specs[1] = pl.BlockSpec((1, N_HEADS, tm, HEAD_DIM), lambda b, i: (b, 0, i, 0))
    out_shape[1] = jax.ShapeDtypeStruct((b_count, N_HEADS, t, HEAD_DIM), BF16)
    gain = pl.BlockSpec((1, HEAD_DIM), lambda b, i: (0, 0))
    fb = jnp.pad(f_bias.reshape(1, N_HEADS), ((0, 0), (0, LANE - N_HEADS)))
    in_specs += [
        pl.BlockSpec((tm, LANE), lambda b, i: (row(b, i), 0)),
        pl.BlockSpec((1, LANE), lambda b, i: (0, 0)),
        pl.BlockSpec((HEAD_DIM, 1), lambda b, i: (0, 0)), gain,
    ]
    out_specs += [
        pl.BlockSpec((1, tm, LANE), lambda b, i: (b, i, 0)),
        pl.BlockSpec((1, LANE, tm), lambda b, i: (b, 0, i)),
    ]
    out_shape += [
        jax.ShapeDtypeStruct((b_count, t, LANE), F32),
        jax.ShapeDtypeStruct((b_count, LANE, t), F32),
    ]
    return pl.pallas_call(
        _fox_prep_kernel,
        grid=(b_count, t // tm),
        in_specs=in_specs, out_specs=out_specs, out_shape=out_shape,
        scratch_shapes=[pltpu.VMEM((1, LANE), F32)],
        compiler_params=_cparams("parallel", "arbitrary"),
        name="fox_prep",
    )(z, z, z, z_tail, fb, q_g.reshape(HEAD_DIM, 1), k_g.reshape(1, HEAD_DIM))


def _dsa_select_kernel(kia_ref, kib_ref, qiT_ref, wT_ref, bias_ref, key_ref, ans_ref, n_ge_ref):
    qb = pl.program_id(1)
    n_kt = key_ref.shape[0] // TK
    w = wT_ref[0]
    s_iota = lax.broadcasted_iota(I32, (TK, TQ), 0)
    t_iota = lax.broadcasted_iota(I32, (TK, TQ), 1)

    for kt in range(n_kt):
        rows = slice(kt * TK, (kt + 1) * TK)

        @pl.when(kt <= qb)
        def _():
            ka = kia_ref[0, rows, :]
            kb = kib_ref[0, rows, :]
            score = jnp.zeros((TK, TQ), F32)
            for j in range(IDX_HEADS // 2):
                qp = qiT_ref[0, j * LANE:(j + 1) * LANE, :]
                ra = jnp.maximum(jnp.dot(ka, qp, preferred_element_type=F32), 0.0)
                rb = jnp.maximum(jnp.dot(kb, qp, preferred_element_type=F32), 0.0)
                wa = w[IDX_DIM + 2 * j:IDX_DIM + 2 * j + 1, :]
                wb = w[IDX_DIM + 2 * j + 1:IDX_DIM + 2 * j + 2, :]
                score = score + (ra * wa + rb * wb)
            bits = lax.bitcast_convert_type(score + 0.0, I32)
            key = bits ^ ((bits >> 31) & 0x7FFFFFFF)
            causal = (kt * TK + s_iota) <= (qb * TQ + t_iota)
            key_ref[rows, :] = jnp.where(causal, key, INT_MIN)

        @pl.when(kt > qb)
        def _():
            key_ref[rows, :] = jnp.full((TK, TQ), INT_MIN, I32)

    def variant(v):
        n = (v + 1) * TK
        kf = float(DSA_TOPK)

        def count_ge(cand):
            return jnp.sum((key_ref[:n, :] >= cand).astype(F32), axis=0, keepdims=True)

        def bit_step(i, ans):
            cand = ans | (jnp.int32(1) << (30 - i))
            return jnp.where(count_ge(cand) >= kf, cand, ans)

        sign = jnp.where(count_ge(jnp.zeros((1, TQ), I32)) >= kf, 0, INT_MIN).astype(I32)
        ans = lax.fori_loop(0, 31, bit_step, sign)
        ans_ref[...] = ans
        n_ge_ref[...] = count_ge(ans)

    _for_each_query_tile(qb, n_kt, variant)
    ans = ans_ref[...]

    for kt in range(n_kt):
        rows = slice(kt * TK, (kt + 1) * TK)
        k = key_ref[rows, :]
        sel = (k >= ans) & (k > INT_MIN)
        bias_ref[0, rows, :] = jnp.where(sel, 0.0, NEG)

    @pl.when(jnp.max(n_ge_ref[...]) > float(DSA_TOPK))
    def _():
        n_gt = jnp.zeros((1, TQ), F32)
        for kt in range(n_kt):
            n_gt = n_gt + jnp.sum((key_ref[kt * TK:(kt + 1) * TK, :] > ans).astype(F32), axis=0, keepdims=True)
        n_tied_kept = float(DSA_TOPK) - n_gt
        tri = (lax.broadcasted_iota(I32, (TK, TK), 1) <= lax.broadcasted_iota(I32, (TK, TK), 0)).astype(BF16)
        seen = jnp.zeros((1, TQ), F32)
        for kt in range(n_kt):
            rows = slice(kt * TK, (kt + 1) * TK)
            k = key_ref[rows, :]
            tied = (k == ans) & (k > INT_MIN)
            rank = jnp.dot(tri, tied.astype(F32).astype(BF16), preferred_element_type=F32) + seen
            keep = (k > ans) | (tied & (rank <= n_tied_kept))
            bias_ref[0, rows, :] = jnp.where(keep, 0.0, NEG)
            seen = rank[TK - 1:TK, :]


def _dsa_select(kia, kib, qiT, wT):
    b_count, t, _ = kia.shape
    return pl.pallas_call(
        _dsa_select_kernel,
        grid=(b_count, t // TQ),
        in_specs=[pl.BlockSpec((1, t, LANE), lambda b, i: (b, 0, 0)),
                  pl.BlockSpec((1, t, LANE), lambda b, i: (b, 0, 0)),
                  pl.BlockSpec((1, IDX_HEADS * IDX_DIM, TQ), lambda b, i: (b, 0, i)),
                  pl.BlockSpec((1, LANE, TQ), lambda b, i: (b, 0, i))],
        out_specs=pl.BlockSpec((1, t, TQ), lambda b, i: (b, 0, i)),
        out_shape=jax.ShapeDtypeStruct((b_count, t, t), F32),
        scratch_shapes=[pltpu.VMEM((t, TQ), I32), pltpu.VMEM((1, TQ), I32), pltpu.VMEM((1, TQ), F32)],
        compiler_params=_cparams("parallel", "parallel"),
        name="dsa_select",
    )(kia, kib, qiT, wT)


def _softmax_attend_heads(qs, k_alls, vt_alls, bias_fns, n_tiles, causal_last):
    heads = range(len(qs))
    s_alls = [jnp.dot(k_alls[i], qs[i], preferred_element_type=F32) for i in heads]
    tiles = [[] for _ in heads]
    diag = (n_tiles - 1) * TK
    for i in heads:
        for lo in range(0, n_tiles * TK, KEY_CHUNK):
            sj = bias_fns[i](lo, s_alls[i][lo:lo + KEY_CHUNK])
            if causal_last and lo >= diag:
                s_iota = lax.broadcasted_iota(I32, (KEY_CHUNK, TQ), 0) + (lo - diag)
                sj = jnp.where(s_iota <= lax.broadcasted_iota(I32, (KEY_CHUNK, TQ), 1), sj, NEG)
            tiles[i].append(sj)
    ms = [functools.reduce(jnp.maximum, [jnp.max(sj, axis=0, keepdims=True) for sj in tiles[i]]) for i in heads]
    outs = []
    for i in heads:
        ps = [jnp.exp2(sj - ms[i]) for sj in tiles[i]]
        l = functools.reduce(jnp.add, [jnp.sum(p, axis=0, keepdims=True) for p in ps])
        ps = [p.astype(BF16) for p in ps]
        p_all = jnp.concatenate(ps, axis=0)
        outs.append(jnp.dot(vt_alls[i], p_all, preferred_element_type=F32) * (1.0 / l))
    return outs


def _for_each_query_tile(qb, n_q, body):
    for v in range(n_q):
        pl.when(qb == v)(functools.partial(body, v))


def _pair_rows(i):
    base = pl.multiple_of(i * 2 * HEAD_DIM, 2 * HEAD_DIM)
    return [pl.ds(base, HEAD_DIM), pl.ds(base + HEAD_DIM, HEAD_DIM)]

def _gate_and_store(oT_ref, gate_ref, o_ref):
    for r in range(GROUP):
        sl = slice(r * HEAD_DIM, (r + 1) * HEAD_DIM)
        gate = gate_ref[:, sl]
        o_ref[:, sl] = (_silu(gate) * oT_ref[sl, :].T).astype(BF16)


def _dsa_attn_kernel(qT_ref, k_ref, vT_ref, bias_ref, gate_ref, o_ref, oT_ref):
    def variant(v):
        n = (v + 1) * TK

        def bias_fn(lo, s):
            return s + bias_ref[0, lo:lo + KEY_CHUNK, :]

        def head_pair(i, carry):
            rows = _pair_rows(i)
            outs = _softmax_attend_heads([qT_ref[0, r, :] for r in rows], [k_ref[0, :n, :]] * 2,
                                         [vT_ref[0, :, :n]] * 2, [bias_fn] * 2, v + 1, False)
            for r, o in zip(rows, outs):
                oT_ref[r, :] = o
            return carry

        lax.fori_loop(0, GROUP // 2, head_pair, 0, unroll=True)

    _for_each_query_tile(pl.program_id(2), k_ref.shape[1] // TQ, variant)
    _gate_and_store(oT_ref, gate_ref, o_ref)


def _moba_attn_kernel(qT_ref, k_ref, vT_ref, kmean_ref, gate_ref, o_ref, oT_ref, selb_ref):
    own = pl.program_id(2)
    n_blocks = kmean_ref.shape[2]

    km = kmean_ref[0, 0].astype(BF16)
    gs = jnp.dot(km, qT_ref[0], preferred_element_type=F32)
    blk = lax.broadcasted_iota(I32, (n_blocks, TQ), 0)
    past = blk < own
    gs = jnp.where(past, gs, NEG)
    rank = jnp.zeros((n_blocks, TQ), F32)
    for m in range(n_blocks):
        row = gs[m:m + 1, :]
        rank = rank + ((row > gs) | ((row == gs) & (blk > m))).astype(F32)
    sel = rank < float(min(MOBA_TOPK, n_blocks - 1))
    selb_ref[...] = jnp.where(sel & past, 0.0, NEG)

    def variant(v):
        n = (v + 1) * TK

        def bias_fn(lo, s):
            j = lo // MOBA_BLOCK
            return s if j == v else s + selb_ref[j:j + 1, :]

        def head_pair(i, carry):
            rows = _pair_rows(i)
            outs = _softmax_attend_heads([qT_ref[0, r, :] for r in rows], [k_ref[0, :n, :]] * 2,
                                         [vT_ref[0, :, :n]] * 2, [bias_fn] * 2, v + 1, True)
            for r, o in zip(rows, outs):
                oT_ref[r, :] = o
            return carry

        lax.fori_loop(0, GROUP // 2, head_pair, 0, unroll=True)

    _for_each_query_tile(own, n_blocks, variant)
    _gate_and_store(oT_ref, gate_ref, o_ref)


def _fox_attn_kernel(qT_ref, k_ref, vT_ref, cum_ref, cumT_ref, gate_ref, o_ref, oT_ref):
    g = pl.program_id(1)

    def variant(v):
        n = (v + 1) * TK

        def make_bias_fn(h):
            cq = cumT_ref[0, pl.ds(h, 1), :]
            onehot = (lax.broadcasted_iota(I32, (KEY_CHUNK, LANE), 1) == h).astype(F32)

            def bias_fn(lo, s):
                ck = jnp.sum(cum_ref[0, lo:lo + KEY_CHUNK, :] * onehot, axis=1, keepdims=True)
                return s + (cq - ck)

            return bias_fn

        def head_pair(i, carry):
            rows = _pair_rows(i)
            heads = [2 * i, 2 * i + 1]
            outs = _softmax_attend_heads([qT_ref[0, r, :] for r in rows], [k_ref[0, r, :n, :] for r in heads],
                                         [vT_ref[0, r, :n] for r in rows],
                                         [make_bias_fn(g * GROUP + r) for r in heads], v + 1, True)
            for r, o in zip(rows, outs):
                oT_ref[r, :] = o
            return carry

        lax.fori_loop(0, GROUP // 2, head_pair, 0, unroll=True)

    _for_each_query_tile(pl.program_id(2), k_ref.shape[2] // TQ, variant)
    _gate_and_store(oT_ref, gate_ref, o_ref)


def _attn_scratch():
    return [pltpu.VMEM((GROUP * HEAD_DIM, TQ), F32)]


def _gqa_specs(t, gate_blk):
    nq = t // TQ
    gw = GROUP * HEAD_DIM
    in_specs = [
        pl.BlockSpec((1, gw, TQ), lambda b, g, i: (b, g, i)),
        pl.BlockSpec((1, t, HEAD_DIM), lambda b, g, i: (b, 0, g)),
        pl.BlockSpec((1, HEAD_DIM, t), lambda b, g, i: (b, g, 0)),
    ]
    gate_spec = pl.BlockSpec((TQ, gw), lambda b, g, i: (b * nq + i, gate_blk + g))
    out_spec = pl.BlockSpec((TQ, gw), lambda b, g, i: (b * nq + i, g))
    return nq, in_specs, gate_spec, out_spec


def _dsa_attn(qT, k, vT, bias, z):
    b_count, t, _ = k.shape
    gate_blk = (INNER + 2 * KV_WIDTH) // (GROUP * HEAD_DIM)
    nq, in_specs, gate_spec, out_spec = _gqa_specs(t, gate_blk)
    in_specs += [pl.BlockSpec((1, t, TQ), lambda b, g, i: (b, 0, i)), gate_spec]
    return pl.pallas_call(
        _dsa_attn_kernel,
        grid=(b_count, N_KV_HEADS, nq),
        in_specs=in_specs, out_specs=out_spec,
        out_shape=jax.ShapeDtypeStruct((b_count * t, INNER), BF16),
        scratch_shapes=_attn_scratch(),
        compiler_params=_cparams("parallel", "parallel", "parallel"),
        name="dsa_attn",
    )(qT, k, vT, bias, z)


def _moba_attn(qT, k, vT, kmean, z):
    b_count, t, _ = k.shape
    gate_blk = (INNER + 2 * KV_WIDTH) // (GROUP * HEAD_DIM)
    nq, in_specs, gate_spec, out_spec = _gqa_specs(t, gate_blk)
    n_blocks = t // MOBA_BLOCK
    in_specs += [pl.BlockSpec((1, 1, n_blocks, GROUP * HEAD_DIM), lambda b, g, i: (b, g, 0, 0)), gate_spec]
    return pl.pallas_call(
        _moba_attn_kernel,
        grid=(b_count, N_KV_HEADS, nq),
        in_specs=in_specs, out_specs=out_spec,
        out_shape=jax.ShapeDtypeStruct((b_count * t, INNER), BF16),
        scratch_shapes=_attn_scratch() + [pltpu.VMEM((n_blocks, TQ), F32)],
        compiler_params=_cparams("parallel", "parallel", "parallel"),
        name="moba_attn",
    )(qT, k, vT, kmean, z)


def _fox_attn(qT, k, vT, cum, cumT, z):
    b_count, _, t, _ = k.shape
    nq = t // TQ
    gw = GROUP * HEAD_DIM
    gate_blk = 3 * INNER // gw
    return pl.pallas_call(
        _fox_attn_kernel,
        grid=(b_count, N_HEADS // GROUP, nq),
        in_specs=[pl.BlockSpec((1, gw, TQ), lambda b, g, i: (b, g, i)),
                  pl.BlockSpec((1, GROUP, t, HEAD_DIM), lambda b, g, i: (b, g, 0, 0)),
                  pl.BlockSpec((1, gw, t), lambda b, g, i: (b, g, 0)),
                  pl.BlockSpec((1, t, LANE), lambda b, g, i: (b, 0, 0)),
                  pl.BlockSpec((1, LANE, TQ), lambda b, g, i: (b, 0, i)),
                  pl.BlockSpec((TQ, gw), lambda b, g, i: (b * nq + i, gate_blk + g))],
        out_specs=pl.BlockSpec((TQ, gw), lambda b, g, i: (b * nq + i, g)),
        out_shape=jax.ShapeDtypeStruct((b_count * t, INNER), BF16),
        scratch_shapes=_attn_scratch(),
        compiler_params=_cparams("parallel", "parallel", "parallel"),
        name="fox_attn",
    )(qT, k, vT, cum, cumT, z)


def _retention_kernel(q_ref, k_ref, v_ref, gate_ref, cos_ref, sin_ref, lg_ref, gn_ref, o_ref,
                      state_ref, dmask_ref):
    c = RET_CHUNK
    heads = range(RET_PAIR)
    lgs = [lg_ref[h][:, :1] for h in heads]

    @pl.when(pl.program_id(2) == 0)
    def _():
        state_ref[...] = jnp.zeros_like(state_ref)
        diff = (lax.broadcasted_iota(I32, (c, c), 0) - lax.broadcasted_iota(I32, (c, c), 1)).astype(F32)
        for h in heads:
            dmask_ref[h] = jnp.where(diff >= 0, jnp.exp(jnp.maximum(diff, 0.0) * lgs[h]), 0.0)

    cos = cos_ref[...]
    sin = sin_ref[...]
    half = RET_QK_DIM // 2

    def rope(ref, h):
        x1 = ref[:, h * RET_QK_DIM:h * RET_QK_DIM + half]
        x2 = ref[:, h * RET_QK_DIM + half:(h + 1) * RET_QK_DIM]
        return jnp.concatenate([x1 * cos - x2 * sin, x1 * sin + x2 * cos], axis=1)

    i_col = lax.broadcasted_iota(I32, (c, 1), 0).astype(F32)
    vsl = [slice(h * RET_V_DIM, (h + 1) * RET_V_DIM) for h in heads]
    qs = [rope(q_ref, h) for h in heads]
    ks = [rope(k_ref, h) * (RET_QK_DIM ** -0.5) for h in heads]
    inners = [lax.dot_general(qs[h].astype(BF16), ks[h].astype(BF16), (((1,), (1,)), ((), ())),
                              preferred_element_type=F32) for h in heads]
    vs = [v_ref[:, vsl[h]].astype(BF16) for h in heads]
    cross = [jnp.dot((qs[h] * jnp.exp((i_col + 1.0) * lgs[h])).astype(BF16), state_ref[h].astype(BF16),
                     preferred_element_type=F32) for h in heads]
    kdTs = [(ks[h] * jnp.exp((c - 1.0 - i_col) * lgs[h])).T.astype(BF16) for h in heads]
    os_ = [jnp.dot((inners[h] * dmask_ref[h]).astype(BF16), vs[h], preferred_element_type=F32) + cross[h]
           for h in heads]
    for h in heads:
        state_ref[h] = state_ref[h] * jnp.exp(c * lgs[h]) + jnp.dot(kdTs[h], vs[h], preferred_element_type=F32)
    for h in heads:
        o = os_[h]
        mu = jnp.mean(o, axis=-1, keepdims=True)
        var = jnp.mean(jnp.square(o - mu), axis=-1, keepdims=True)
        on = (o - mu) * lax.rsqrt(var + EPS) * gn_ref[:, vsl[h]]
        gate = gate_ref[:, vsl[h]]
        o_ref[:, vsl[h]] = (_silu(gate) * on).astype(BF16)


def _retention(z, b_count, t, gn_g):
    c = RET_CHUNK
    nc = t // c
    half = RET_QK_DIM // 2
    pos = jnp.arange(t, dtype=jnp.int32)
    inv = ROPE_THETA ** (-jnp.arange(0, RET_QK_DIM, 2, dtype=F32) / RET_QK_DIM)
    ang = pos.astype(F32)[:, None] * inv[None, :]
    cos, sin = jnp.cos(ang), jnp.sin(ang)
    log_gamma = jnp.log(1.0 - 2.0 ** (-5.0 - jnp.arange(RET_HEADS, dtype=F32)))
    lg = jnp.broadcast_to(log_gamma[:, None, None], (RET_HEADS, 1, LANE))
    qk_w = RET_HEADS * RET_QK_DIM
    qk_blk = RET_PAIR * RET_QK_DIM
    v_blk = RET_PAIR * RET_V_DIM
    row = lambda b, h, i: b * nc + i
    return pl.pallas_call(
        _retention_kernel,
        grid=(b_count, RET_HEADS // RET_PAIR, nc),
        in_specs=[pl.BlockSpec((c, qk_blk), lambda b, h, i: (row(b, h, i), h)),
                  pl.BlockSpec((c, qk_blk), lambda b, h, i: (row(b, h, i), qk_w // qk_blk + h)),
                  pl.BlockSpec((c, v_blk), lambda b, h, i: (row(b, h, i), 2 * qk_w // v_blk + h)),
                  pl.BlockSpec((c, v_blk), lambda b, h, i: (row(b, h, i), (2 * qk_w + RET_INNER) // v_blk + h)),
                  pl.BlockSpec((c, half), lambda b, h, i: (i, 0)),
                  pl.BlockSpec((c, half), lambda b, h, i: (i, 0)),
                  pl.BlockSpec((RET_PAIR, 1, LANE), lambda b, h, i: (h, 0, 0)),
                  pl.BlockSpec((1, v_blk), lambda b, h, i: (0, h))],
        out_specs=pl.BlockSpec((c, v_blk), lambda b, h, i: (row(b, h, i), h)),
        out_shape=jax.ShapeDtypeStruct((b_count * t, RET_INNER), BF16),
        scratch_shapes=[pltpu.VMEM((RET_PAIR, RET_QK_DIM, RET_V_DIM), F32), pltpu.VMEM((RET_PAIR, c, c), F32)],
        compiler_params=_cparams("parallel", "parallel", "arbitrary"),
        name="retention",
    )(z, z, z, z, cos, sin, lg, gn_g.reshape(1, RET_INNER))


def _tail_proj_kernel(a_ref, wt_ref, o_ref):
    o_ref[...] = lax.dot_general(a_ref[...], wt_ref[...], (((1,), (1,)), ((), ())), preferred_element_type=F32)


def _tail_proj(hn, w_in_t, n_main, name, tm=1024):
    m, k = hn.shape
    wt_tail = jnp.pad(w_in_t[n_main:, :], ((0, LANE - (w_in_t.shape[0] - n_main)), (0, 0))).astype(BF16)
    return pl.pallas_call(
        _tail_proj_kernel,
        grid=(m // tm,),
        in_specs=[pl.BlockSpec((tm, k), lambda i: (i, 0)),
                  pl.BlockSpec((LANE, k), lambda i: (0, 0))],
        out_specs=pl.BlockSpec((tm, LANE), lambda i: (i, 0)),
        out_shape=jax.ShapeDtypeStruct((m, LANE), F32),
        compiler_params=_cparams("parallel"),
        name=name,
    )(hn, wt_tail)


def _dsa_layer(h, hn, b_count, t, w_in, q_g, k_g, w_out, next_g):
    n_main = 2 * INNER + 2 * KV_WIDTH + IDX_HEADS * IDX_DIM
    w_in_t = w_in.T
    z = _in_proj_wt(hn, w_in_t, n_main, 1024, 1024, "dsa_in_proj")
    z_tail = _tail_proj(hn, w_in_t, n_main, "dsa_tail_proj")
    qT, k, vT, qiT, kia, kib, wT = _dsa_prep(z, z_tail, b_count, t, q_g, k_g)
    bias = _dsa_select(kia, kib, qiT, wT)
    gated = _dsa_attn(qT, k, vT, bias, z)
    return _out_proj(gated, w_out.astype(BF16), h, next_g, 512, "dsa_out_proj")


def _moba_layer(h, hn, b_count, t, w_in, q_g, k_g, w_out, next_g):
    z = _in_proj(hn, w_in, w_in.shape[1], 1024, 1024, "moba_in_proj")
    qT, k, vT, kmean = _moba_prep(z, b_count, t, q_g, k_g)
    gated = _moba_attn(qT, k, vT, kmean, z)
    return _out_proj(gated, w_out.astype(BF16), h, next_g, 512, "moba_out_proj")


def _ret_layer(h, hn, b_count, t, w_in, gn_g, w_out, next_g):
    z = _in_proj(hn, w_in, w_in.shape[1], 1024, 1024, "ret_in_proj")
    gated = _retention(z, b_count, t, gn_g)
    return _out_proj(gated, w_out.astype(BF16), h, next_g, 256, "ret_out_proj")


def _fox_layer(h, hn, b_count, t, w_in, f_bias, q_g, k_g, w_out, next_g):
    n_main = 4 * INNER
    w_in_t = w_in.T
    z = _in_proj_wt(hn, w_in_t, n_main, 1024, 1024, "fox_in_proj")
    z_tail = _tail_proj(hn, w_in_t, n_main, "fox_tail_proj")
    qT, k, vT, cum, cumT = _fox_prep(z, z_tail, b_count, t, f_bias, q_g, k_g)
    gated = _fox_attn(qT, k, vT, cum, cumT, z)
    return _out_proj(gated, w_out.astype(BF16), h, next_g, 512, "fox_out_proj")


def kernel(x, a_norm, a_w_in, a_q_norm, a_k_norm, a_w_out, b_norm, b_w_in, b_q_norm, b_k_norm, b_w_out,
           c_norm, c_w_in, c_gn, c_w_out, d_norm, d_w_in, d_f_bias, d_q_norm, d_k_norm, d_w_out):
    b_count, t, d = x.shape
    assert d == D_MODEL and t % TQ == 0 and t // 4 >= DSA_TOPK
    depth = 4
    norms = (a_norm, b_norm, c_norm, d_norm)
    h = x.reshape(b_count * t, d)
    hn = _rmsnorm(h, a_norm[0])
    for i in range(depth):
        m, j = i % 4, i // 4
        next_g = norms[(i + 1) % 4][(i + 1) // 4] if i + 1 < depth else None
        if m == 0:
            h, hn = _dsa_layer(h, hn, b_count, t, a_w_in[j], a_q_norm[j], a_k_norm[j], a_w_out[j], next_g)
        elif m == 1:
            h, hn = _moba_layer(h, hn, b_count, t, b_w_in[j], b_q_norm[j], b_k_norm[j], b_w_out[j], next_g)
        elif m == 2:
            h, hn = _ret_layer(h, hn, b_count, t, c_w_in[j], c_gn[j], c_w_out[j], next_g)
        else:
            h, hn = _fox_layer(h, hn, b_count, t, d_w_in[j], d_f_bias[j], d_q_norm[j], d_k_norm[j], d_w_out[j],
                               next_g)
    return h.reshape(b_count, t, d)
```

```python
import functools

import jax
import jax.numpy as jnp
from jax import lax
from jax.experimental import pallas as pl
from jax.experimental.pallas import tpu as pltpu

F32 = jnp.float32
BF16 = jnp.bfloat16
I32 = jnp.int32

D_MODEL = 2048
HEAD_DIM = 128
N_HEADS = 16
N_KV_HEADS = 4
GROUP = N_HEADS // N_KV_HEADS
INNER = N_HEADS * HEAD_DIM
KV_WIDTH = N_KV_HEADS * HEAD_DIM
IDX_HEADS = 16
IDX_DIM = 64
DSA_TOPK = 256
MOBA_BLOCK = 256
MOBA_TOPK = 3
RET_HEADS = 8
RET_QK_DIM = 256
RET_V_DIM = 512
RET_INNER = RET_HEADS * RET_V_DIM
ROPE_THETA = 10000.0
EPS = 1e-6
NEG = -1e30
INT_MIN = -(2 ** 31)
LOG2E = 1.4426950408889634

LANE = 128
TQ = 256
TK = 256
KEY_CHUNK = 256
RET_CHUNK = 256
RET_PAIR = 4
VMEM_LIMIT = 56 * 1024 * 1024


def _cparams(*sem):
    return pltpu.CompilerParams(dimension_semantics=sem, vmem_limit_bytes=VMEM_LIMIT)


def _rmsnorm_kernel(x_ref, g_ref, o_ref):
    x = x_ref[...]
    ms = jnp.mean(x * x, axis=-1, keepdims=True)
    o_ref[...] = (x * lax.rsqrt(ms + EPS) * g_ref[...]).astype(o_ref.dtype)


def _rmsnorm(x, g, tm=512):
    m, d = x.shape
    return pl.pallas_call(
        _rmsnorm_kernel,
        grid=(m // tm,),
        in_specs=[pl.BlockSpec((tm, d), lambda i: (i, 0)),
                  pl.BlockSpec((1, d), lambda i: (0, 0))],
        out_specs=pl.BlockSpec((tm, d), lambda i: (i, 0)),
        out_shape=jax.ShapeDtypeStruct((m, d), BF16),
        compiler_params=_cparams("parallel"),
        name="rmsnorm",
    )(x, g.reshape(1, d))


def _in_proj_kernel(a_ref, w_ref, o_ref, wb_ref):
    @pl.when(pl.program_id(1) == 0)
    def _():
        wb_ref[...] = w_ref[...].astype(BF16)

    o_ref[...] = jnp.dot(a_ref[...], wb_ref[...], preferred_element_type=F32)


def _serpentine(n_rows):
    return lambda j, i: i + (j % 2) * (n_rows - 1 - 2 * i)


def _in_proj(a, w, n, tm, tn, name):
    m, k = a.shape
    row = _serpentine(m // tm)
    return pl.pallas_call(
        _in_proj_kernel,
        grid=(n // tn, m // tm),
        in_specs=[pl.BlockSpec((tm, k), lambda j, i: (row(j, i), 0)),
                  pl.BlockSpec((k, tn), lambda j, i: (0, j))],
        out_specs=pl.BlockSpec((tm, tn), lambda j, i: (row(j, i), j)),
        out_shape=jax.ShapeDtypeStruct((m, n), F32),
        scratch_shapes=[pltpu.VMEM((k, tn), BF16)],
        compiler_params=_cparams("parallel", "arbitrary"),
        name=name,
    )(a, w)


def _in_proj_wt_kernel(a_ref, wt_ref, o_ref, wb_ref):
    @pl.when(pl.program_id(1) == 0)
    def _():
        wb_ref[...] = wt_ref[...].T.astype(BF16)

    o_ref[...] = jnp.dot(a_ref[...], wb_ref[...], preferred_element_type=F32)


def _in_proj_wt(a, wt, n, tm, tn, name):
    m, k = a.shape
    row = _serpentine(m // tm)
    return pl.pallas_call(
        _in_proj_wt_kernel,
        grid=(n // tn, m // tm),
        in_specs=[pl.BlockSpec((tm, k), lambda j, i: (row(j, i), 0)),
                  pl.BlockSpec((tn, k), lambda j, i: (j, 0))],
        out_specs=pl.BlockSpec((tm, tn), lambda j, i: (row(j, i), j)),
        out_shape=jax.ShapeDtypeStruct((m, n), F32),
        scratch_shapes=[pltpu.VMEM((k, tn), BF16)],
        compiler_params=_cparams("parallel", "arbitrary"),
        name=name,
    )(a, wt)


def _out_proj_kernel(a_ref, w_ref, r_ref, o_ref):
    o_ref[...] = r_ref[...] + jnp.dot(a_ref[...], w_ref[...], preferred_element_type=F32)


def _out_proj_norm_kernel(a_ref, w_ref, r_ref, g_ref, o_ref, hn_ref):
    h = r_ref[...] + jnp.dot(a_ref[...], w_ref[...], preferred_element_type=F32)
    o_ref[...] = h
    ms = jnp.mean(h * h, axis=-1, keepdims=True)
    hn_ref[...] = (h * lax.rsqrt(ms + EPS) * g_ref[...]).astype(BF16)


def _out_proj(a, w, res, next_g, tm, name):
    m, k = a.shape
    n = w.shape[1]
    in_specs = [pl.BlockSpec((tm, k), lambda i: (i, 0)),
                pl.BlockSpec((k, n), lambda i: (0, 0)),
                pl.BlockSpec((tm, n), lambda i: (i, 0))]
    out_specs = [pl.BlockSpec((tm, n), lambda i: (i, 0))]
    out_shape = [jax.ShapeDtypeStruct((m, n), F32)]
    args = [a, w, res]
    if next_g is not None:
        in_specs.append(pl.BlockSpec((1, n), lambda i: (0, 0)))
        out_specs.append(pl.BlockSpec((tm, n), lambda i: (i, 0)))
        out_shape.append(jax.ShapeDtypeStruct((m, n), BF16))
        args.append(next_g.reshape(1, n))
    out = pl.pallas_call(
        _out_proj_kernel if next_g is None else _out_proj_norm_kernel,
        grid=(m // tm,),
        in_specs=in_specs, out_specs=out_specs, out_shape=out_shape,
        compiler_params=_cparams("parallel"),
        name=name,
    )(*args)
    return (out[0], out[1]) if next_g is not None else (out[0], None)


def _rope_tables(t, d, reps):
    pos = jnp.arange(t, dtype=jnp.int32)
    inv = ROPE_THETA ** (-jnp.arange(0, d, 2, dtype=F32) / d)
    ang = pos.astype(F32)[:, None] * inv[None, :]
    cos, sin = jnp.cos(ang), jnp.sin(ang)
    cos_t = jnp.tile(jnp.concatenate([cos, cos], axis=-1), (1, reps))
    sin_t = jnp.tile(jnp.concatenate([-sin, sin], axis=-1), (1, reps))
    return cos_t, sin_t


def _rope_tables_t(t, d):
    pos = jnp.arange(t, dtype=jnp.int32)
    inv = ROPE_THETA ** (-jnp.arange(0, d, 2, dtype=F32) / d)
    ang = pos.astype(F32)[:, None] * inv[None, :]
    return jnp.cos(ang).T, jnp.sin(ang).T


def _silu(x):
    hx = 0.5 * x
    return hx + hx * jnp.tanh(hx)


def _head_norm(x, g):
    ms = jnp.mean(x * x, axis=-1, keepdims=True)
    return x * lax.rsqrt(ms + EPS) * g


def _rope128(y, cos, sin):
    return y * cos + pltpu.roll(y, 64, 1) * sin


def _rope_rows(y, cos_t, sin_t):
    half = y.shape[0] // 2
    y1, y2 = y[:half], y[half:]
    return jnp.concatenate([y1 * cos_t - y2 * sin_t, y1 * sin_t + y2 * cos_t], axis=0)


def _qkv_prep_body(q_ref, k_ref, v_ref, cos_ref, sin_ref, cos_t_ref, sin_t_ref, qg_col_ref, kg_ref,
                   qT_ref, ko_ref, vT_ref, *, n_kv, rope):
    tm = q_ref.shape[0]
    kg = kg_ref[...]
    scale = HEAD_DIM ** -0.5 * LOG2E
    if rope:
        cos = cos_ref[...]
        sin = sin_ref[...]
        cos_t = cos_t_ref[...]
        sin_t = sin_t_ref[...]
    qg_t = jnp.broadcast_to(qg_col_ref[...], (HEAD_DIM, tm)) * scale
    for h in range(N_HEADS):
        sl = slice(h * HEAD_DIM, (h + 1) * HEAD_DIM)
        x = q_ref[:, sl].T
        ms = jnp.mean(x * x, axis=0, keepdims=True)
        y = x * lax.rsqrt(ms + EPS) * qg_t
        if rope:
            y = _rope_rows(y, cos_t, sin_t)
        qT_ref[0, sl, :] = y.astype(BF16)
    k_out = []
    for g in range(n_kv):
        sl = slice(g * HEAD_DIM, (g + 1) * HEAD_DIM)
        y = _head_norm(k_ref[:, sl], kg)
        if rope:
            y = _rope128(y, cos, sin)
        if ko_ref.ndim == 4:
            ko_ref[0, g] = y.astype(BF16)
        else:
            ko_ref[0, :, sl] = y.astype(BF16)
        vT_ref[0, sl, :] = v_ref[:, sl].T.astype(BF16)
        k_out.append(y)
    return k_out


def _dsa_prep_kernel(q_ref, k_ref, v_ref, qi_ref, tail_ref, cos_ref, sin_ref, cos_t_ref, sin_t_ref,
                     cos64_ref, sin64_ref, cos32_t_ref, sin32_t_ref,
                     qg_ref, kg_ref, qT_ref, ko_ref, vT_ref, qiT_ref, kia_ref, kib_ref, wT_ref):
    _qkv_prep_body(q_ref, k_ref, v_ref, cos_ref, sin_ref, cos_t_ref, sin_t_ref, qg_ref, kg_ref,
                   qT_ref, ko_ref, vT_ref, n_kv=N_KV_HEADS, rope=True)
    tm = tail_ref.shape[0]
    lane = lax.broadcasted_iota(I32, (tm, LANE), 1)
    first_half = (lane % IDX_DIM) < (IDX_DIM // 2)
    c64 = cos64_ref[...]
    s64 = sin64_ref[...]

    def rope64(x):
        rot = jnp.where(first_half, pltpu.roll(x, LANE - IDX_DIM // 2, 1), pltpu.roll(x, IDX_DIM // 2, 1))
        return x * c64 + rot * s64

    c32_t = cos32_t_ref[...]
    s32_t = sin32_t_ref[...]
    for j in range(IDX_HEADS * IDX_DIM // LANE):
        sl = slice(j * LANE, (j + 1) * LANE)
        x = qi_ref[:, sl].T * (IDX_DIM ** -0.5)
        y = jnp.concatenate([_rope_rows(x[:IDX_DIM], c32_t, s32_t), _rope_rows(x[IDX_DIM:], c32_t, s32_t)], axis=0)
        qiT_ref[0, sl, :] = y.astype(BF16)
    tail = tail_ref[...]
    ka = jnp.where(lane < IDX_DIM, rope64(tail), 0.0)
    kia_ref[0] = ka.astype(BF16)
    kib_ref[0] = pltpu.roll(ka, IDX_DIM, 1).astype(BF16)
    wT_ref[0] = (tail * (IDX_HEADS ** -0.5)).T


def _moba_prep_kernel(q_ref, k_ref, v_ref, cos_ref, sin_ref, cos_t_ref, sin_t_ref, qg_ref, kg_ref,
                      qT_ref, ko_ref, vT_ref, kmean_ref):
    k_out = _qkv_prep_body(q_ref, k_ref, v_ref, cos_ref, sin_ref, cos_t_ref, sin_t_ref, qg_ref, kg_ref,
                           qT_ref, ko_ref, vT_ref, n_kv=N_KV_HEADS, rope=True)
    for g in range(N_KV_HEADS):
        km = jnp.mean(k_out[g], axis=0, keepdims=True)
        kmean_ref[0, g, pl.ds(pl.program_id(1), 1), :] = jnp.concatenate([km] * GROUP, axis=1)


def _fox_prep_kernel(q_ref, k_ref, v_ref, f_ref, fb_ref, qg_ref, kg_ref,
                     qT_ref, ko_ref, vT_ref, cum_ref, cumT_ref, carry_ref):
    _qkv_prep_body(q_ref, k_ref, v_ref, None, None, None, None, qg_ref, kg_ref, qT_ref, ko_ref, vT_ref,
                   n_kv=N_HEADS, rope=False)

    @pl.when(pl.program_id(1) == 0)
    def _():
        carry_ref[...] = jnp.zeros_like(carry_ref)

    tm = f_ref.shape[0]
    x = f_ref[...] + fb_ref[...]
    lf = jnp.minimum(x, 0.0) - jnp.log(1.0 + jnp.exp(-jnp.abs(x)))
    hi = lf.astype(BF16)
    r1 = lf - hi.astype(F32)
    lo = r1.astype(BF16)
    lo2 = (r1 - lo.astype(F32)).astype(BF16)
    row = lax.broadcasted_iota(I32, (tm, tm), 0)
    col = lax.broadcasted_iota(I32, (tm, tm), 1)
    tri = (col <= row).astype(BF16)
    parts = jnp.dot(tri, jnp.concatenate([hi, lo, lo2], axis=1), preferred_element_type=F32)
    cum = parts[:, :LANE] + parts[:, LANE:2 * LANE] + parts[:, 2 * LANE:] + carry_ref[...]
    carry_ref[...] = cum[tm - 1:tm, :]
    cum2 = cum * LOG2E
    cum_ref[0] = cum2
    cumT_ref[0] = cum2.T


def _prep_specs(b_count, t, tm, z_q_blk, z_k_blk, z_v_blk, kv_width):
    nt = t // tm
    row = lambda b, i: b * nt + i
    in_specs = [
        pl.BlockSpec((tm, INNER), lambda b, i: (row(b, i), z_q_blk)),
        pl.BlockSpec((tm, kv_width), lambda b, i: (row(b, i), z_k_blk)),
        pl.BlockSpec((tm, kv_width), lambda b, i: (row(b, i), z_v_blk)),
    ]
    out_specs = [
        pl.BlockSpec((1, INNER, tm), lambda b, i: (b, 0, i)),
        pl.BlockSpec((1, tm, kv_width), lambda b, i: (b, i, 0)),
        pl.BlockSpec((1, kv_width, tm), lambda b, i: (b, 0, i)),
    ]
    out_shape = [
        jax.ShapeDtypeStruct((b_count, INNER, t), BF16),
        jax.ShapeDtypeStruct((b_count, t, kv_width), BF16),
        jax.ShapeDtypeStruct((b_count, kv_width, t), BF16),
    ]
    return row, in_specs, out_specs, out_shape


def _dsa_prep(z, z_tail, b_count, t, q_g, k_g, tm=256):
    row, in_specs, out_specs, out_shape = _prep_specs(b_count, t, tm, 0, INNER // KV_WIDTH,
                                                      INNER // KV_WIDTH + 1, KV_WIDTH)
    qi_w = IDX_HEADS * IDX_DIM
    qi_off = 2 * INNER + 2 * KV_WIDTH
    cos, sin = _rope_tables(t, HEAD_DIM, 1)
    cos64, sin64 = _rope_tables(t, IDX_DIM, LANE // IDX_DIM)
    cos_t, sin_t = _rope_tables_t(t, HEAD_DIM)
    cos32_t, sin32_t = _rope_tables_t(t, IDX_DIM)
    tab = pl.BlockSpec((tm, LANE), lambda b, i: (i, 0))
    tab_t = pl.BlockSpec((HEAD_DIM // 2, tm), lambda b, i: (0, i))
    tab32_t = pl.BlockSpec((IDX_DIM // 2, tm), lambda b, i: (0, i))
    gain = pl.BlockSpec((1, HEAD_DIM), lambda b, i: (0, 0))
    gain_col = pl.BlockSpec((HEAD_DIM, 1), lambda b, i: (0, 0))
    in_specs += [
        pl.BlockSpec((tm, qi_w), lambda b, i: (row(b, i), qi_off // qi_w)),
        pl.BlockSpec((tm, LANE), lambda b, i: (row(b, i), 0)),
        tab, tab, tab_t, tab_t, tab, tab, tab32_t, tab32_t, gain_col, gain,
    ]
    out_specs += [
        pl.BlockSpec((1, qi_w, tm), lambda b, i: (b, 0, i)),
        pl.BlockSpec((1, tm, LANE), lambda b, i: (b, i, 0)),
        pl.BlockSpec((1, tm, LANE), lambda b, i: (b, i, 0)),
        pl.BlockSpec((1, LANE, tm), lambda b, i: (b, 0, i)),
    ]
    out_shape += [
        jax.ShapeDtypeStruct((b_count, qi_w, t), BF16),
        jax.ShapeDtypeStruct((b_count, t, LANE), BF16),
        jax.ShapeDtypeStruct((b_count, t, LANE), BF16),
        jax.ShapeDtypeStruct((b_count, LANE, t), F32),
    ]
    return pl.pallas_call(
        _dsa_prep_kernel,
        grid=(b_count, t // tm),
        in_specs=in_specs, out_specs=out_specs, out_shape=out_shape,
        compiler_params=_cparams("parallel", "parallel"),
        name="dsa_prep",
    )(z, z, z, z, z_tail, cos, sin, cos_t, sin_t, cos64, sin64, cos32_t, sin32_t,
      q_g.reshape(HEAD_DIM, 1), k_g.reshape(1, HEAD_DIM))


def _moba_prep(z, b_count, t, q_g, k_g):
    tm = MOBA_BLOCK
    row, in_specs, out_specs, out_shape = _prep_specs(b_count, t, tm, 0, INNER // KV_WIDTH,
                                                      INNER // KV_WIDTH + 1, KV_WIDTH)
    cos, sin = _rope_tables(t, HEAD_DIM, 1)
    cos_t, sin_t = _rope_tables_t(t, HEAD_DIM)
    tab = pl.BlockSpec((tm, LANE), lambda b, i: (i, 0))
    tab_t = pl.BlockSpec((HEAD_DIM // 2, tm), lambda b, i: (0, i))
    gain = pl.BlockSpec((1, HEAD_DIM), lambda b, i: (0, 0))
    gain_col = pl.BlockSpec((HEAD_DIM, 1), lambda b, i: (0, 0))
    in_specs += [tab, tab, tab_t, tab_t, gain_col, gain]
    out_specs += [pl.BlockSpec((1, N_KV_HEADS, t // tm, GROUP * HEAD_DIM), lambda b, i: (b, 0, 0, 0))]
    out_shape += [jax.ShapeDtypeStruct((b_count, N_KV_HEADS, t // tm, GROUP * HEAD_DIM), F32)]
    return pl.pallas_call(
        _moba_prep_kernel,
        grid=(b_count, t // tm),
        in_specs=in_specs, out_specs=out_specs, out_shape=out_shape,
        compiler_params=_cparams("parallel", "arbitrary"),
        name="moba_prep",
    )(z, z, z, cos, sin, cos_t, sin_t, q_g.reshape(HEAD_DIM, 1), k_g.reshape(1, HEAD_DIM))


def _fox_prep(z, z_tail, b_count, t, f_bias, q_g, k_g, tm=256):
    row, in_specs, out_specs, out_shape = _prep_specs(b_count, t, tm, 0, 1, 2, INNER)
    out_specs[1] = pl.BlockSpec((1, N_HEADS, tm, HEAD_DIM), lambda b, i: (b, 0, i, 0))
    out_shape[1] = jax.ShapeDtypeStruct((b_count, N_HEADS, t, HEAD_DIM), BF16)
    gain = pl.BlockSpec((1, HEAD_DIM), lambda b, i: (0, 0))
    fb = jnp.pad(f_bias.reshape(1, N_HEADS), ((0, 0), (0, LANE - N_HEADS)))
    in_specs += [
        pl.BlockSpec((tm, LANE), lambda b, i: (row(b, i), 0)),
        pl.BlockSpec((1, LANE), lambda b, i: (0, 0)),
        pl.BlockSpec((HEAD_DIM, 1), lambda b, i: (0, 0)), gain,
    ]
    out_specs += [
        pl.BlockSpec((1, tm, LANE), lambda b, i: (b, i, 0)),
        pl.BlockSpec((1, LANE, tm), lambda b, i: (b, 0, i)),
    ]
    out_shape += [
        jax.ShapeDtypeStruct((b_count, t, LANE), F32),
        jax.ShapeDtypeStruct((b_count, LANE, t), F32),
    ]
    return pl.pallas_call(
        _fox_prep_kernel,
        grid=(b_count, t // tm),
        in_specs=in_specs, out_specs=out_specs, out_shape=out_shape,
        scratch_shapes=[pltpu.VMEM((1, LANE), F32)],
        compiler_params=_cparams("parallel", "arbitrary"),
        name="fox_prep",
    )(z, z, z, z_tail, fb, q_g.reshape(HEAD_DIM, 1), k_g.reshape(1, HEAD_DIM))


def _dsa_select_kernel(kia_ref, kib_ref, qiT_ref, wT_ref, bias_ref, key_ref, ans_ref, n_ge_ref):
    qb = pl.program_id(1)
    n_kt = key_ref.shape[0] // TK
    w = wT_ref[0]
    s_iota = lax.broadcasted_iota(I32, (TK, TQ), 0)
    t_iota = lax.broadcasted_iota(I32, (TK, TQ), 1)

    for kt in range(n_kt):
        rows = slice(kt * TK, (kt + 1) * TK)

        @pl.when(kt <= qb)
        def _():
            ka = kia_ref[0, rows, :]
            kb = kib_ref[0, rows, :]
            score = jnp.zeros((TK, TQ), F32)
            for j in range(IDX_HEADS // 2):
                qp = qiT_ref[0, j * LANE:(j + 1) * LANE, :]
                ra = jnp.maximum(jnp.dot(ka, qp, preferred_element_type=F32), 0.0)
                rb = jnp.maximum(jnp.dot(kb, qp, preferred_element_type=F32), 0.0)
                wa = w[IDX_DIM + 2 * j:IDX_DIM + 2 * j + 1, :]
                wb = w[IDX_DIM + 2 * j + 1:IDX_DIM + 2 * j + 2, :]
                score = score + (ra * wa + rb * wb)
            bits = lax.bitcast_convert_type(score + 0.0, I32)
            key = bits ^ ((bits >> 31) & 0x7FFFFFFF)
            causal = (kt * TK + s_iota) <= (qb * TQ + t_iota)
            key_ref[rows, :] = jnp.where(causal, key, INT_MIN)

        @pl.when(kt > qb)
        def _():
            key_ref[rows, :] = jnp.full((TK, TQ), INT_MIN, I32)

    def variant(v):
        n = (v + 1) * TK
        kf = float(DSA_TOPK)

        def count_ge(cand):
            return jnp.sum((key_ref[:n, :] >= cand).astype(F32), axis=0, keepdims=True)

        def bit_step(i, ans):
            cand = ans | (jnp.int32(1) << (30 - i))
            return jnp.where(count_ge(cand) >= kf, cand, ans)

        sign = jnp.where(count_ge(jnp.zeros((1, TQ), I32)) >= kf, 0, INT_MIN).astype(I32)
        ans = lax.fori_loop(0, 31, bit_step, sign)
        ans_ref[...] = ans
        n_ge_ref[...] = count_ge(ans)

    _for_each_query_tile(qb, n_kt, variant)
    ans = ans_ref[...]

    for kt in range(n_kt):
        rows = slice(kt * TK, (kt + 1) * TK)
        k = key_ref[rows, :]
        sel = (k >= ans) & (k > INT_MIN)
        bias_ref[0, rows, :] = jnp.where(sel, 0.0, NEG)

    @pl.when(jnp.max(n_ge_ref[...]) > float(DSA_TOPK))
    def _():
        n_gt = jnp.zeros((1, TQ), F32)
        for kt in range(n_kt):
            n_gt = n_gt + jnp.sum((key_ref[kt * TK:(kt + 1) * TK, :] > ans).astype(F32), axis=0, keepdims=True)
        n_tied_kept = float(DSA_TOPK) - n_gt
        tri = (lax.broadcasted_iota(I32, (TK, TK), 1) <= lax.broadcasted_iota(I32, (TK, TK), 0)).astype(BF16)
        seen = jnp.zeros((1, TQ), F32)
        for kt in range(n_kt):
            rows = slice(kt * TK, (kt + 1) * TK)
            k = key_ref[rows, :]
            tied = (k == ans) & (k > INT_MIN)
            rank = jnp.dot(tri, tied.astype(F32).astype(BF16), preferred_element_type=F32) + seen
            keep = (k > ans) | (tied & (rank <= n_tied_kept))
            bias_ref[0, rows, :] = jnp.where(keep, 0.0, NEG)
            seen = rank[TK - 1:TK, :]


def _dsa_select(kia, kib, qiT, wT):
    b_count, t, _ = kia.shape
    return pl.pallas_call(
        _dsa_select_kernel,
        grid=(b_count, t // TQ),
        in_specs=[pl.BlockSpec((1, t, LANE), lambda b, i: (b, 0, 0)),
                  pl.BlockSpec((1, t, LANE), lambda b, i: (b, 0, 0)),
                  pl.BlockSpec((1, IDX_HEADS * IDX_DIM, TQ), lambda b, i: (b, 0, i)),
                  pl.BlockSpec((1, LANE, TQ), lambda b, i: (b, 0, i))],
        out_specs=pl.BlockSpec((1, t, TQ), lambda b, i: (b, 0, i)),
        out_shape=jax.ShapeDtypeStruct((b_count, t, t), F32),
        scratch_shapes=[pltpu.VMEM((t, TQ), I32), pltpu.VMEM((1, TQ), I32), pltpu.VMEM((1, TQ), F32)],
        compiler_params=_cparams("parallel", "parallel"),
        name="dsa_select",
    )(kia, kib, qiT, wT)


def _softmax_attend_heads(qs, k_alls, vt_alls, bias_fns, n_tiles, causal_last):
    heads = range(len(qs))
    s_alls = [jnp.dot(k_alls[i], qs[i], preferred_element_type=F32) for i in heads]
    tiles = [[] for _ in heads]
    diag = (n_tiles - 1) * TK
    for i in heads:
        for lo in range(0, n_tiles * TK, KEY_CHUNK):
            sj = bias_fns[i](lo, s_alls[i][lo:lo + KEY_CHUNK])
            if causal_last and lo >= diag:
                s_iota = lax.broadcasted_iota(I32, (KEY_CHUNK, TQ), 0) + (lo - diag)
                sj = jnp.where(s_iota <= lax.broadcasted_iota(I32, (KEY_CHUNK, TQ), 1), sj, NEG)
            tiles[i].append(sj)
    ms = [functools.reduce(jnp.maximum, [jnp.max(sj, axis=0, keepdims=True) for sj in tiles[i]]) for i in heads]
    outs = []
    for i in heads:
        ps = [jnp.exp2(sj - ms[i]) for sj in tiles[i]]
        l = functools.reduce(jnp.add, [jnp.sum(p, axis=0, keepdims=True) for p in ps])
        ps = [p.astype(BF16) for p in ps]
        p_all = jnp.concatenate(ps, axis=0)
        outs.append(jnp.dot(vt_alls[i], p_all, preferred_element_type=F32) * (1.0 / l))
    return outs


def _for_each_query_tile(qb, n_q, body):
    for v in range(n_q):
        pl.when(qb == v)(functools.partial(body, v))


def _pair_rows(i):
    base = pl.multiple_of(i * 2 * HEAD_DIM, 2 * HEAD_DIM)
    return [pl.ds(base, HEAD_DIM), pl.ds(base + HEAD_DIM, HEAD_DIM)]

def _gate_and_store(oT_ref, gate_ref, o_ref):
    for r in range(GROUP):
        sl = slice(r * HEAD_DIM, (r + 1) * HEAD_DIM)
        gate = gate_ref[:, sl]
        o_ref[:, sl] = (_silu(gate) * oT_ref[sl, :].T).astype(BF16)


def _dsa_attn_kernel(qT_ref, k_ref, vT_ref, bias_ref, gate_ref, o_ref, oT_ref):
    def variant(v):
        n = (v + 1) * TK

        def bias_fn(lo, s):
            return s + bias_ref[0, lo:lo + KEY_CHUNK, :]

        def head_pair(i, carry):
            rows = _pair_rows(i)
            outs = _softmax_attend_heads([qT_ref[0, r, :] for r in rows], [k_ref[0, :n, :]] * 2,
                                         [vT_ref[0, :, :n]] * 2, [bias_fn] * 2, v + 1, False)
            for r, o in zip(rows, outs):
                oT_ref[r, :] = o
            return carry

        lax.fori_loop(0, GROUP // 2, head_pair, 0, unroll=True)

    _for_each_query_tile(pl.program_id(2), k_ref.shape[1] // TQ, variant)
    _gate_and_store(oT_ref, gate_ref, o_ref)


def _moba_attn_kernel(qT_ref, k_ref, vT_ref, kmean_ref, gate_ref, o_ref, oT_ref, selb_ref):
    own = pl.program_id(2)
    n_blocks = kmean_ref.shape[2]

    km = kmean_ref[0, 0].astype(BF16)
    gs = jnp.dot(km, qT_ref[0], preferred_element_type=F32)
    blk = lax.broadcasted_iota(I32, (n_blocks, TQ), 0)
    past = blk < own
    gs = jnp.where(past, gs, NEG)
    rank = jnp.zeros((n_blocks, TQ), F32)
    for m in range(n_blocks):
        row = gs[m:m + 1, :]
        rank = rank + ((row > gs) | ((row == gs) & (blk > m))).astype(F32)
    sel = rank < float(min(MOBA_TOPK, n_blocks - 1))
    selb_ref[...] = jnp.where(sel & past, 0.0, NEG)

    def variant(v):
        n = (v + 1) * TK

        def bias_fn(lo, s):
            j = lo // MOBA_BLOCK
            return s if j == v else s + selb_ref[j:j + 1, :]

        def head_pair(i, carry):
            rows = _pair_rows(i)
            outs = _softmax_attend_heads([qT_ref[0, r, :] for r in rows], [k_ref[0, :n, :]] * 2,
                                         [vT_ref[0, :, :n]] * 2, [bias_fn] * 2, v + 1, True)
            for r, o in zip(rows, outs):
                oT_ref[r, :] = o
            return carry

        lax.fori_loop(0, GROUP // 2, head_pair, 0, unroll=True)

    _for_each_query_tile(own, n_blocks, variant)
    _gate_and_store(oT_ref, gate_ref, o_ref)


def _fox_attn_kernel(qT_ref, k_ref, vT_ref, cum_ref, cumT_ref, gate_ref, o_ref, oT_ref):
    g = pl.program_id(1)

    def variant(v):
        n = (v + 1) * TK

        def make_bias_fn(h):
            cq = cumT_ref[0, pl.ds(h, 1), :]
            onehot = (lax.broadcasted_iota(I32, (KEY_CHUNK, LANE), 1) == h).astype(F32)

            def bias_fn(lo, s):
                ck = jnp.sum(cum_ref[0, lo:lo + KEY_CHUNK, :] * onehot, axis=1, keepdims=True)
                return s + (cq - ck)

            return bias_fn

        def head_pair(i, carry):
            rows = _pair_rows(i)
            heads = [2 * i, 2 * i + 1]
            outs = _softmax_attend_heads([qT_ref[0, r, :] for r in rows], [k_ref[0, r, :n, :] for r in heads],
                                         [vT_ref[0, r, :n] for r in rows],
                                         [make_bias_fn(g * GROUP + r) for r in heads], v + 1, True)
            for r, o in zip(rows, outs):
                oT_ref[r, :] = o
            return carry

        lax.fori_loop(0, GROUP // 2, head_pair, 0, unroll=True)

    _for_each_query_tile(pl.program_id(2), k_ref.shape[2] // TQ, variant)
    _gate_and_store(oT_ref, gate_ref, o_ref)


def _attn_scratch():
    return [pltpu.VMEM((GROUP * HEAD_DIM, TQ), F32)]


def _gqa_specs(t, gate_blk):
    nq = t // TQ
    gw = GROUP * HEAD_DIM
    in_specs = [
        pl.BlockSpec((1, gw, TQ), lambda b, g, i: (b, g, i)),
        pl.BlockSpec((1, t, HEAD_DIM), lambda b, g, i: (b, 0, g)),
        pl.BlockSpec((1, HEAD_DIM, t), lambda b, g, i: (b, g, 0)),
    ]
    gate_spec = pl.BlockSpec((TQ, gw), lambda b, g, i: (b * nq + i, gate_blk + g))
    out_spec = pl.BlockSpec((TQ, gw), lambda b, g, i: (b * nq + i, g))
    return nq, in_specs, gate_spec, out_spec


def _dsa_attn(qT, k, vT, bias, z):
    b_count, t, _ = k.shape
    gate_blk = (INNER + 2 * KV_WIDTH) // (GROUP * HEAD_DIM)
    nq, in_specs, gate_spec, out_spec = _gqa_specs(t, gate_blk)
    in_specs += [pl.BlockSpec((1, t, TQ), lambda b, g, i: (b, 0, i)), gate_spec]
    return pl.pallas_call(
        _dsa_attn_kernel,
        grid=(b_count, N_KV_HEADS, nq),
        in_specs=in_specs, out_specs=out_spec,
        out_shape=jax.ShapeDtypeStruct((b_count * t, INNER), BF16),
        scratch_shapes=_attn_scratch(),
        compiler_params=_cparams("parallel", "parallel", "parallel"),
        name="dsa_attn",
    )(qT, k, vT, bias, z)


def _moba_attn(qT, k, vT, kmean, z):
    b_count, t, _ = k.shape
    gate_blk = (INNER + 2 * KV_WIDTH) // (GROUP * HEAD_DIM)
    nq, in_specs, gate_spec, out_spec = _gqa_specs(t, gate_blk)
    n_blocks = t // MOBA_BLOCK
    in_specs += [pl.BlockSpec((1, 1, n_blocks, GROUP * HEAD_DIM), lambda b, g, i: (b, g, 0, 0)), gate_spec]
    return pl.pallas_call(
        _moba_attn_kernel,
        grid=(b_count, N_KV_HEADS, nq),
        in_specs=in_specs, out_specs=out_spec,
        out_shape=jax.ShapeDtypeStruct((b_count * t, INNER), BF16),
        scratch_shapes=_attn_scratch() + [pltpu.VMEM((n_blocks, TQ), F32)],
        compiler_params=_cparams("parallel", "parallel", "parallel"),
        name="moba_attn",
    )(qT, k, vT, kmean, z)


def _fox_attn(qT, k, vT, cum, cumT, z):
    b_count, _, t, _ = k.shape
    nq = t // TQ
    gw = GROUP * HEAD_DIM
    gate_blk = 3 * INNER // gw
    return pl.pallas_call(
        _fox_attn_kernel,
        grid=(b_count, N_HEADS // GROUP, nq),
        in_specs=[pl.BlockSpec((1, gw, TQ), lambda b, g, i: (b, g, i)),
                  pl.BlockSpec((1, GROUP, t, HEAD_DIM), lambda b, g, i: (b, g, 0, 0)),
                  pl.BlockSpec((1, gw, t), lambda b, g, i: (b, g, 0)),
                  pl.BlockSpec((1, t, LANE), lambda b, g, i: (b, 0, 0)),
                  pl.BlockSpec((1, LANE, TQ), lambda b, g, i: (b, 0, i)),
                  pl.BlockSpec((TQ, gw), lambda b, g, i: (b * nq + i, gate_blk + g))],
        out_specs=pl.BlockSpec((TQ, gw), lambda b, g, i: (b * nq + i, g)),
        out_shape=jax.ShapeDtypeStruct((b_count * t, INNER), BF16),
        scratch_shapes=_attn_scratch(),
        compiler_params=_cparams("parallel", "parallel", "parallel"),
        name="fox_attn",
    )(qT, k, vT, cum, cumT, z)


def _retention_kernel(q_ref, k_ref, v_ref, gate_ref, cos_ref, sin_ref, lg_ref, gn_ref, o_ref,
                      state_ref, dmask_ref):
    c = RET_CHUNK
    heads = range(RET_PAIR)
    lgs = [lg_ref[h][:, :1] for h in heads]

    @pl.when(pl.program_id(2) == 0)
    def _():
        state_ref[...] = jnp.zeros_like(state_ref)
        diff = (lax.broadcasted_iota(I32, (c, c), 0) - lax.broadcasted_iota(I32, (c, c), 1)).astype(F32)
        for h in heads:
            dmask_ref[h] = jnp.where(diff >= 0, jnp.exp(jnp.maximum(diff, 0.0) * lgs[h]), 0.0)

    cos = cos_ref[...]
    sin = sin_ref[...]
    half = RET_QK_DIM // 2

    def rope(ref, h):
        x1 = ref[:, h * RET_QK_DIM:h * RET_QK_DIM + half]
        x2 = ref[:, h * RET_QK_DIM + half:(h + 1) * RET_QK_DIM]
        return jnp.concatenate([x1 * cos - x2 * sin, x1 * sin + x2 * cos], axis=1)

    i_col = lax.broadcasted_iota(I32, (c, 1), 0).astype(F32)
    vsl = [slice(h * RET_V_DIM, (h + 1) * RET_V_DIM) for h in heads]
    qs = [rope(q_ref, h) for h in heads]
    ks = [rope(k_ref, h) * (RET_QK_DIM ** -0.5) for h in heads]
    inners = [lax.dot_general(qs[h].astype(BF16), ks[h].astype(BF16), (((1,), (1,)), ((), ())),
                              preferred_element_type=F32) for h in heads]
    vs = [v_ref[:, vsl[h]].astype(BF16) for h in heads]
    cross = [jnp.dot((qs[h] * jnp.exp((i_col + 1.0) * lgs[h])).astype(BF16), state_ref[h].astype(BF16),
                     preferred_element_type=F32) for h in heads]
    kdTs = [(ks[h] * jnp.exp((c - 1.0 - i_col) * lgs[h])).T.astype(BF16) for h in heads]
    os_ = [jnp.dot((inners[h] * dmask_ref[h]).astype(BF16), vs[h], preferred_element_type=F32) + cross[h]
           for h in heads]
    for h in heads:
        state_ref[h] = state_ref[h] * jnp.exp(c * lgs[h]) + jnp.dot(kdTs[h], vs[h], preferred_element_type=F32)
    for h in heads:
        o = os_[h]
        mu = jnp.mean(o, axis=-1, keepdims=True)
        var = jnp.mean(jnp.square(o - mu), axis=-1, keepdims=True)
        on = (o - mu) * lax.rsqrt(var + EPS) * gn_ref[:, vsl[h]]
        gate = gate_ref[:, vsl[h]]
        o_ref[:, vsl[h]] = (_silu(gate) * on).astype(BF16)


def _retention(z, b_count, t, gn_g):
    c = RET_CHUNK
    nc = t // c
    half = RET_QK_DIM // 2
    pos = jnp.arange(t, dtype=jnp.int32)
    inv = ROPE_THETA ** (-jnp.arange(0, RET_QK_DIM, 2, dtype=F32) / RET_QK_DIM)
    ang = pos.astype(F32)[:, None] * inv[None, :]
    cos, sin = jnp.cos(ang), jnp.sin(ang)
    log_gamma = jnp.log(1.0 - 2.0 ** (-5.0 - jnp.arange(RET_HEADS, dtype=F32)))
    lg = jnp.broadcast_to(log_gamma[:, None, None], (RET_HEADS, 1, LANE))
    qk_w = RET_HEADS * RET_QK_DIM
    qk_blk = RET_PAIR * RET_QK_DIM
    v_blk = RET_PAIR * RET_V_DIM
    row = lambda b, h, i: b * nc + i
    return pl.pallas_call(
        _retention_kernel,
        grid=(b_count, RET_HEADS // RET_PAIR, nc),
        in_specs=[pl.BlockSpec((c, qk_blk), lambda b, h, i: (row(b, h, i), h)),
                  pl.BlockSpec((c, qk_blk), lambda b, h, i: (row(b, h, i), qk_w // qk_blk + h)),
                  pl.BlockSpec((c, v_blk), lambda b, h, i: (row(b, h, i), 2 * qk_w // v_blk + h)),
                  pl.BlockSpec((c, v_blk), lambda b, h, i: (row(b, h, i), (2 * qk_w + RET_INNER) // v_blk + h)),
                  pl.BlockSpec((c, half), lambda b, h, i: (i, 0)),
                  pl.BlockSpec((c, half), lambda b, h, i: (i, 0)),
                  pl.BlockSpec((RET_PAIR, 1, LANE), lambda b, h, i: (h, 0, 0)),
                  pl.BlockSpec((1, v_blk), lambda b, h, i: (0, h))],
        out_specs=pl.BlockSpec((c, v_blk), lambda b, h, i: (row(b, h, i), h)),
        out_shape=jax.ShapeDtypeStruct((b_count * t, RET_INNER), BF16),
        scratch_shapes=[pltpu.VMEM((RET_PAIR, RET_QK_DIM, RET_V_DIM), F32), pltpu.VMEM((RET_PAIR, c, c), F32)],
        compiler_params=_cparams("parallel", "parallel", "arbitrary"),
        name="retention",
    )(z, z, z, z, cos, sin, lg, gn_g.reshape(1, RET_INNER))


def _tail_proj_kernel(a_ref, wt_ref, o_ref):
    o_ref[...] = lax.dot_general(a_ref[...], wt_ref[...], (((1,), (1,)), ((), ())), preferred_element_type=F32)


def _tail_proj(hn, w_in_t, n_main, name, tm=1024):
    m, k = hn.shape
    wt_tail = jnp.pad(w_in_t[n_main:, :], ((0, LANE - (w_in_t.shape[0] - n_main)), (0, 0))).astype(BF16)
    return pl.pallas_call(
        _tail_proj_kernel,
        grid=(m // tm,),
        in_specs=[pl.BlockSpec((tm, k), lambda i: (i, 0)),
                  pl.BlockSpec((LANE, k), lambda i: (0, 0))],
        out_specs=pl.BlockSpec((tm, LANE), lambda i: (i, 0)),
        out_shape=jax.ShapeDtypeStruct((m, LANE), F32),
        compiler_params=_cparams("parallel"),
        name=name,
    )(hn, wt_tail)


def _dsa_layer(h, hn, b_count, t, w_in, q_g, k_g, w_out, next_g):
    n_main = 2 * INNER + 2 * KV_WIDTH + IDX_HEADS * IDX_DIM
    w_in_t = w_in.T
    z = _in_proj_wt(hn, w_in_t, n_main, 1024, 1024, "dsa_in_proj")
    z_tail = _tail_proj(hn, w_in_t, n_main, "dsa_tail_proj")
    qT, k, vT, qiT, kia, kib, wT = _dsa_prep(z, z_tail, b_count, t, q_g, k_g)
    bias = _dsa_select(kia, kib, qiT, wT)
    gated = _dsa_attn(qT, k, vT, bias, z)
    return _out_proj(gated, w_out.astype(BF16), h, next_g, 512, "dsa_out_proj")


def _moba_layer(h, hn, b_count, t, w_in, q_g, k_g, w_out, next_g):
    z = _in_proj(hn, w_in, w_in.shape[1], 1024, 1024, "moba_in_proj")
    qT, k, vT, kmean = _moba_prep(z, b_count, t, q_g, k_g)
    gated = _moba_attn(qT, k, vT, kmean, z)
    return _out_proj(gated, w_out.astype(BF16), h, next_g, 512, "moba_out_proj")


def _ret_layer(h, hn, b_count, t, w_in, gn_g, w_out, next_g):
    z = _in_proj(hn, w_in, w_in.shape[1], 1024, 1024, "ret_in_proj")
    gated = _retention(z, b_count, t, gn_g)
    return _out_proj(gated, w_out.astype(BF16), h, next_g, 256, "ret_out_proj")


def _fox_layer(h, hn, b_count, t, w_in, f_bias, q_g, k_g, w_out, next_g):
    n_main = 4 * INNER
    w_in_t = w_in.T
    z = _in_proj_wt(hn, w_in_t, n_main, 1024, 1024, "fox_in_proj")
    z_tail = _tail_proj(hn, w_in_t, n_main, "fox_tail_proj")
    qT, k, vT, cum, cumT = _fox_prep(z, z_tail, b_count, t, f_bias, q_g, k_g)
    gated = _fox_attn(qT, k, vT, cum, cumT, z)
    return _out_proj(gated, w_out.astype(BF16), h, next_g, 512, "fox_out_proj")


def kernel(x, a_norm, a_w_in, a_q_norm, a_k_norm, a_w_out, b_norm, b_w_in, b_q_norm, b_k_norm, b_w_out,
           c_norm, c_w_in, c_gn, c_w_out, d_norm, d_w_in, d_f_bias, d_q_norm, d_k_norm, d_w_out):
    b_count, t, d = x.shape
    assert d == D_MODEL and t % TQ == 0 and t // 4 >= DSA_TOPK
    depth = 4
    norms = (a_norm, b_norm, c_norm, d_norm)
    h = x.reshape(b_count * t, d)
    hn = _rmsnorm(h, a_norm[0])
    for i in range(depth):
        m, j = i % 4, i // 4
        next_g = norms[(i + 1) % 4][(i + 1) // 4] if i + 1 < depth else None
        if m == 0:
            h, hn = _dsa_layer(h, hn, b_count, t, a_w_in[j], a_q_norm[j], a_k_norm[j], a_w_out[j], next_g)
        elif m == 1:
            h, hn = _moba_layer(h, hn, b_count, t, b_w_in[j], b_q_norm[j], b_k_norm[j], b_w_out[j], next_g)
        elif m == 2:
            h, hn = _ret_layer(h, hn, b_count, t, c_w_in[j], c_gn[j], c_w_out[j], next_g)
        else:
            h, hn = _fox_layer(h, hn, b_count, t, d_w_in[j], d_f_bias[j], d_q_norm[j], d_k_norm[j], d_w_out[j],
                               next_g)
    return h.reshape(b_count, t, d)
```
